```python
import math
import jax
import jax.numpy as jnp
from jax import lax
import numpy as np

D_MODEL = 1024
BATCH = 4
SEQ = 4096
DEPTH = 2
DEC_BATCH = 32
DEC_SEQ = 8
PAST_LEN = 8192
PAGE_SIZE = 128

D_MIX = D_MODEL
HD_R = 64
C_R = D_MIX // 4
H_R = C_R // HD_R
R_DECAY = 32
R_AAA = 32
R_GATE = 64
RWKV_LN_EPS = 64e-5
C_CONV = D_MIX // 4
CONV_W = 31
CONV_LN_EPS = 1e-5
HD_A = 64
C_A = D_MIX - C_R - C_CONV
H_A = C_A // HD_A
N_KV = 2
G_A = H_A // N_KV
H_I = H_A // 2
D_IDX = 64
TOPK_MAX = 256
Q_BLOCK = 128
N_BUCKETS = 32
MAX_DISTANCE = 128
D_FF = 2816
FFN_CONV_W = 3
D_PLE = 256
NORM_EPS = 1e-6
RWKV_SPLITS = (C_R, C_R, C_R, R_DECAY, R_AAA, R_GATE)
ATT_SPLITS = (C_A, N_KV * HD_A, N_KV * HD_A, H_I * D_IDX, D_IDX, H_I)
N_RWKV_COLS = 3 * C_R + R_DECAY + R_AAA + R_GATE
N_CONV_COLS = 2 * C_CONV
N_ATT_COLS = C_A + 2 * N_KV * HD_A + H_I * D_IDX + D_IDX + H_I
N_IN = N_RWKV_COLS + N_CONV_COLS + N_ATT_COLS
N_KEYS = 48

kernel_name = 'hybrid_rwkv7_conformer_dsa_step'


def split_cols(x, sizes):
    parts, start = [], 0
    for s in sizes:
        parts.append(x[..., start:start + s])
        start += s
    return parts


def rmsnorm(x, g):
    xf = x.astype(jnp.float32)
    y = xf * lax.rsqrt(jnp.mean(xf * xf, axis=-1, keepdims=True) + NORM_EPS)
    return (y * g.astype(jnp.float32)).astype(x.dtype)


def layernorm(x, g, b, eps):
    xf = x.astype(jnp.float32)
    mu = jnp.mean(xf, axis=-1, keepdims=True)
    var = jnp.mean(jnp.square(xf - mu), axis=-1, keepdims=True)
    return ((xf - mu) * lax.rsqrt(var + eps) * g + b).astype(x.dtype)


def causal_dwconv(x_pad, w, b):
    y = lax.conv_general_dilated(x_pad, w[:, None, :].astype(x_pad.dtype), window_strides=(1,), padding='VALID',
                                 dimension_numbers=('NWC', 'WIO', 'NWC'), feature_group_count=x_pad.shape[-1])
    return y + b


def _wkv7_step(S, inp):
    rt, wt, kt, vt, at, bt = inp
    sa = jnp.einsum('bhvk,bhk->bhv', S, at)
    S = S * wt[:, :, None, :] + sa[..., None] * bt[:, :, None, :] + vt[..., None] * kt[:, :, None, :]
    return S, jnp.einsum('bhvk,bhk->bhv', S, rt)


def rwkv7_mix(z_rw, shift0, wkv0, W, i):
    B, T, _ = z_rw.shape
    dt = z_rw.dtype
    z_prev = jnp.concatenate([shift0.astype(dt)[:, None], z_rw[:, :-1]], axis=1)
    zs = z_rw + (z_prev - z_rw) * W['mu_shift'][i]
    r, k, v, xw, xa, xg = split_cols(zs, RWKV_SPLITS)
    w_log = -jax.nn.softplus(-(W['w0'][i] + jnp.tanh(xw) @ W['w_lora'][i]).astype(jnp.float32)) - 0.5
    decay = jnp.exp(-jnp.exp(w_log)).astype(dt)
    a = jax.nn.sigmoid(W['a0'][i] + xa @ W['a_lora'][i])
    g = jax.nn.sigmoid(xg) @ W['g_lora'][i]
    heads = lambda t: t.reshape(B, T, H_R, HD_R)
    kk = heads(k * W['k_k'][i]).astype(jnp.float32)
    kk = (kk / jnp.maximum(jnp.sqrt(jnp.sum(kk * kk, axis=-1, keepdims=True)), 1e-12)).astype(dt)
    k = k * (1.0 + (a - 1.0) * W['k_a'][i])
    r4, k4, v4, a4 = heads(r), heads(k), heads(v), heads(a)
    seq = tuple(jnp.swapaxes(t, 0, 1) for t in (r4, heads(decay), k4, v4, -kk, kk * a4))
    S_T, y = lax.scan(_wkv7_step, wkv0.astype(dt), seq)
    yf = jnp.swapaxes(y, 0, 1).astype(jnp.float32)
    mu = jnp.mean(yf, axis=-1, keepdims=True)
    var = jnp.mean(jnp.square(yf - mu), axis=-1, keepdims=True)
    yn = ((yf - mu) * lax.rsqrt(var + RWKV_LN_EPS)).reshape(B, T, C_R) * W['lnx_g'][i] + W['lnx_b'][i]
    bonus = (jnp.sum(r4 * k4 * W['r_k'][i], axis=-1, keepdims=True) * v4).reshape(B, T, C_R)
    out = ((yn + bonus) * g).astype(dt)
    return out, S_T, z_rw[:, -1]


def conformer_conv_mix(z_glu, conv0, W, i):
    dt = z_glu.dtype
    u = z_glu[..., :C_CONV] * jax.nn.sigmoid(z_glu[..., C_CONV:])
    u_pad = jnp.concatenate([conv0.astype(dt), u], axis=1)
    c = causal_dwconv(u_pad, W['conv_w'][i], W['conv_b'][i])
    c = layernorm(c, W['conv_ln_g'][i], W['conv_ln_b'][i], CONV_LN_EPS)
    return jax.nn.silu(c), u_pad[:, -(CONV_W - 1):]


def rel_bucket(dist):
    max_exact = N_BUCKETS // 2
    d_f = jnp.maximum(dist, max_exact).astype(jnp.float32)
    large = max_exact + (jnp.log(d_f / max_exact) / math.log(MAX_DISTANCE / max_exact)
                         * (N_BUCKETS - max_exact)).astype(jnp.int32)
    return jnp.where(dist < max_exact, dist, jnp.minimum(large, N_BUCKETS - 1))


def dsa_block(q, qi, wi, q_pos, k_all, v_all, ki_all, rel_bias, top_k):
    B, Tq = q.shape[:2]
    L = k_all.shape[1]
    dots = jnp.einsum('bthd,bsd->bths', qi, ki_all, preferred_element_type=jnp.float32) * (D_IDX ** -0.5)
    score = jnp.einsum('bths,bth->bts', jax.nn.relu(dots), wi.astype(jnp.float32) * (H_I ** -0.5))
    key_pos = jnp.arange(L, dtype=jnp.int32)
    score = jnp.where(key_pos[None, None, :] <= q_pos[None, :, None], score, -jnp.inf)
    _, sel = lax.top_k(score, top_k)
    valid = sel <= q_pos[None, :, None]
    k_sel = jax.vmap(lambda kb, ib: kb[ib])(k_all, sel)
    v_sel = jax.vmap(lambda vb, ib: vb[ib])(v_all, sel)
    dist = jnp.maximum(q_pos[None, :, None] - sel, 0)
    bias = rel_bias[rel_bucket(dist)].astype(jnp.float32)
    bias = bias.reshape(B, Tq, top_k, N_KV, G_A).transpose(0, 1, 3, 4, 2)
    qg = q.reshape(B, Tq, N_KV, G_A, HD_A)
    logits = jnp.einsum('btngd,btknd->btngk', qg, k_sel, preferred_element_type=jnp.float32) * (HD_A ** -0.5) + bias
    logits = jnp.where(valid[:, :, None, None, :], logits, -jnp.inf)
    p = jax.nn.softmax(logits, axis=-1).astype(v_sel.dtype)
    o = jnp.einsum('btngk,btknd->btngd', p, v_sel)
    return o.reshape(B, Tq, C_A)


def dsa_attend(q, qi, wi, q_pos, k_all, v_all, ki_all, rel_bias, top_k):
    B, T = q.shape[:2]
    if T > Q_BLOCK and T % Q_BLOCK == 0:
        nb = T // Q_BLOCK
        blk = lambda t: jnp.swapaxes(t.reshape((B, nb, Q_BLOCK) + t.shape[2:]), 0, 1)
        xs = (blk(q), blk(qi), blk(wi), q_pos.reshape(nb, Q_BLOCK))
        out = lax.map(lambda a: dsa_block(a[0], a[1], a[2], a[3], k_all, v_all, ki_all, rel_bias, top_k), xs)
        return jnp.swapaxes(out, 0, 1).reshape(B, T, C_A)
    return dsa_block(q, qi, wi, q_pos, k_all, v_all, ki_all, rel_bias, top_k)


def dsa_mix(z_att, q_pos, past, W, i, top_k):
    B, T, _ = z_att.shape
    q, k, v, qi, ki, wi = split_cols(z_att, ATT_SPLITS)
    q = rmsnorm(q.reshape(B, T, H_A, HD_A), W['q_norm'][i])
    k = rmsnorm(k.reshape(B, T, N_KV, HD_A), W['k_norm'][i])
    v = v.reshape(B, T, N_KV, HD_A)
    ki = rmsnorm(ki, W['kidx_norm'][i])
    qi = qi.reshape(B, T, H_I, D_IDX)
    k_past, v_past, ki_past = past
    k_all = jnp.concatenate([k_past.astype(k.dtype), k], axis=1)
    v_all = jnp.concatenate([v_past.astype(v.dtype), v], axis=1)
    ki_all = jnp.concatenate([ki_past.astype(ki.dtype), ki], axis=1)
    o = dsa_attend(q, qi, wi, q_pos, k_all, v_all, ki_all, W['rel_bias'], top_k)
    return o, k, v, ki


def conv_ffn(h, ffn0, W, i):
    u = h @ W['ffn_up'][i]
    gate_pre, val = u[..., :D_FF], u[..., D_FF:]
    pad = jnp.concatenate([ffn0.astype(u.dtype), gate_pre], axis=1)
    gate = causal_dwconv(pad, W['ffn_conv_w'][i], W['ffn_conv_b'][i])
    out = (jax.nn.gelu(gate, approximate=False) * val) @ W['ffn_down'][i]
    return out, pad[:, -(FFN_CONV_W - 1):]


def decoder_layer(i, x, p_i, q_pos, st, past, W, top_k):
    shift0, wkv0, conv0, ffn0 = st
    h = rmsnorm(x, W['norm_mix'][i])
    z = h @ W['w_in'][i]
    o_rw, wkv_T, shift_T = rwkv7_mix(z[..., :N_RWKV_COLS], shift0, wkv0, W, i)
    o_cv, conv_T = conformer_conv_mix(z[..., N_RWKV_COLS:N_RWKV_COLS + N_CONV_COLS], conv0, W, i)
    o_at, k, v, ki = dsa_mix(z[..., N_RWKV_COLS + N_CONV_COLS:], q_pos, past, W, i, top_k)
    x = x + jnp.concatenate([o_rw, o_cv.astype(o_rw.dtype), o_at.astype(o_rw.dtype)], axis=-1) @ W['w_out'][i]
    f, ffn_T = conv_ffn(rmsnorm(x, W['norm_ffn'][i]), ffn0, W, i)
    x = x + f
    gate = jax.nn.sigmoid(rmsnorm(x, W['ple_norm'][i]) @ W['ple_gate'][i])
    x = x + (p_i @ W['ple_proj'][i]) * gate
    return x, (k, v, ki, wkv_T, shift_T, conv_T, ffn_T)


def stack_layers(new_list):
    return tuple(jnp.stack(t) for t in zip(*new_list))


def setup_inputs(seed: int = 0) -> dict:
    key = jax.random.key(seed)
    ks = iter([jax.random.fold_in(key, j) for j in range(N_KEYS)])

    def nrm(shape, scale):
        return jax.random.normal(next(ks), shape, jnp.float32) * scale

    def unif(shape, lo, hi):
        return jax.random.uniform(next(ks), shape, jnp.float32, lo, hi)

    n_pages = PAST_LEN // PAGE_SIZE
    n_pool = (DEC_BATCH * n_pages * 5) // 4
    perm = jax.random.permutation(next(ks), n_pool)
    page_table = perm[:DEC_BATCH * n_pages].reshape(DEC_BATCH, n_pages).astype(jnp.int32)
    return dict(
        x_prompt=nrm((BATCH, SEQ, D_MODEL), 1.0),
        x_sample=nrm((DEC_BATCH, DEC_SEQ, D_MODEL), 1.0),
        cache_k=nrm((DEPTH, n_pool, PAGE_SIZE, N_KV, HD_A), 1.0),
        cache_v=nrm((DEPTH, n_pool, PAGE_SIZE, N_KV, HD_A), 1.0),
        cache_kidx=nrm((DEPTH, n_pool, PAGE_SIZE, D_IDX), 1.0),
        state_wkv=nrm((DEPTH, DEC_BATCH, H_R, HD_R, HD_R), 1.0),
        state_shift=nrm((DEPTH, DEC_BATCH, N_RWKV_COLS), 1.0),
        state_conv=nrm((DEPTH, DEC_BATCH, CONV_W - 1, C_CONV), 0.5),
        state_ffn=nrm((DEPTH, DEC_BATCH, FFN_CONV_W - 1, D_FF), 1.0),
        page_table=page_table,
        p_prompt=nrm((DEPTH, BATCH, SEQ, D_PLE), 1.0),
        p_sample=nrm((DEPTH, DEC_BATCH, DEC_SEQ, D_PLE), 1.0),
        norm_mix=1.0 + nrm((DEPTH, D_MODEL), 0.05),
        w_in=nrm((DEPTH, D_MODEL, N_IN), D_MODEL ** -0.5),
        w_out=nrm((DEPTH, D_MIX, D_MODEL), D_MIX ** -0.5),
        mu_shift=unif((DEPTH, N_RWKV_COLS), 0.0, 1.0),
        w0=unif((DEPTH, C_R), -6.0, 1.0),
        w_lora=nrm((DEPTH, R_DECAY, C_R), 0.1),
        a0=nrm((DEPTH, C_R), 0.5),
        a_lora=nrm((DEPTH, R_AAA, C_R), R_AAA ** -0.5),
        g_lora=nrm((DEPTH, R_GATE, C_R), R_GATE ** -0.5),
        k_k=0.85 + nrm((DEPTH, C_R), 0.05),
        k_a=1.0 + nrm((DEPTH, C_R), 0.05),
        r_k=nrm((DEPTH, H_R, HD_R), 0.1),
        lnx_g=1.0 + nrm((DEPTH, C_R), 0.05),
        lnx_b=nrm((DEPTH, C_R), 0.02),
        conv_w=nrm((DEPTH, CONV_W, C_CONV), CONV_W ** -0.5),
        conv_b=nrm((DEPTH, C_CONV), 0.02),
        conv_ln_g=1.0 + nrm((DEPTH, C_CONV), 0.05),
        conv_ln_b=nrm((DEPTH, C_CONV), 0.02),
        q_norm=1.0 + nrm((DEPTH, HD_A), 0.05),
        k_norm=1.0 + nrm((DEPTH, HD_A), 0.05),
        kidx_norm=1.0 + nrm((DEPTH, D_IDX), 0.05),
        rel_bias=nrm((N_BUCKETS, H_A), 0.5),
        norm_ffn=1.0 + nrm((DEPTH, D_MODEL), 0.05),
        ffn_up=nrm((DEPTH, D_MODEL, 2 * D_FF), D_MODEL ** -0.5),
        ffn_conv_w=nrm((DEPTH, FFN_CONV_W, D_FF), FFN_CONV_W ** -0.5),
        ffn_conv_b=nrm((DEPTH, D_FF), 0.02),
        ffn_down=nrm((DEPTH, D_FF, D_MODEL), D_FF ** -0.5),
        ple_norm=1.0 + nrm((DEPTH, D_MODEL), 0.05),
        ple_proj=nrm((DEPTH, D_PLE, D_MODEL), D_PLE ** -0.5),
        ple_gate=nrm((DEPTH, D_MODEL, D_MODEL), D_MODEL ** -0.5),
    )


def reference(x_prompt, x_sample, cache_k, cache_v, cache_kidx, state_wkv, state_shift, state_conv, state_ffn,
              page_table, p_prompt, p_sample, norm_mix, w_in, w_out, mu_shift, w0, w_lora, a0, a_lora, g_lora,
              k_k, k_a, r_k, lnx_g, lnx_b, conv_w, conv_b, conv_ln_g, conv_ln_b, q_norm, k_norm, kidx_norm,
              rel_bias, norm_ffn, ffn_up, ffn_conv_w, ffn_conv_b, ffn_down, ple_norm, ple_proj, ple_gate):
    W = dict(norm_mix=norm_mix, w_in=w_in, w_out=w_out, mu_shift=mu_shift, w0=w0, w_lora=w_lora, a0=a0,
             a_lora=a_lora, g_lora=g_lora, k_k=k_k, k_a=k_a, r_k=r_k, lnx_g=lnx_g, lnx_b=lnx_b,
             conv_w=conv_w, conv_b=conv_b, conv_ln_g=conv_ln_g, conv_ln_b=conv_ln_b, q_norm=q_norm,
             k_norm=k_norm, kidx_norm=kidx_norm, rel_bias=rel_bias, norm_ffn=norm_ffn, ffn_up=ffn_up,
             ffn_conv_w=ffn_conv_w, ffn_conv_b=ffn_conv_b, ffn_down=ffn_down, ple_norm=ple_norm,
             ple_proj=ple_proj, ple_gate=ple_gate)

    B, T, _ = x_prompt.shape
    dt = x_prompt.dtype
    pos_p = jnp.arange(T, dtype=jnp.int32)
    topk_p = min(TOPK_MAX, T // 4)
    st0 = (jnp.zeros((B, N_RWKV_COLS), dt), jnp.zeros((B, H_R, HD_R, HD_R), dt),
           jnp.zeros((B, CONV_W - 1, C_CONV), dt), jnp.zeros((B, FFN_CONV_W - 1, D_FF), dt))
    past0 = (jnp.zeros((B, 0, N_KV, HD_A), dt), jnp.zeros((B, 0, N_KV, HD_A), dt), jnp.zeros((B, 0, D_IDX), dt))
    x = x_prompt
    new_p = []
    for i in range(DEPTH):
        x, new = decoder_layer(i, x, p_prompt[i], pos_p, st0, past0, W, topk_p)
        new_p.append(new)
    y_prompt = x
    k_p, v_p, ki_p, wkv_p, shift_p, conv_p, ffn_p = stack_layers(new_p)

    Bs, Ts, _ = x_sample.shape
    past_len = page_table.shape[1] * cache_k.shape[2]
    pos_s = past_len + jnp.arange(Ts, dtype=jnp.int32)
    topk_s = min(TOPK_MAX, (past_len + Ts) // 4)
    x = x_sample
    new_s = []
    for i in range(DEPTH):
        past = tuple(c[i][page_table].reshape((Bs, past_len) + c.shape[3:]) for c in (cache_k, cache_v, cache_kidx))
        st = (state_shift[i], state_wkv[i], state_conv[i], state_ffn[i])
        x, new = decoder_layer(i, x, p_sample[i], pos_s, st, past, W, topk_s)
        new_s.append(new)
    y_sample = x
    k_s, v_s, ki_s, wkv_s, shift_s, conv_s, ffn_s = stack_layers(new_s)

    return (y_prompt, y_sample, k_p, v_p, ki_p, wkv_p, shift_p, conv_p, ffn_p,
            k_s, v_s, ki_s, wkv_s, shift_s, conv_s, ffn_s)
```

```python
import functools
import math

import numpy as np
import jax
import jax.numpy as jnp
from jax import lax
from jax.experimental import pallas as pl
from jax.experimental.pallas import tpu as pltpu

F32 = jnp.float32
BF16 = jnp.bfloat16
I32 = jnp.int32

HD = 64
C_R = 256
H_R = C_R // HD
C_CONV = 256
CONV_W = 31
C_A = 512
H_A = C_A // HD
N_KV = 2
G_A = H_A // N_KV
H_I = 4
D_IDX = 64
TOPK_MAX = 256
N_BUCKETS = 32
MAX_DISTANCE = 128
D_FF = 2816
NORM_EPS = 1e-6
RWKV_LN_EPS = 64e-5
CONV_LN_EPS = 1e-5
N_RWKV_COLS = 896
N_IN = 2500

LANES = 128
SUBLANES = 8
VMEM_LIMIT = 56 * 1024 * 1024

NZ = 2688
ZC_CONV = 1024 // 512
ZC_Q = 1536 // 512
ZC_K = 2048 // 128
ZC_V = 2176 // 128
ZC_QI = 2304 // 256
ZC_KW = 2560 // 128

HIST = 32
NEG = -1e30
INT_MIN = -2 ** 31
KEY_NEG_INF = int(np.array(0xFF800000 ^ 0x7FFFFFFF, dtype=np.uint32).view(np.int32))


def _cparams(sem):
    return pltpu.CompilerParams(dimension_semantics=sem, vmem_limit_bytes=VMEM_LIMIT)


def _split2(x):
    hi = x.astype(BF16)
    lo = (x - hi.astype(F32)).astype(BF16)
    return hi, lo


def _dot(a, b):
    return jnp.dot(a, b, preferred_element_type=F32)


def _dot_nt(a, b):
    return lax.dot_general(a, b, (((1,), (1,)), ((), ())), preferred_element_type=F32)


def _dot_exactw(x, w_bf):
    hi, lo = _split2(x)
    return _dot(hi, w_bf) + _dot(lo, w_bf)


def _dot_hl(x, w_hi, w_lo):
    hi, lo = _split2(x)
    return _dot(hi, w_hi) + (_dot(lo, w_hi) + _dot(hi, w_lo))


def _sigmoid(x):
    return 1.0 / (1.0 + jnp.exp(-x))


def _rms(x, g):
    ms = jnp.mean(x * x, axis=-1, keepdims=True)
    return x * lax.rsqrt(ms + NORM_EPS) * g


def _in_proj_kernel(x_ref, g_ref, w_ref, z_ref, xn_s):
    @pl.when(pl.program_id(1) == 0)
    def _():
        xn_s[...] = _rms(x_ref[...], g_ref[...]).astype(BF16)

    z_ref[...] = _dot(xn_s[...], w_ref[...])


def _in_proj(x2, g, w_bf, tm):
    M, D = x2.shape
    tn = 896
    return pl.pallas_call(
        _in_proj_kernel,
        grid=(M // tm, NZ // tn),
        in_specs=[pl.BlockSpec((tm, D), lambda i, j: (i, 0)),
                  pl.BlockSpec((1, D), lambda i, j: (0, 0)),
                  pl.BlockSpec((D, tn), lambda i, j: (0, j))],
        out_specs=pl.BlockSpec((tm, tn), lambda i, j: (i, j)),
        out_shape=jax.ShapeDtypeStruct((M, NZ), F32),
        scratch_shapes=[pltpu.VMEM((tm, D), BF16)],
        compiler_params=_cparams(("parallel", "arbitrary")),
        name="in_proj",
    )(x2, g, w_bf)


def _rwkv_kernel(z_ref, sh0_ref, wkv0_ref, mu_ref, w0_ref, wl_hi, wl_lo, a0_ref, al_hi, al_lo, gl_hi, gl_lo,
                 kk_ref, ka_ref, rk_ref, lg_ref, lb_ref, gseg_ref,
                 o_ref, st_ref,
                 S_s, carry_s, w_s, k_s, v_s, a_s, b_s, r_s, yT_s, bon_s, g_s, *, bb, Tc):
    c = pl.program_id(1)
    LW = yT_s.shape[-1] if Tc < LANES else LANES

    @pl.when(c == 0)
    def _():
        S_s[...] = wkv0_ref[...]
        carry_s[...] = sh0_ref[...]

    if Tc % LANES != 0:
        yT_s[...] = jnp.zeros(yT_s.shape, F32)

    gseg = gseg_ref[...]
    mu = mu_ref[...]
    rid = lax.broadcasted_iota(I32, (Tc, N_RWKV_COLS), 0)
    for b in range(bb):
        z = z_ref[b]
        prev = jnp.where(rid == 0, carry_s[b], pltpu.roll(z, 1, 0))
        carry_s[b] = z[Tc - 1:Tc, :]
        zs = z + (prev - z) * mu
        r = zs[:, 0:C_R]
        k = zs[:, C_R:2 * C_R]
        v = zs[:, 2 * C_R:3 * C_R]
        t6 = zs[:, 3 * C_R:]
        lw = w0_ref[...] + _dot_hl(jnp.tanh(t6), wl_hi[...], wl_lo[...])
        nl = -lw
        softplus = jnp.maximum(nl, 0.0) + jnp.log(1.0 + jnp.exp(-jnp.abs(nl)))
        decay = jnp.exp(-jnp.exp(-softplus - 0.5))
        a = _sigmoid(a0_ref[...] + _dot_hl(t6, al_hi[...], al_lo[...]))
        g_s[b] = _dot_hl(_sigmoid(t6), gl_hi[...], gl_lo[...])
        kk = k * kk_ref[...]
        nrm = jnp.sqrt(_dot_exactw(kk * kk, gseg))
        kk = kk / jnp.maximum(nrm, 1e-12)
        k2 = k * (1.0 + (a - 1.0) * ka_ref[...])
        bon_s[b] = _dot_exactw(r * k2 * rk_ref[...], gseg) * v
        na = -kk
        kb = kk * a
        for h in range(H_R):
            sl = slice(h * HD, (h + 1) * HD)
            w_s[b, h] = decay[:, sl]
            k_s[b, h] = k2[:, sl]
            v_s[b, h] = v[:, sl]
            a_s[b, h] = na[:, sl]
            b_s[b, h] = kb[:, sl]
            r_s[b, h] = r[:, sl]

    eye = (lax.broadcasted_iota(I32, (HD, HD), 0) == lax.broadcasted_iota(I32, (HD, HD), 1)).astype(F32)
    lane = lax.broadcasted_iota(I32, (HD, LW), 1)

    def sub(i, carry):
        t0 = pl.multiple_of(i * SUBLANES, SUBLANES)
        tile0 = pl.multiple_of((t0 // LW) * LW, LW)
        lbase = t0 - tile0
        for b in range(bb):
            for h in range(H_R):
                S = S_s[b, h]
                yt = yT_s[b, h, :, pl.ds(tile0, LW)]
                for j in range(SUBLANES):
                    t = t0 + j
                    wv = w_s[b, h, pl.ds(t, 1), :]
                    kv = k_s[b, h, pl.ds(t, 1), :]
                    vv = v_s[b, h, pl.ds(t, 1), :]
                    av = a_s[b, h, pl.ds(t, 1), :]
                    bv = b_s[b, h, pl.ds(t, 1), :]
                    rv = r_s[b, h, pl.ds(t, 1), :]
                    sa = jnp.sum(S * av, axis=1, keepdims=True)
                    vc = jnp.sum(eye * vv, axis=1, keepdims=True)
                    S = S * wv + sa * bv + vc * kv
                    y = jnp.sum(S * rv, axis=1, keepdims=True)
                    yt = jnp.where(lane == lbase + j, y, yt)
                S_s[b, h] = S
                yT_s[b, h, :, pl.ds(tile0, LW)] = yt
        return carry

    lax.fori_loop(0, Tc // SUBLANES, sub, 0)

    for b in range(bb):
        ys = [jnp.transpose(yT_s[b, h])[:Tc] for h in range(H_R)]
        y = jnp.concatenate(ys, axis=-1)
        mean = _dot_exactw(y, gseg) * (1.0 / HD)
        d = y - mean
        var = _dot_exactw(d * d, gseg) * (1.0 / HD)
        yn = d * lax.rsqrt(var + RWKV_LN_EPS) * lg_ref[...] + lb_ref[...]
        o_ref[b] = (yn + bon_s[b]) * g_s[b]
    st_ref[...] = S_s[...]


def _rwkv(z3, shift0, wkv0, P, bb, Tc):
    B, T, _ = z3.shape
    Tcp = max(Tc, LANES)
    vec = lambda n: pl.BlockSpec((1, n), lambda bi, c: (0, 0))
    mat = lambda r, n: pl.BlockSpec((r, n), lambda bi, c: (0, 0))
    kern = functools.partial(_rwkv_kernel, bb=bb, Tc=Tc)
    hv = lambda: pltpu.VMEM((bb, H_R, Tc, HD), F32)
    return pl.pallas_call(
        kern,
        grid=(B // bb, T // Tc),
        in_specs=[pl.BlockSpec((bb, Tc, N_RWKV_COLS), lambda bi, c: (bi, c, 0)),
                  pl.BlockSpec((bb, 1, N_RWKV_COLS), lambda bi, c: (bi, 0, 0)),
                  pl.BlockSpec((bb, H_R, HD, HD), lambda bi, c: (bi, 0, 0, 0)),
                  vec(N_RWKV_COLS), vec(C_R), mat(LANES, C_R), mat(LANES, C_R), vec(C_R), mat(LANES, C_R),
                  mat(LANES, C_R), mat(LANES, C_R), mat(LANES, C_R),
                  vec(C_R), vec(C_R), vec(C_R), vec(C_R), vec(C_R), mat(C_R, C_R)],
        out_specs=[pl.BlockSpec((bb, Tc, C_R), lambda bi, c: (bi, c, 0)),
                   pl.BlockSpec((bb, H_R, HD, HD), lambda bi, c: (bi, 0, 0, 0))],
        out_shape=[jax.ShapeDtypeStruct((B, T, C_R), F32),
                   jax.ShapeDtypeStruct((B, H_R, HD, HD), F32)],
        scratch_shapes=[pltpu.VMEM((bb, H_R, HD, HD), F32),
                        pltpu.VMEM((bb, 1, N_RWKV_COLS), F32),
                        hv(), hv(), hv(), hv(), hv(), hv(),
                        pltpu.VMEM((bb, H_R, HD, Tcp), F32),
                        pltpu.VMEM((bb, Tc, C_R), F32),
                        pltpu.VMEM((bb, Tc, C_R), F32)],
        compiler_params=_cparams(("parallel", "arbitrary")),
        name="rwkv7",
    )(z3, shift0, wkv0, P["mu"], P["w0"], P["wl_hi"], P["wl_lo"], P["a0"], P["al_hi"], P["al_lo"],
      P["gl_hi"], P["gl_lo"], P["k_k"], P["k_a"], P["r_k"], P["lnx_g"], P["lnx_b"], P["gseg4"])


def _conv_kernel(z_ref, c0_ref, w_ref, b_ref, lg_ref, lb_ref, o_ref, ct_ref, buf, *, Tc):
    @pl.when(pl.program_id(1) == 0)
    def _():
        buf[0:HIST, :] = c0_ref[0]

    z = z_ref[0]
    u = z[:, 0:C_CONV] * _sigmoid(z[:, C_CONV:])
    buf[HIST:HIST + Tc, :] = u
    w = w_ref[...]
    acc = jnp.zeros((Tc, C_CONV), F32) + b_ref[...]
    off = HIST - (CONV_W - 1)
    for j in range(CONV_W):
        acc = acc + w[j:j + 1, :] * buf[off + j:off + j + Tc, :]
    mu = jnp.mean(acc, axis=-1, keepdims=True)
    d = acc - mu
    var = jnp.mean(d * d, axis=-1, keepdims=True)
    cn = d * lax.rsqrt(var + CONV_LN_EPS) * lg_ref[...] + lb_ref[...]
    o_ref[0] = cn * _sigmoid(cn)
    hist = buf[Tc:Tc + HIST, :]
    buf[0:HIST, :] = hist
    ct_ref[0] = hist


def _conv(z3, conv0p, P, Tc):
    B, T, _ = z3.shape
    vec = pl.BlockSpec((1, C_CONV), lambda b, c: (0, 0))
    return pl.pallas_call(
        functools.partial(_conv_kernel, Tc=Tc),
        grid=(B, T // Tc),
        in_specs=[pl.BlockSpec((1, Tc, 2 * C_CONV), lambda b, c: (b, c, ZC_CONV)),
                  pl.BlockSpec((1, HIST, C_CONV), lambda b, c: (b, 0, 0)),
                  pl.BlockSpec((CONV_W, C_CONV), lambda b, c: (0, 0)),
                  vec, vec, vec],
        out_specs=[pl.BlockSpec((1, Tc, C_CONV), lambda b, c: (b, c, 0)),
                   pl.BlockSpec((1, HIST, C_CONV), lambda b, c: (b, 0, 0))],
        out_shape=[jax.ShapeDtypeStruct((B, T, C_CONV), F32),
                   jax.ShapeDtypeStruct((B, HIST, C_CONV), F32)],
        scratch_shapes=[pltpu.VMEM((HIST + Tc, C_CONV), F32)],
        compiler_params=_cparams(("parallel", "arbitrary")),
        name="conformer_conv",
    )(z3, conv0p, P["conv_w"], P["conv_b"], P["conv_ln_g"], P["conv_ln_b"])


def _att_prep_kernel(k_ref, kw_ref, kg_ref, kig_ref, gseg_ref, kn_ref, kin_ref):
    k = k_ref[...]
    ms = _dot_exactw(k * k, gseg_ref[...]) * (1.0 / HD)
    kn_ref[...] = k * lax.rsqrt(ms + NORM_EPS) * kg_ref[...]
    kw = kw_ref[...]
    ki = kw[:, 0:D_IDX]
    msi = jnp.mean(ki * ki, axis=-1, keepdims=True)
    kin_ref[...] = ki * lax.rsqrt(msi + NORM_EPS) * kig_ref[...]


def _att_prep(z2, P, tm):
    M = z2.shape[0]
    return pl.pallas_call(
        _att_prep_kernel,
        grid=(M // tm,),
        in_specs=[pl.BlockSpec((tm, LANES), lambda i: (i, ZC_K)),
                  pl.BlockSpec((tm, LANES), lambda i: (i, ZC_KW)),
                  pl.BlockSpec((1, LANES), lambda i: (0, 0)),
                  pl.BlockSpec((1, D_IDX), lambda i: (0, 0)),
                  pl.BlockSpec((LANES, LANES), lambda i: (0, 0))],
        out_specs=[pl.BlockSpec((tm, LANES), lambda i: (i, 0)),
                   pl.BlockSpec((tm, D_IDX), lambda i: (i, 0))],
        out_shape=[jax.ShapeDtypeStruct((M, N_KV * HD), F32),
                   jax.ShapeDtypeStruct((M, D_IDX), F32)],
        compiler_params=_cparams(("parallel",)),
        name="att_prep",
    )(z2, z2, P["k_norm2"], P["kidx_norm"], P["gseg2"])


def _bias_kernel(rb_ref, idx_ref, o_ref):
    h = pl.program_id(0) % H_A
    idx = idx_ref[0]
    out = jnp.zeros(idx.shape, F32)
    for bk in range(N_BUCKETS):
        out = jnp.where(idx == bk, rb_ref[bk, h], out)
    o_ref[0] = out


def _bias_expand(rel_bias, idx):
    S, R, L = idx.shape
    return pl.pallas_call(
        _bias_kernel,
        grid=(S,),
        in_specs=[pl.BlockSpec(memory_space=pltpu.SMEM),
                  pl.BlockSpec((1, R, L), lambda s: (s, 0, 0))],
        out_specs=pl.BlockSpec((1, R, L), lambda s: (s, 0, 0)),
        out_shape=jax.ShapeDtypeStruct((S, R, L), F32),
        compiler_params=_cparams(("parallel",)),
        name="rel_bias_tiles",
    )(rel_bias, idx)


def _rel_bucket_np(dist):
    max_exact = N_BUCKETS // 2
    d_f = np.maximum(dist, max_exact).astype(np.float32)
    large = max_exact + (np.log(d_f / np.float32(max_exact)) / np.float32(math.log(MAX_DISTANCE / max_exact))
                         * np.float32(N_BUCKETS - max_exact)).astype(np.int32)
    return np.where(dist < max_exact, dist, np.minimum(large, N_BUCKETS - 1)).astype(np.int32)


def _sort_key(s):
    bits = pltpu.bitcast(s + 0.0, I32)
    return jnp.where(bits < 0, bits ^ 0x7FFFFFFF, bits)


def _q_prep(q, qg, gseg8):
    ms = _dot_exactw(q * q, gseg8) * (1.0 / HD)
    return (q * lax.rsqrt(ms + NORM_EPS) * qg * (HD ** -0.5)).astype(BF16)


def _attp_kernel(q_ref, qi_ref, kw_ref, kn_ref, v_ref, kin_ref, qg_ref, bias_ref, gseg_ref, tri_ref, ones_ref,
                 o_ref,
                 kk_s, vv_s, ki_s, qh_s, qi_s, wb_s, key_s, madd_s, *, tq, topk):
    j = pl.program_id(1)
    nb = j + 1

    @pl.when(j == 0)
    def _():
        kn = kn_ref[0]
        v = v_ref[0]
        for n in range(N_KV):
            kk_s[n] = kn[:, n * HD:(n + 1) * HD].astype(BF16)
            vv_s[n] = v[:, n * HD:(n + 1) * HD].astype(BF16)
        ki_s[...] = kin_ref[0].astype(BF16)

    qn = _q_prep(q_ref[0], qg_ref[...], gseg_ref[...])
    for h in range(H_A):
        qh_s[h] = qn[:, h * HD:(h + 1) * HD]
    qi = (qi_ref[0] * (D_IDX ** -0.5)).astype(BF16)
    kw = kw_ref[0]
    for h in range(H_I):
        qi_s[h] = qi[:, h * D_IDX:(h + 1) * D_IDX]
        wb_s[h] = jnp.broadcast_to(kw[:, D_IDX + h:D_IDX + h + 1] * (H_I ** -0.5), (tq, LANES))

    row = lax.broadcasted_iota(I32, (tq, LANES), 0) + j * tq
    col0 = lax.broadcasted_iota(I32, (tq, LANES), 1)

    def idx_body(c, carry):
        off = pl.multiple_of(c * LANES, LANES)
        kc = ki_s[pl.ds(off, LANES), :]
        s = jnp.zeros((tq, LANES), F32)
        for h in range(H_I):
            s = s + jnp.maximum(_dot_nt(qi_s[h], kc), 0.0) * wb_s[h]
        s = jnp.where(col0 + off <= row, s, -jnp.inf)
        key_s[:, pl.ds(off, LANES)] = _sort_key(s)
        return carry

    lax.fori_loop(0, nb, idx_body, 0)

    def count(pred):
        def body(c, acc):
            off = pl.multiple_of(c * LANES, LANES)
            return acc + jnp.where(pred(key_s[:, pl.ds(off, LANES)]), 1.0, 0.0)
        acc = lax.fori_loop(0, nb, body, jnp.zeros((tq, LANES), F32))
        return jnp.broadcast_to(jnp.sum(acc, axis=1, keepdims=True), (tq, LANES))

    def bit_body(it, cur):
        cand = cur + lax.shift_left(jnp.int32(1), (31 - it).astype(I32))
        cnt = count(lambda key: key >= cand)
        return jnp.where(cnt >= float(topk), cand, cur)

    tau = lax.fori_loop(0, 32, bit_body, jnp.full((tq, LANES), INT_MIN, I32))
    need = float(topk) - count(lambda key: key > tau)
    tri = tri_ref[...]
    ones = ones_ref[...]

    def mask_body(c, run):
        off = pl.multiple_of(c * LANES, LANES)
        key = key_s[:, pl.ds(off, LANES)]
        eq = jnp.where(key == tau, jnp.where(key > KEY_NEG_INF, 1.0, 0.0), 0.0)
        eqb = eq.astype(BF16)
        pre = _dot(eqb, tri) + run
        take = jnp.where(key > tau, 1.0, jnp.where(pre <= need, eq, 0.0))
        take = jnp.where(key > KEY_NEG_INF, take, 0.0)
        madd_s[:, pl.ds(off, LANES)] = jnp.where(take > 0.5, 0.0, NEG)
        return run + _dot(eqb, ones)

    lax.fori_loop(0, nb, mask_body, jnp.zeros((tq, LANES), F32))

    outs = []
    for h in range(H_A):
        n = h // G_A
        qh = qh_s[h]

        def body(c, carry, n=n, h=h, qh=qh):
            m, l, acc = carry
            off = pl.multiple_of(c * LANES, LANES)
            kc = kk_s[n, pl.ds(off, LANES), :]
            vc = vv_s[n, pl.ds(off, LANES), :]
            dd = jnp.minimum(j - c, 2)
            s = _dot_nt(qh, kc) + bias_ref[dd, h] + madd_s[:, pl.ds(off, LANES)]
            mn = jnp.maximum(m, jnp.max(s, axis=1, keepdims=True))
            p = jnp.exp(s - mn)
            al = jnp.exp(m - mn)
            l = al * l + jnp.sum(p, axis=1, keepdims=True)
            acc = al * acc + _dot(p.astype(BF16), vc)
            return mn, l, acc

        m, l, acc = lax.fori_loop(0, nb, body, (jnp.full((tq, 1), NEG, F32), jnp.zeros((tq, 1), F32),
                                                jnp.zeros((tq, HD), F32)))
        outs.append(acc / l)
    o_ref[0] = jnp.concatenate(outs, axis=-1)


def _att_prompt(z3, kn3, kin3, P, bias_p, topk, tq):
    B, T, _ = z3.shape
    cst = lambda shp: pl.BlockSpec(shp, lambda b, j: tuple(0 for _ in shp))
    return pl.pallas_call(
        functools.partial(_attp_kernel, tq=tq, topk=topk),
        grid=(B, T // tq),
        in_specs=[pl.BlockSpec((1, tq, C_A), lambda b, j: (b, j, ZC_Q)),
                  pl.BlockSpec((1, tq, H_I * D_IDX), lambda b, j: (b, j, ZC_QI)),
                  pl.BlockSpec((1, tq, LANES), lambda b, j: (b, j, ZC_KW)),
                  pl.BlockSpec((1, T, N_KV * HD), lambda b, j: (b, 0, 0)),
                  pl.BlockSpec((1, T, N_KV * HD), lambda b, j: (b, 0, ZC_V)),
                  pl.BlockSpec((1, T, D_IDX), lambda b, j: (b, 0, 0)),
                  cst((1, C_A)), cst((3, H_A, tq, LANES)), cst((C_A, C_A)), cst((LANES, LANES)), cst((LANES, LANES))],
        out_specs=pl.BlockSpec((1, tq, C_A), lambda b, j: (b, j, 0)),
        out_shape=jax.ShapeDtypeStruct((B, T, C_A), F32),
        scratch_shapes=[pltpu.VMEM((N_KV, T, HD), BF16),
                        pltpu.VMEM((N_KV, T, HD), BF16),
                        pltpu.VMEM((T, D_IDX), BF16),
                        pltpu.VMEM((H_A, tq, HD), BF16),
                        pltpu.VMEM((H_I, tq, D_IDX), BF16),
                        pltpu.VMEM((H_I, tq, LANES), F32),
                        pltpu.VMEM((tq, T), I32),
                        pltpu.VMEM((tq, T), F32)],
        compiler_params=_cparams(("parallel", "arbitrary")),
        name="dsa_prompt",
    )(z3, z3, z3, kn3, z3, kin3, P["q_norm8"], bias_p, P["gseg8"], P["tri"], P["ones"])


def _atts_kernel(pt_ref, ck_ref, cv_ref, cki_ref, q_ref, qi_ref, kw_ref, kn_ref, vn_ref, kin_ref, qg_ref, bias_ref,
                 gseg_ref, tri_ref, ones_ref,
                 o_ref,
                 kk_s, vv_s, ki_s, *, npages, Ts, topk):
    p = pl.program_id(1)
    P = LANES
    L = (npages + 1) * P

    @pl.when(p < npages)
    def _():
        off = pl.multiple_of(p * P, P)
        kp = ck_ref[...]
        vp = cv_ref[...]
        for n in range(N_KV):
            kk_s[n, pl.ds(off, P), :] = kp[:, n * HD:(n + 1) * HD].astype(BF16)
            vv_s[n, pl.ds(off, P), :] = vp[:, n * HD:(n + 1) * HD].astype(BF16)
        ki_s[pl.ds(off, P), :] = cki_ref[...].astype(BF16)

    @pl.when(p == npages)
    def _():
        zpad = jnp.zeros((P - Ts, N_KV * HD), F32)
        kt = jnp.concatenate([kn_ref[0], zpad], axis=0)
        vt = jnp.concatenate([vn_ref[0], zpad], axis=0)
        for n in range(N_KV):
            kk_s[n, npages * P:L, :] = kt[:, n * HD:(n + 1) * HD].astype(BF16)
            vv_s[n, npages * P:L, :] = vt[:, n * HD:(n + 1) * HD].astype(BF16)
        ki_s[npages * P:L, :] = jnp.concatenate([kin_ref[0], jnp.zeros((P - Ts, D_IDX), F32)], axis=0).astype(BF16)

        qn = _q_prep(q_ref[0], qg_ref[...], gseg_ref[...])
        qi = (qi_ref[0] * (D_IDX ** -0.5)).astype(BF16)
        kw = kw_ref[0]
        kia = ki_s[...]
        s = jnp.zeros((Ts, L), F32)
        for h in range(H_I):
            d = _dot_nt(qi[:, h * D_IDX:(h + 1) * D_IDX], kia)
            s = s + jnp.maximum(d, 0.0) * (kw[:, D_IDX + h:D_IDX + h + 1] * (H_I ** -0.5))
        col = lax.broadcasted_iota(I32, (Ts, L), 1)
        row = lax.broadcasted_iota(I32, (Ts, L), 0) + npages * P
        key = _sort_key(jnp.where(col <= row, s, -jnp.inf))

        def bit_body(it, cur):
            cand = cur + lax.shift_left(jnp.int32(1), (31 - it).astype(I32))
            cnt = jnp.sum(jnp.where(key >= cand, 1.0, 0.0), axis=1, keepdims=True)
            return jnp.where(cnt >= float(topk), cand, cur)

        tau = lax.fori_loop(0, 32, bit_body, jnp.full((Ts, 1), INT_MIN, I32))
        need = float(topk) - jnp.sum(jnp.where(key > tau, 1.0, 0.0), axis=1, keepdims=True)
        tri = tri_ref[...]
        ones = ones_ref[...]
        eq = jnp.where(key == tau, jnp.where(key > KEY_NEG_INF, 1.0, 0.0), 0.0)
        run = jnp.zeros((Ts, LANES), F32)
        pieces = []
        for c in range(L // LANES):
            eqc = eq[:, c * LANES:(c + 1) * LANES]
            eqb = eqc.astype(BF16)
            pre = _dot(eqb, tri) + run
            pieces.append(jnp.where(pre <= need, eqc, 0.0))
            run = run + _dot(eqb, ones)
        take = jnp.where(key > tau, 1.0, jnp.concatenate(pieces, axis=-1))
        take = jnp.where(key > KEY_NEG_INF, take, 0.0)
        madd = jnp.where(take > 0.5, 0.0, NEG)
        madd4 = jnp.concatenate([madd] * G_A, axis=0)

        outs = []
        for n in range(N_KV):
            qs = jnp.concatenate([qn[:, (n * G_A + g) * HD:(n * G_A + g + 1) * HD] for g in range(G_A)], axis=0)
            sc = _dot_nt(qs, kk_s[n]) + bias_ref[n] + madd4
            m = jnp.max(sc, axis=1, keepdims=True)
            pe = jnp.exp(sc - m)
            l = jnp.sum(pe, axis=1, keepdims=True)
            o = _dot(pe.astype(BF16), vv_s[n]) / l
            for g in range(G_A):
                outs.append(o[g * Ts:(g + 1) * Ts, :])
        o_ref[0] = jnp.concatenate(outs, axis=-1)


def _att_sample(layer, page_table, ck4, cv4, cki4, z3, kn3, kin3, P, bias_s, topk):
    B, Ts, _ = z3.shape
    npages = page_table.shape[1]
    L = (npages + 1) * LANES
    page = lambda w: pl.BlockSpec((None, None, LANES, w),
                                  lambda b, p, pt: (layer, pt[b, jnp.minimum(p, npages - 1)], 0, 0))
    cst = lambda shp: pl.BlockSpec(shp, lambda b, p, pt: tuple(0 for _ in shp))
    grid_spec = pltpu.PrefetchScalarGridSpec(
        num_scalar_prefetch=1,
        grid=(B, npages + 1),
        in_specs=[page(N_KV * HD), page(N_KV * HD), page(D_IDX),
                  pl.BlockSpec((1, Ts, C_A), lambda b, p, pt: (b, 0, ZC_Q)),
                  pl.BlockSpec((1, Ts, H_I * D_IDX), lambda b, p, pt: (b, 0, ZC_QI)),
                  pl.BlockSpec((1, Ts, LANES), lambda b, p, pt: (b, 0, ZC_KW)),
                  pl.BlockSpec((1, Ts, N_KV * HD), lambda b, p, pt: (b, 0, 0)),
                  pl.BlockSpec((1, Ts, N_KV * HD), lambda b, p, pt: (b, 0, ZC_V)),
                  pl.BlockSpec((1, Ts, D_IDX), lambda b, p, pt: (b, 0, 0)),
                  cst((1, C_A)), cst((N_KV, G_A * Ts, L)), cst((C_A, C_A)), cst((LANES, LANES)),
                  cst((LANES, LANES))],
        out_specs=pl.BlockSpec((1, Ts, C_A), lambda b, p, pt: (b, 0, 0)),
        scratch_shapes=[pltpu.VMEM((N_KV, L, HD), BF16),
                        pltpu.VMEM((N_KV, L, HD), BF16),
                        pltpu.VMEM((L, D_IDX), BF16)])
    return pl.pallas_call(
        functools.partial(_atts_kernel, npages=npages, Ts=Ts, topk=topk),
        grid_spec=grid_spec,
        out_shape=jax.ShapeDtypeStruct((B, Ts, C_A), F32),
        compiler_params=_cparams(("parallel", "arbitrary")),
        name="dsa_sample",
    )(page_table, ck4, cv4, cki4, z3, z3, z3, kn3, z3, kin3, P["q_norm8"], bias_s, P["gseg8"], P["tri"], P["ones"])


def _out_proj_kernel(x_ref, orw_ref, ocv_ref, oat_ref, w_ref, y_ref):
    acc = _dot(orw_ref[...].astype(BF16), w_ref[0:C_R, :])
    acc = acc + _dot(ocv_ref[...].astype(BF16), w_ref[C_R:C_R + C_CONV, :])
    acc = acc + _dot(oat_ref[...].astype(BF16), w_ref[C_R + C_CONV:, :])
    y_ref[...] = x_ref[...] + acc


def _out_proj(x2, orw, ocv, oat, w_bf, tm):
    M, D = x2.shape
    row = lambda n: pl.BlockSpec((tm, n), lambda i: (i, 0))
    return pl.pallas_call(
        _out_proj_kernel,
        grid=(M // tm,),
        in_specs=[row(D), row(C_R), row(C_CONV), row(C_A), pl.BlockSpec((D, D), lambda i: (0, 0))],
        out_specs=row(D),
        out_shape=jax.ShapeDtypeStruct((M, D), F32),
        compiler_params=_cparams(("parallel",)),
        name="out_proj",
    )(x2, orw, ocv, oat, w_bf)


def _gelu(x):
    return 0.5 * x * (1.0 + lax.erf(x * (2.0 ** -0.5)))


def _ffn_kernel(x_ref, p_ref, ng_ref, upg_ref, upv_ref, cw_ref, cb_ref, dn_ref, h_ref, h2_ref, pg_ref, pgate_ref,
                pproj_ref,
                xo_ref, tail_ref,
                xn_s, acc_s, hist_s, *, tm, nF, seq_tiles, seq_len):
    i = pl.program_id(0)
    f = pl.program_id(1)

    @pl.when(f == 0)
    def _():
        xn_s[...] = _rms(x_ref[...], ng_ref[...]).astype(BF16)
        acc_s[...] = jnp.zeros(acc_s.shape, F32)

    xn = xn_s[...]
    g = _dot(xn, upg_ref[...])
    val = _dot(xn, upv_ref[...])
    row = lax.broadcasted_iota(I32, g.shape, 0)
    if seq_tiles is not None:
        first = (i % seq_tiles) == 0
        hs = hist_s[f]
        h0 = h_ref[0]
        hm2 = jnp.where(first, h0[0:1, :], hs[SUBLANES - 2:SUBLANES - 1, :])
        hm1 = jnp.where(first, h0[1:2, :], hs[SUBLANES - 1:SUBLANES, :])
        g1 = jnp.where(row == 0, hm1, pltpu.roll(g, 1, 0))
        g2 = jnp.where(row == 0, hm2, jnp.where(row == 1, hm1, pltpu.roll(g, 2, 0)))
        slab = g[tm - SUBLANES:tm, :]
        hist_s[f] = slab
        tail_ref[0] = slab
    else:
        t = row % seq_len
        g1 = jnp.where(t == 0, 0.0, pltpu.roll(g, 1, 0)) + h_ref[...]
        g2 = jnp.where(t < 2, 0.0, pltpu.roll(g, 2, 0)) + h2_ref[...]
        tail_ref[...] = g
    cw = cw_ref[...]
    gate = cw[0:1, :] * g2 + cw[1:2, :] * g1 + cw[2:3, :] * g + cb_ref[...]
    act = (_gelu(gate) * val).astype(BF16)
    acc_s[...] += _dot(act, dn_ref[...])

    @pl.when(f == nF - 1)
    def _():
        x2 = x_ref[...] + acc_s[...]
        xn2 = _rms(x2, pg_ref[...]).astype(BF16)
        gate2 = _sigmoid(_dot(xn2, pgate_ref[...]))
        xo_ref[...] = x2 + _dot(p_ref[...].astype(BF16), pproj_ref[...]) * gate2


def _ffn(x2, p2, ffn0, P, tm, B, T):
    M, D = x2.shape
    tf = 256
    nF = D_FF // tf
    DP = p2.shape[1]
    whole = tm % T == 0
    if whole:
        seq_tiles, seq_len = None, T
        h1 = jnp.pad(ffn0[:, 1:2, :], ((0, 0), (0, T - 1), (0, 0))).reshape(M, D_FF)
        h2 = jnp.pad(ffn0, ((0, 0), (0, T - 2), (0, 0))).reshape(M, D_FF)
        h_specs = [pl.BlockSpec((tm, tf), lambda i, f: (i, f)), pl.BlockSpec((tm, tf), lambda i, f: (i, f))]
        tail_spec = pl.BlockSpec((tm, tf), lambda i, f: (i, f))
        tail_shape = jax.ShapeDtypeStruct((M, D_FF), F32)
    else:
        seq_tiles, seq_len = T // tm, T
        h1, h2 = ffn0, ffn0
        h_specs = [pl.BlockSpec((1, 2, tf), lambda i, f: (i // seq_tiles, 0, f)),
                   pl.BlockSpec((1, 2, tf), lambda i, f: (i // seq_tiles, 0, f))]
        tail_spec = pl.BlockSpec((1, SUBLANES, tf), lambda i, f: (i, 0, f))
        tail_shape = jax.ShapeDtypeStruct((M // tm, SUBLANES, D_FF), F32)
    cst = lambda shp: pl.BlockSpec(shp, lambda i, f: tuple(0 for _ in shp))
    xo, tail = pl.pallas_call(
        functools.partial(_ffn_kernel, tm=tm, nF=nF, seq_tiles=seq_tiles, seq_len=seq_len),
        grid=(M // tm, nF),
        in_specs=[pl.BlockSpec((tm, D), lambda i, f: (i, 0)),
                  pl.BlockSpec((tm, DP), lambda i, f: (i, 0)),
                  cst((1, D)),
                  pl.BlockSpec((D, tf), lambda i, f: (0, f)),
                  pl.BlockSpec((D, tf), lambda i, f: (0, nF + f)),
                  pl.BlockSpec((3, tf), lambda i, f: (0, f)),
                  pl.BlockSpec((1, tf), lambda i, f: (0, f)),
                  pl.BlockSpec((tf, D), lambda i, f: (f, 0))] + h_specs +
                 [cst((1, D)), cst((D, D)), cst((DP, D))],
        out_specs=[pl.BlockSpec((tm, D), lambda i, f: (i, 0)), tail_spec],
        out_shape=[jax.ShapeDtypeStruct((M, D), F32), tail_shape],
        scratch_shapes=[pltpu.VMEM((tm, D), BF16), pltpu.VMEM((tm, D), F32), pltpu.VMEM((nF, SUBLANES, tf), F32)],
        compiler_params=_cparams(("parallel", "arbitrary")),
        name="conv_ffn_ple",
    )(x2, p2, P["norm_ffn"], P["ffn_up"], P["ffn_up"], P["ffn_conv_w"], P["ffn_conv_b"], P["ffn_down"], h1, h2,
      P["ple_norm"], P["ple_gate"], P["ple_proj"])
    if whole:
        ffn_T = tail.reshape(B, T, D_FF)[:, T - 2:, :]
    else:
        ffn_T = tail.reshape(B, seq_tiles, SUBLANES, D_FF)[:, seq_tiles - 1, SUBLANES - 2:, :]
    return xo, ffn_T


def _seg_ones(n, seg):
    i = np.arange(n)
    return jnp.asarray((i[:, None] // seg) == (i[None, :] // seg), dtype=BF16)


def _hl(w):
    hi = w.astype(BF16)
    return hi, (w - hi.astype(F32)).astype(BF16)


def _layer_params(i, W):
    P = {}
    w_in = W["w_in"][i]
    D = w_in.shape[0]
    P["w_in"] = jnp.concatenate([w_in[:, :N_RWKV_COLS], jnp.zeros((D, 1024 - N_RWKV_COLS), F32),
                                 w_in[:, N_RWKV_COLS:], jnp.zeros((D, NZ - 128 - N_IN), F32)], axis=1).astype(BF16)
    row = lambda v: v.reshape(1, -1)
    P["norm_mix"] = row(W["norm_mix"][i])
    P["mu"] = row(W["mu_shift"][i])
    P["w0"] = row(W["w0"][i])
    P["a0"] = row(W["a0"][i])
    z = lambda r: jnp.zeros((r, C_R), F32)
    P["wl_hi"], P["wl_lo"] = _hl(jnp.concatenate([W["w_lora"][i], z(96)], axis=0))
    P["al_hi"], P["al_lo"] = _hl(jnp.concatenate([z(32), W["a_lora"][i], z(64)], axis=0))
    P["gl_hi"], P["gl_lo"] = _hl(jnp.concatenate([z(64), W["g_lora"][i]], axis=0))
    P["k_k"] = row(W["k_k"][i])
    P["k_a"] = row(W["k_a"][i])
    P["r_k"] = row(W["r_k"][i])
    P["lnx_g"] = row(W["lnx_g"][i])
    P["lnx_b"] = row(W["lnx_b"][i])
    P["conv_w"] = W["conv_w"][i]
    P["conv_b"] = row(W["conv_b"][i])
    P["conv_ln_g"] = row(W["conv_ln_g"][i])
    P["conv_ln_b"] = row(W["conv_ln_b"][i])
    P["q_norm8"] = row(jnp.tile(W["q_norm"][i], H_A))
    P["k_norm2"] = row(jnp.tile(W["k_norm"][i], N_KV))
    P["kidx_norm"] = row(W["kidx_norm"][i])
    P["w_out"] = W["w_out"][i].astype(BF16)
    P["norm_ffn"] = row(W["norm_ffn"][i])
    P["ffn_up"] = W["ffn_up"][i].astype(BF16)
    P["ffn_conv_w"] = W["ffn_conv_w"][i]
    P["ffn_conv_b"] = row(W["ffn_conv_b"][i])
    P["ffn_down"] = W["ffn_down"][i].astype(BF16)
    P["ple_norm"] = row(W["ple_norm"][i])
    P["ple_gate"] = W["ple_gate"][i].astype(BF16)
    P["ple_proj"] = W["ple_proj"][i].astype(BF16)
    P["gseg4"] = _seg_ones(C_R, HD)
    P["gseg2"] = _seg_ones(N_KV * HD, HD)
    P["gseg8"] = _seg_ones(C_A, HD)
    P["tri"] = jnp.asarray(np.arange(LANES)[:, None] <= np.arange(LANES)[None, :], dtype=BF16)
    P["ones"] = jnp.ones((LANES, LANES), BF16)
    return P


def _pick_tile(n, pref):
    t = min(n, pref)
    while n % t:
        t //= 2
    return t


def _layer(layer, x, p_i, st, P, att_fn, bb):
    B, T, D = x.shape
    M = B * T
    shift0, wkv0, conv0, ffn0 = st
    x2 = x.reshape(M, D)
    tm = _pick_tile(M, 512)
    z2 = _in_proj(x2, P["norm_mix"], P["w_in"], tm)
    z3 = z2.reshape(B, T, NZ)
    Tc = _pick_tile(T, 256)
    o_rw, wkv_T = _rwkv(z3, shift0.reshape(B, 1, N_RWKV_COLS), wkv0, P, bb, Tc)
    conv0p = jnp.pad(conv0, ((0, 0), (HIST - (CONV_W - 1), 0), (0, 0)))
    o_cv, conv_Tp = _conv(z3, conv0p, P, _pick_tile(T, 512))
    kn2, kin2 = _att_prep(z2, P, tm)
    kn3 = kn2.reshape(B, T, N_KV * HD)
    kin3 = kin2.reshape(B, T, D_IDX)
    o_at = att_fn(layer, z3, kn3, kin3, P)
    xa = _out_proj(x2, o_rw.reshape(M, C_R), o_cv.reshape(M, C_CONV), o_at.reshape(M, C_A), P["w_out"], tm)
    tmf = M if M <= 1024 else _pick_tile(T, 1024)
    xo, ffn_T = _ffn(xa, p_i.reshape(M, -1), ffn0, P, tmf, B, T)
    k_new = kn3.reshape(B, T, N_KV, HD)
    v_new = z3[:, :, ZC_V * LANES:(ZC_V + 1) * LANES].reshape(B, T, N_KV, HD)
    shift_T = z3[:, T - 1, :N_RWKV_COLS]
    conv_T = conv_Tp[:, HIST - (CONV_W - 1):, :]
    return xo.reshape(B, T, D), (k_new, v_new, kin3, wkv_T, shift_T, conv_T, ffn_T)


def kernel(x_prompt, x_sample, cache_k, cache_v, cache_kidx, state_wkv, state_shift, state_conv, state_ffn, page_table, p_prompt, p_sample, norm_mix, w_in, w_out, mu_shift, w0, w_lora, a0, a_lora, g_lora, k_k, k_a, r_k, lnx_g, lnx_b, conv_w, conv_b, conv_ln_g, conv_ln_b, q_norm, k_norm, kidx_norm, rel_bias, norm_ffn, ffn_up, ffn_conv_w, ffn_conv_b, ffn_down, ple_norm, ple_proj, ple_gate):
    W = dict(norm_mix=norm_mix, w_in=w_in, w_out=w_out, mu_shift=mu_shift, w0=w0, w_lora=w_lora, a0=a0,
             a_lora=a_lora, g_lora=g_lora, k_k=k_k, k_a=k_a, r_k=r_k, lnx_g=lnx_g, lnx_b=lnx_b,
             conv_w=conv_w, conv_b=conv_b, conv_ln_g=conv_ln_g, conv_ln_b=conv_ln_b, q_norm=q_norm,
             k_norm=k_norm, kidx_norm=kidx_norm, norm_ffn=norm_ffn, ffn_up=ffn_up,
             ffn_conv_w=ffn_conv_w, ffn_conv_b=ffn_conv_b, ffn_down=ffn_down, ple_norm=ple_norm,
             ple_proj=ple_proj, ple_gate=ple_gate)
    depth = w_in.shape[0]
    params = [_layer_params(i, W) for i in range(depth)]

    B, T, D = x_prompt.shape
    tq = LANES
    topk_p = min(TOPK_MAX, T // 4)
    t_ = np.arange(tq)[:, None]
    s_ = np.arange(LANES)[None, :]
    tiles = np.stack([_rel_bucket_np(np.maximum(t_ - s_, 0)), _rel_bucket_np(tq + t_ - s_),
                      np.full((tq, LANES), N_BUCKETS - 1, np.int32)])
    idx_p = np.broadcast_to(tiles[:, None], (3, H_A, tq, LANES)).reshape(3 * H_A, tq, LANES)
    bias_p = _bias_expand(rel_bias, jnp.asarray(idx_p)).reshape(3, H_A, tq, LANES)

    def att_p(layer, z3, kn3, kin3, P):
        return _att_prompt(z3, kn3, kin3, P, bias_p, topk_p, tq)

    st0 = (jnp.zeros((B, N_RWKV_COLS), F32), jnp.zeros((B, H_R, HD, HD), F32),
           jnp.zeros((B, CONV_W - 1, C_CONV), F32), jnp.zeros((B, 2, D_FF), F32))
    x = x_prompt
    new_p = []
    for i in range(depth):
        x, new = _layer(i, x, p_prompt[i], st0, params[i], att_p, B)
        new_p.append(new)
    y_prompt = x

    Bs, Ts, _ = x_sample.shape
    npages = page_table.shape[1]
    psz = cache_k.shape[2]
    past_len = npages * psz
    topk_s = min(TOPK_MAX, (past_len + Ts) // 4)
    Ls = past_len + psz
    qpos = past_len + np.arange(Ts)[:, None]
    bk = _rel_bucket_np(np.maximum(qpos - np.arange(Ls)[None, :], 0))
    idx_s = np.broadcast_to(bk[None], (H_A, Ts, Ls))
    bias_s = _bias_expand(rel_bias, jnp.asarray(idx_s)).reshape(N_KV, G_A * Ts, Ls)
    n_pool = cache_k.shape[1]
    ck4 = cache_k.reshape(depth, n_pool, psz, N_KV * HD)
    cv4 = cache_v.reshape(depth, n_pool, psz, N_KV * HD)

    def att_s(layer, z3, kn3, kin3, P):
        return _att_sample(layer, page_table, ck4, cv4, cache_kidx, z3, kn3, kin3, P, bias_s, topk_s)

    x = x_sample
    new_s = []
    for i in range(depth):
        st = (state_shift[i], state_wkv[i], state_conv[i], state_ffn[i])
        x, new = _layer(i, x, p_sample[i], st, params[i], att_s, 4)
        new_s.append(new)
    y_sample = x

    stack = lambda lst: tuple(jnp.stack(t) for t in zip(*lst))
    return (y_prompt, y_sample) + stack(new_p) + stack(new_s)
```

```python
import functools
import math

import numpy as np
import jax
import jax.numpy as jnp
from jax import lax
from jax.experimental import pallas as pl
from jax.experimental.pallas import tpu as pltpu

F32 = jnp.float32
BF16 = jnp.bfloat16
I32 = jnp.int32

HD = 64
C_R = 256
H_R = C_R // HD
C_CONV = 256
CONV_W = 31
C_A = 512
H_A = C_A // HD
N_KV = 2
G_A = H_A // N_KV
H_I = 4
D_IDX = 64
TOPK_MAX = 256
N_BUCKETS = 32
MAX_DISTANCE = 128
D_FF = 2816
NORM_EPS = 1e-6
RWKV_LN_EPS = 64e-5
CONV_LN_EPS = 1e-5
N_RWKV_COLS = 896
N_IN = 2500

LANES = 128
SUBLANES = 8
VMEM_LIMIT = 56 * 1024 * 1024

NZ = 2688
ZC_CONV = 1024 // 512
ZC_Q = 1536 // 512
ZC_K = 2048 // 128
ZC_V = 2176 // 128
ZC_QI = 2304 // 256
ZC_KW = 2560 // 128

HIST = 32
NEG = -1e30
INT_MIN = -2 ** 31
KEY_NEG_INF = int(np.array(0xFF800000 ^ 0x7FFFFFFF, dtype=np.uint32).view(np.int32))


def _cparams(sem):
    return pltpu.CompilerParams(dimension_semantics=sem, vmem_limit_bytes=VMEM_LIMIT)


def _split2(x):
    hi = x.astype(BF16)
    lo = (x - hi.astype(F32)).astype(BF16)
    return hi, lo


def _dot(a, b):
    return jnp.dot(a, b, preferred_element_type=F32)


def _dot_nt(a, b):
    return lax.dot_general(a, b, (((1,), (1,)), ((), ())), preferred_element_type=F32)


def _dot_exactw(x, w_bf):
    hi, lo = _split2(x)
    return _dot(hi, w_bf) + _dot(lo, w_bf)


def _dot_hl(x, w_hi, w_lo):
    hi, lo = _split2(x)
    return _dot(hi, w_hi) + (_dot(lo, w_hi) + _dot(hi, w_lo))


def _sigmoid(x):
    return 1.0 / (1.0 + jnp.exp(-x))


def _rms(x, g):
    ms = jnp.mean(x * x, axis=-1, keepdims=True)
    return x * lax.rsqrt(ms + NORM_EPS) * g


def _in_proj_kernel(x_ref, g_ref, w_ref, z_ref, xn_s):
    @pl.when(pl.program_id(1) == 0)
    def _():
        xn_s[...] = _rms(x_ref[...], g_ref[...]).astype(BF16)

    z_ref[...] = _dot(xn_s[...], w_ref[...])


def _in_proj(x2, g, w_bf, tm):
    M, D = x2.shape
    tn = 896
    return pl.pallas_call(
        _in_proj_kernel,
        grid=(M // tm, NZ // tn),
        in_specs=[pl.BlockSpec((tm, D), lambda i, j: (i, 0)),
                  pl.BlockSpec((1, D), lambda i, j: (0, 0)),
                  pl.BlockSpec((D, tn), lambda i, j: (0, j))],
        out_specs=pl.BlockSpec((tm, tn), lambda i, j: (i, j)),
        out_shape=jax.ShapeDtypeStruct((M, NZ), F32),
        scratch_shapes=[pltpu.VMEM((tm, D), BF16)],
        compiler_params=_cparams(("parallel", "arbitrary")),
        name="in_proj",
    )(x2, g, w_bf)


def _rwkv_kernel(z_ref, sh0_ref, wkv0_ref, mu_ref, w0_ref, wl_hi, wl_lo, a0_ref, al_hi, al_lo, gl_hi, gl_lo,
                 kk_ref, ka_ref, rk_ref, lg_ref, lb_ref, gseg_ref,
                 o_ref, st_ref,
                 S_s, carry_s, w_s, k_s, v_s, a_s, b_s, r_s, y_s, bon_s, g_s, *, bb, Tc):
    c = pl.program_id(1)

    @pl.when(c == 0)
    def _():
        S_s[...] = wkv0_ref[...]
        carry_s[...] = sh0_ref[...]

    gseg = gseg_ref[...]
    mu = mu_ref[...]
    rid = lax.broadcasted_iota(I32, (Tc, N_RWKV_COLS), 0)
    for b in range(bb):
        z = z_ref[b]
        prev = jnp.where(rid == 0, carry_s[b], pltpu.roll(z, 1, 0))
        carry_s[b] = z[Tc - 1:Tc, :]
        zs = z + (prev - z) * mu
        r = zs[:, 0:C_R]
        k = zs[:, C_R:2 * C_R]
        v = zs[:, 2 * C_R:3 * C_R]
        t6 = zs[:, 3 * C_R:]
        lw = w0_ref[...] + _dot_hl(jnp.tanh(t6), wl_hi[...], wl_lo[...])
        nl = -lw
        softplus = jnp.maximum(nl, 0.0) + jnp.log(1.0 + jnp.exp(-jnp.abs(nl)))
        decay = jnp.exp(-jnp.exp(-softplus - 0.5))
        a = _sigmoid(a0_ref[...] + _dot_hl(t6, al_hi[...], al_lo[...]))
        g_s[b] = _dot_hl(_sigmoid(t6), gl_hi[...], gl_lo[...])
        kk = k * kk_ref[...]
        nrm = jnp.sqrt(_dot_exactw(kk * kk, gseg))
        kk = kk / jnp.maximum(nrm, 1e-12)
        k2 = k * (1.0 + (a - 1.0) * ka_ref[...])
        bon_s[b] = _dot_exactw(r * k2 * rk_ref[...], gseg) * v
        na = -kk
        kb = kk * a
        for h in range(H_R):
            sl = slice(h * HD, (h + 1) * HD)
            w_s[b, h] = decay[:, sl]
            k_s[b, h] = k2[:, sl]
            v_s[b, h] = v[:, sl]
            a_s[b, h] = na[:, sl]
            b_s[b, h] = kb[:, sl]
            r_s[b, h] = r[:, sl]

    eye = (lax.broadcasted_iota(I32, (HD, HD), 0) == lax.broadcasted_iota(I32, (HD, HD), 1)).astype(F32)
    chains = [(b, h) for b in range(bb) for h in range(H_R)]

    def sub(i, carry):
        t0 = pl.multiple_of(i * SUBLANES, SUBLANES)
        S = [S_s[b, h] for b, h in chains]
        for j in range(SUBLANES):
            t = t0 + j
            for ci, (b, h) in enumerate(chains):
                row = lambda ref: ref[b, h, pl.ds(t, 1), :]
                sa = jnp.sum(S[ci] * row(a_s), axis=1, keepdims=True)
                vc = jnp.sum(eye * row(v_s), axis=1, keepdims=True)
                Sn = S[ci] * row(w_s) + sa * row(b_s) + vc * row(k_s)
                S[ci] = Sn
                y_s[b, h, pl.ds(t, 1), :] = _dot_nt(row(r_s).astype(BF16), Sn.astype(BF16))
        for ci, (b, h) in enumerate(chains):
            S_s[b, h] = S[ci]
        return carry

    lax.fori_loop(0, Tc // SUBLANES, sub, 0)

    for b in range(bb):
        y = jnp.concatenate([y_s[b, h] for h in range(H_R)], axis=-1)
        mean = _dot_exactw(y, gseg) * (1.0 / HD)
        d = y - mean
        var = _dot_exactw(d * d, gseg) * (1.0 / HD)
        yn = d * lax.rsqrt(var + RWKV_LN_EPS) * lg_ref[...] + lb_ref[...]
        o_ref[b] = (yn + bon_s[b]) * g_s[b]
    st_ref[...] = S_s[...]


def _rwkv(z3, shift0, wkv0, P, bb, Tc):
    B, T, _ = z3.shape
    vec = lambda n: pl.BlockSpec((1, n), lambda bi, c: (0, 0))
    mat = lambda r, n: pl.BlockSpec((r, n), lambda bi, c: (0, 0))
    kern = functools.partial(_rwkv_kernel, bb=bb, Tc=Tc)
    hv = lambda: pltpu.VMEM((bb, H_R, Tc, HD), F32)
    return pl.pallas_call(
        kern,
        grid=(B // bb, T // Tc),
        in_specs=[pl.BlockSpec((bb, Tc, N_RWKV_COLS), lambda bi, c: (bi, c, 0)),
                  pl.BlockSpec((bb, 1, N_RWKV_COLS), lambda bi, c: (bi, 0, 0)),
                  pl.BlockSpec((bb, H_R, HD, HD), lambda bi, c: (bi, 0, 0, 0)),
                  vec(N_RWKV_COLS), vec(C_R), mat(LANES, C_R), mat(LANES, C_R), vec(C_R), mat(LANES, C_R),
                  mat(LANES, C_R), mat(LANES, C_R), mat(LANES, C_R),
                  vec(C_R), vec(C_R), vec(C_R), vec(C_R), vec(C_R), mat(C_R, C_R)],
        out_specs=[pl.BlockSpec((bb, Tc, C_R), lambda bi, c: (bi, c, 0)),
                   pl.BlockSpec((bb, H_R, HD, HD), lambda bi, c: (bi, 0, 0, 0))],
        out_shape=[jax.ShapeDtypeStruct((B, T, C_R), F32),
                   jax.ShapeDtypeStruct((B, H_R, HD, HD), F32)],
        scratch_shapes=[pltpu.VMEM((bb, H_R, HD, HD), F32),
                        pltpu.VMEM((bb, 1, N_RWKV_COLS), F32),
                        hv(), hv(), hv(), hv(), hv(), hv(), hv(),
                        pltpu.VMEM((bb, Tc, C_R), F32),
                        pltpu.VMEM((bb, Tc, C_R), F32)],
        compiler_params=_cparams(("parallel", "arbitrary")),
        name="rwkv7",
    )(z3, shift0, wkv0, P["mu"], P["w0"], P["wl_hi"], P["wl_lo"], P["a0"], P["al_hi"], P["al_lo"],
      P["gl_hi"], P["gl_lo"], P["k_k"], P["k_a"], P["r_k"], P["lnx_g"], P["lnx_b"], P["gseg4"])


def _conv_kernel(z_ref, c0_ref, w_ref, b_ref, lg_ref, lb_ref, o_ref, ct_ref, buf, *, Tc):
    @pl.when(pl.program_id(1) == 0)
    def _():
        buf[0:HIST, :] = c0_ref[0]

    z = z_ref[0]
    u = z[:, 0:C_CONV] * _sigmoid(z[:, C_CONV:])
    buf[HIST:HIST + Tc, :] = u
    w = w_ref[...]
    acc = jnp.zeros((Tc, C_CONV), F32) + b_ref[...]
    off = HIST - (CONV_W - 1)
    for j in range(CONV_W):
        acc = acc + w[j:j + 1, :] * buf[off + j:off + j + Tc, :]
    mu = jnp.mean(acc, axis=-1, keepdims=True)
    d = acc - mu
    var = jnp.mean(d * d, axis=-1, keepdims=True)
    cn = d * lax.rsqrt(var + CONV_LN_EPS) * lg_ref[...] + lb_ref[...]
    o_ref[0] = cn * _sigmoid(cn)
    hist = buf[Tc:Tc + HIST, :]
    buf[0:HIST, :] = hist
    ct_ref[0] = hist


def _conv(z3, conv0p, P, Tc):
    B, T, _ = z3.shape
    vec = pl.BlockSpec((1, C_CONV), lambda b, c: (0, 0))
    return pl.pallas_call(
        functools.partial(_conv_kernel, Tc=Tc),
        grid=(B, T // Tc),
        in_specs=[pl.BlockSpec((1, Tc, 2 * C_CONV), lambda b, c: (b, c, ZC_CONV)),
                  pl.BlockSpec((1, HIST, C_CONV), lambda b, c: (b, 0, 0)),
                  pl.BlockSpec((CONV_W, C_CONV), lambda b, c: (0, 0)),
                  vec, vec, vec],
        out_specs=[pl.BlockSpec((1, Tc, C_CONV), lambda b, c: (b, c, 0)),
                   pl.BlockSpec((1, HIST, C_CONV), lambda b, c: (b, 0, 0))],
        out_shape=[jax.ShapeDtypeStruct((B, T, C_CONV), F32),
                   jax.ShapeDtypeStruct((B, HIST, C_CONV), F32)],
        scratch_shapes=[pltpu.VMEM((HIST + Tc, C_CONV), F32)],
        compiler_params=_cparams(("parallel", "arbitrary")),
        name="conformer_conv",
    )(z3, conv0p, P["conv_w"], P["conv_b"], P["conv_ln_g"], P["conv_ln_b"])


def _att_prep_kernel(k_ref, kw_ref, kg_ref, kig_ref, gseg_ref, kn_ref, kin_ref):
    k = k_ref[...]
    ms = _dot_exactw(k * k, gseg_ref[...]) * (1.0 / HD)
    kn_ref[...] = k * lax.rsqrt(ms + NORM_EPS) * kg_ref[...]
    kw = kw_ref[...]
    ki = kw[:, 0:D_IDX]
    msi = jnp.mean(ki * ki, axis=-1, keepdims=True)
    kin_ref[...] = ki * lax.rsqrt(msi + NORM_EPS) * kig_ref[...]


def _att_prep(z2, P, tm):
    M = z2.shape[0]
    return pl.pallas_call(
        _att_prep_kernel,
        grid=(M // tm,),
        in_specs=[pl.BlockSpec((tm, LANES), lambda i: (i, ZC_K)),
                  pl.BlockSpec((tm, LANES), lambda i: (i, ZC_KW)),
                  pl.BlockSpec((1, LANES), lambda i: (0, 0)),
                  pl.BlockSpec((1, D_IDX), lambda i: (0, 0)),
                  pl.BlockSpec((LANES, LANES), lambda i: (0, 0))],
        out_specs=[pl.BlockSpec((tm, LANES), lambda i: (i, 0)),
                   pl.BlockSpec((tm, D_IDX), lambda i: (i, 0))],
        out_shape=[jax.ShapeDtypeStruct((M, N_KV * HD), F32),
                   jax.ShapeDtypeStruct((M, D_IDX), F32)],
        compiler_params=_cparams(("parallel",)),
        name="att_prep",
    )(z2, z2, P["k_norm2"], P["kidx_norm"], P["gseg2"])


def _bias_kernel(rb_ref, idx_ref, o_ref):
    h = pl.program_id(0) % H_A
    idx = idx_ref[0]
    out = jnp.zeros(idx.shape, F32)
    for bk in range(N_BUCKETS):
        out = jnp.where(idx == bk, rb_ref[bk, h], out)
    o_ref[0] = out


def _bias_expand(rel_bias, idx):
    S, R, L = idx.shape
    return pl.pallas_call(
        _bias_kernel,
        grid=(S,),
        in_specs=[pl.BlockSpec(memory_space=pltpu.SMEM),
                  pl.BlockSpec((1, R, L), lambda s: (s, 0, 0))],
        out_specs=pl.BlockSpec((1, R, L), lambda s: (s, 0, 0)),
        out_shape=jax.ShapeDtypeStruct((S, R, L), F32),
        compiler_params=_cparams(("parallel",)),
        name="rel_bias_tiles",
    )(rel_bias, idx)


def _rel_bucket_np(dist):
    max_exact = N_BUCKETS // 2
    d_f = np.maximum(dist, max_exact).astype(np.float32)
    large = max_exact + (np.log(d_f / np.float32(max_exact)) / np.float32(math.log(MAX_DISTANCE / max_exact))
                         * np.float32(N_BUCKETS - max_exact)).astype(np.int32)
    return np.where(dist < max_exact, dist, np.minimum(large, N_BUCKETS - 1)).astype(np.int32)


def _sort_key(s):
    bits = pltpu.bitcast(s + 0.0, I32)
    return jnp.where(bits < 0, bits ^ 0x7FFFFFFF, bits)


def _q_prep(q, qg, gseg8):
    ms = _dot_exactw(q * q, gseg8) * (1.0 / HD)
    return (q * lax.rsqrt(ms + NORM_EPS) * qg * (HD ** -0.5)).astype(BF16)


def _attp_kernel(q_ref, qi_ref, kw_ref, kn_ref, v_ref, kin_ref, qg_ref, bias_ref, gseg_ref, tri_ref, ones_ref,
                 o_ref,
                 kk_s, vv_s, ki_s, qs_s, qi_s, wb_s, key_s, madd_s, s_s, *, tq, topk, kg, ag):
    j = pl.program_id(1)
    nb = j + 1
    KW = kg * LANES
    AW = ag * LANES
    ngk = (nb + kg - 1) // kg
    nga = (nb + ag - 1) // ag
    R = G_A * tq

    @pl.when(j == 0)
    def _():
        kn = kn_ref[0]
        v = v_ref[0]
        for n in range(N_KV):
            kk_s[n] = kn[:, n * HD:(n + 1) * HD].astype(BF16)
            vv_s[n] = v[:, n * HD:(n + 1) * HD].astype(BF16)
        ki_s[...] = kin_ref[0].astype(BF16)

    qn = _q_prep(q_ref[0], qg_ref[...], gseg_ref[...])
    for n in range(N_KV):
        qs_s[n] = jnp.concatenate([qn[:, (n * G_A + g) * HD:(n * G_A + g + 1) * HD] for g in range(G_A)], axis=0)
    qi = (qi_ref[0] * (D_IDX ** -0.5)).astype(BF16)
    kw = kw_ref[0]
    for h in range(H_I):
        qi_s[h] = qi[:, h * D_IDX:(h + 1) * D_IDX]
        wb_s[h] = jnp.broadcast_to(kw[:, D_IDX + h:D_IDX + h + 1] * (H_I ** -0.5), (tq, KW))

    row = lax.broadcasted_iota(I32, (tq, KW), 0) + j * tq
    col0 = lax.broadcasted_iota(I32, (tq, KW), 1)

    def idx_body(gi, carry):
        off = pl.multiple_of(gi * KW, KW)
        kc = ki_s[pl.ds(off, KW), :]
        s = jnp.zeros((tq, KW), F32)
        for h in range(H_I):
            s = s + jnp.maximum(_dot_nt(qi_s[h], kc), 0.0) * wb_s[h]
        s = jnp.where(col0 + off <= row, s, -jnp.inf)
        key_s[:, pl.ds(off, KW)] = _sort_key(s)
        return carry

    lax.fori_loop(0, ngk, idx_body, 0)

    def key_tiles(gi):
        off = pl.multiple_of(gi * KW, KW)
        key = key_s[:, pl.ds(off, KW)]
        return [key[:, u * LANES:(u + 1) * LANES] for u in range(kg)]

    def lane_total(acc):
        return jnp.broadcast_to(jnp.sum(acc, axis=1, keepdims=True), (tq, LANES))

    def bit_body(it, cur):
        cand = cur + lax.shift_left(jnp.int32(1), (31 - it).astype(I32))

        def body(gi, acc):
            for kt in key_tiles(gi):
                acc = acc + jnp.where(kt >= cand, 1.0, 0.0)
            return acc

        cnt = lane_total(lax.fori_loop(0, ngk, body, jnp.zeros((tq, LANES), F32)))
        return jnp.where(cnt >= float(topk), cand, cur)

    tau = lax.fori_loop(0, 32, bit_body, jnp.full((tq, LANES), INT_MIN, I32))

    def cnt2_body(gi, carry):
        cge, cgt = carry
        for kt in key_tiles(gi):
            cge = cge + jnp.where(kt >= tau, 1.0, 0.0)
            cgt = cgt + jnp.where(kt > tau, 1.0, 0.0)
        return cge, cgt

    zero = jnp.zeros((tq, LANES), F32)
    cge, cgt = lax.fori_loop(0, ngk, cnt2_body, (zero, zero))
    need = float(topk) - lane_total(cgt)
    ties = jnp.max(lane_total(cge)) > float(topk)

    @pl.when(jnp.logical_not(ties))
    def _():
        def body(gi, carry):
            off = pl.multiple_of(gi * KW, KW)
            for u, kt in enumerate(key_tiles(gi)):
                take = jnp.where(kt >= tau, jnp.where(kt > KEY_NEG_INF, 0.0, NEG), NEG)
                madd_s[:, pl.ds(off + u * LANES, LANES)] = take
            return carry

        lax.fori_loop(0, ngk, body, 0)

    @pl.when(ties)
    def _():
        tri = tri_ref[...]
        ones = ones_ref[...]

        def body(c, run):
            off = pl.multiple_of(c * LANES, LANES)
            key = key_s[:, pl.ds(off, LANES)]
            eq = jnp.where(key == tau, 1.0, 0.0)
            eqb = eq.astype(BF16)
            pre = _dot(eqb, tri) + run
            take = jnp.where(key > tau, 1.0, jnp.where(pre <= need, eq, 0.0))
            take = jnp.where(key > KEY_NEG_INF, take, 0.0)
            madd_s[:, pl.ds(off, LANES)] = jnp.where(take > 0.5, 0.0, NEG)
            return run + _dot(eqb, ones)

        lax.fori_loop(0, ngk * kg, body, jnp.zeros((tq, LANES), F32))

    outs = []
    for n in range(N_KV):
        qs = qs_s[n]

        def pass_a(gi, mx, n=n, qs=qs):
            off = pl.multiple_of(gi * AW, AW)
            s = _dot_nt(qs, kk_s[n, pl.ds(off, AW), :])
            for u in range(ag):
                dd = jnp.clip(j - (gi * ag + u), 0, 2)
                mt = madd_s[:, pl.ds(off + u * LANES, LANES)]
                su = s[:, u * LANES:(u + 1) * LANES] + bias_ref[dd, n] + jnp.concatenate([mt] * G_A, axis=0)
                s_s[:, pl.ds(off + u * LANES, LANES)] = su
                mx = jnp.maximum(mx, su)
            return mx

        mx = lax.fori_loop(0, nga, pass_a, jnp.full((R, LANES), NEG, F32))
        mb = jnp.broadcast_to(jnp.max(mx, axis=1, keepdims=True), (R, LANES))

        def pass_b(gi, carry, n=n, mb=mb):
            lacc, acc = carry
            off = pl.multiple_of(gi * AW, AW)
            ps = []
            for u in range(ag):
                p = jnp.exp(s_s[:, pl.ds(off + u * LANES, LANES)] - mb)
                lacc = lacc + p
                ps.append(p.astype(BF16))
            pcat = ps[0] if ag == 1 else jnp.concatenate(ps, axis=1)
            return lacc, acc + _dot(pcat, vv_s[n, pl.ds(off, AW), :])

        lacc, acc = lax.fori_loop(0, nga, pass_b, (jnp.zeros((R, LANES), F32), jnp.zeros((R, HD), F32)))
        o = acc / jnp.sum(lacc, axis=1, keepdims=True)
        for g in range(G_A):
            outs.append(o[g * tq:(g + 1) * tq, :])
    o_ref[0] = jnp.concatenate(outs, axis=-1)


def _att_prompt(z3, kn3, kin3, P, bias_p, topk, tq):
    B, T, _ = z3.shape
    nblk = T // LANES
    kg = 4 if nblk % 4 == 0 else (2 if nblk % 2 == 0 else 1)
    ag = min(kg, 2)
    cst = lambda shp: pl.BlockSpec(shp, lambda b, j: tuple(0 for _ in shp))
    return pl.pallas_call(
        functools.partial(_attp_kernel, tq=tq, topk=topk, kg=kg, ag=ag),
        grid=(B, T // tq),
        in_specs=[pl.BlockSpec((1, tq, C_A), lambda b, j: (b, j, ZC_Q)),
                  pl.BlockSpec((1, tq, H_I * D_IDX), lambda b, j: (b, j, ZC_QI)),
                  pl.BlockSpec((1, tq, LANES), lambda b, j: (b, j, ZC_KW)),
                  pl.BlockSpec((1, T, N_KV * HD), lambda b, j: (b, 0, 0)),
                  pl.BlockSpec((1, T, N_KV * HD), lambda b, j: (b, 0, ZC_V)),
                  pl.BlockSpec((1, T, D_IDX), lambda b, j: (b, 0, 0)),
                  cst((1, C_A)), cst((3, N_KV, G_A * tq, LANES)), cst((C_A, C_A)), cst((LANES, LANES)),
                  cst((LANES, LANES))],
        out_specs=pl.BlockSpec((1, tq, C_A), lambda b, j: (b, j, 0)),
        out_shape=jax.ShapeDtypeStruct((B, T, C_A), F32),
        scratch_shapes=[pltpu.VMEM((N_KV, T, HD), BF16),
                        pltpu.VMEM((N_KV, T, HD), BF16),
                        pltpu.VMEM((T, D_IDX), BF16),
                        pltpu.VMEM((N_KV, G_A * tq, HD), BF16),
                        pltpu.VMEM((H_I, tq, D_IDX), BF16),
                        pltpu.VMEM((H_I, tq, kg * LANES), F32),
                        pltpu.VMEM((tq, T), I32),
                        pltpu.VMEM((tq, T), F32),
                        pltpu.VMEM((G_A * tq, T), F32)],
        compiler_params=_cparams(("parallel", "arbitrary")),
        name="dsa_prompt",
    )(z3, z3, z3, kn3, z3, kin3, P["q_norm8"], bias_p.reshape(3, N_KV, G_A * tq, LANES), P["gseg8"], P["tri"],
      P["ones"])


def _atts_kernel(pt_ref, ck_ref, cv_ref, cki_ref, q_ref, qi_ref, kw_ref, kn_ref, vn_ref, kin_ref, qg_ref, bias_ref,
                 gseg_ref, tri_ref, ones_ref,
                 o_ref,
                 kk_s, vv_s, ki_s, *, npages, Ts, topk):
    p = pl.program_id(1)
    P = LANES
    L = (npages + 1) * P

    @pl.when(p < npages)
    def _():
        off = pl.multiple_of(p * P, P)
        kp = ck_ref[...]
        vp = cv_ref[...]
        for n in range(N_KV):
            kk_s[n, pl.ds(off, P), :] = kp[:, n * HD:(n + 1) * HD].astype(BF16)
            vv_s[n, pl.ds(off, P), :] = vp[:, n * HD:(n + 1) * HD].astype(BF16)
        ki_s[pl.ds(off, P), :] = cki_ref[...].astype(BF16)

    @pl.when(p == npages)
    def _():
        zpad = jnp.zeros((P - Ts, N_KV * HD), F32)
        kt = jnp.concatenate([kn_ref[0], zpad], axis=0)
        vt = jnp.concatenate([vn_ref[0], zpad], axis=0)
        for n in range(N_KV):
            kk_s[n, npages * P:L, :] = kt[:, n * HD:(n + 1) * HD].astype(BF16)
            vv_s[n, npages * P:L, :] = vt[:, n * HD:(n + 1) * HD].astype(BF16)
        ki_s[npages * P:L, :] = jnp.concatenate([kin_ref[0], jnp.zeros((P - Ts, D_IDX), F32)], axis=0).astype(BF16)

        qn = _q_prep(q_ref[0], qg_ref[...], gseg_ref[...])
        qi = (qi_ref[0] * (D_IDX ** -0.5)).astype(BF16)
        kw = kw_ref[0]
        kia = ki_s[...]
        s = jnp.zeros((Ts, L), F32)
        for h in range(H_I):
            d = _dot_nt(qi[:, h * D_IDX:(h + 1) * D_IDX], kia)
            s = s + jnp.maximum(d, 0.0) * (kw[:, D_IDX + h:D_IDX + h + 1] * (H_I ** -0.5))
        col = lax.broadcasted_iota(I32, (Ts, L), 1)
        row = lax.broadcasted_iota(I32, (Ts, L), 0) + npages * P
        key = _sort_key(jnp.where(col <= row, s, -jnp.inf))

        def bit_body(it, cur):
            cand = cur + lax.shift_left(jnp.int32(1), (31 - it).astype(I32))
            cnt = jnp.sum(jnp.where(key >= cand, 1.0, 0.0), axis=1, keepdims=True)
            return jnp.where(cnt >= float(topk), cand, cur)

        tau = lax.fori_loop(0, 32, bit_body, jnp.full((Ts, 1), INT_MIN, I32))
        need = float(topk) - jnp.sum(jnp.where(key > tau, 1.0, 0.0), axis=1, keepdims=True)
        tri = tri_ref[...]
        ones = ones_ref[...]
        eq = jnp.where(key == tau, jnp.where(key > KEY_NEG_INF, 1.0, 0.0), 0.0)
        run = jnp.zeros((Ts, LANES), F32)
        pieces = []
        for c in range(L // LANES):
            eqc = eq[:, c * LANES:(c + 1) * LANES]
            eqb = eqc.astype(BF16)
            pre = _dot(eqb, tri) + run
            pieces.append(jnp.where(pre <= need, eqc, 0.0))
            run = run + _dot(eqb, ones)
        take = jnp.where(key > tau, 1.0, jnp.concatenate(pieces, axis=-1))
        take = jnp.where(key > KEY_NEG_INF, take, 0.0)
        madd = jnp.where(take > 0.5, 0.0, NEG)
        madd4 = jnp.concatenate([madd] * G_A, axis=0)

        outs = []
        for n in range(N_KV):
            qs = jnp.concatenate([qn[:, (n * G_A + g) * HD:(n * G_A + g + 1) * HD] for g in range(G_A)], axis=0)
            sc = _dot_nt(qs, kk_s[n]) + bias_ref[n] + madd4
            m = jnp.max(sc, axis=1, keepdims=True)
            pe = jnp.exp(sc - m)
            l = jnp.sum(pe, axis=1, keepdims=True)
            o = _dot(pe.astype(BF16), vv_s[n]) / l
            for g in range(G_A):
                outs.append(o[g * Ts:(g + 1) * Ts, :])
        o_ref[0] = jnp.concatenate(outs, axis=-1)


def _att_sample(layer, page_table, ck4, cv4, cki4, z3, kn3, kin3, P, bias_s, topk):
    B, Ts, _ = z3.shape
    npages = page_table.shape[1]
    L = (npages + 1) * LANES
    page = lambda w: pl.BlockSpec((None, None, LANES, w),
                                  lambda b, p, pt: (layer, pt[b, jnp.minimum(p, npages - 1)], 0, 0))
    cst = lambda shp: pl.BlockSpec(shp, lambda b, p, pt: tuple(0 for _ in shp))
    grid_spec = pltpu.PrefetchScalarGridSpec(
        num_scalar_prefetch=1,
        grid=(B, npages + 1),
        in_specs=[page(N_KV * HD), page(N_KV * HD), page(D_IDX),
                  pl.BlockSpec((1, Ts, C_A), lambda b, p, pt: (b, 0, ZC_Q)),
                  pl.BlockSpec((1, Ts, H_I * D_IDX), lambda b, p, pt: (b, 0, ZC_QI)),
                  pl.BlockSpec((1, Ts, LANES), lambda b, p, pt: (b, 0, ZC_KW)),
                  pl.BlockSpec((1, Ts, N_KV * HD), lambda b, p, pt: (b, 0, 0)),
                  pl.BlockSpec((1, Ts, N_KV * HD), lambda b, p, pt: (b, 0, ZC_V)),
                  pl.BlockSpec((1, Ts, D_IDX), lambda b, p, pt: (b, 0, 0)),
                  cst((1, C_A)), cst((N_KV, G_A * Ts, L)), cst((C_A, C_A)), cst((LANES, LANES)),
                  cst((LANES, LANES))],
        out_specs=pl.BlockSpec((1, Ts, C_A), lambda b, p, pt: (b, 0, 0)),
        scratch_shapes=[pltpu.VMEM((N_KV, L, HD), BF16),
                        pltpu.VMEM((N_KV, L, HD), BF16),
                        pltpu.VMEM((L, D_IDX), BF16)])
    return pl.pallas_call(
        functools.partial(_atts_kernel, npages=npages, Ts=Ts, topk=topk),
        grid_spec=grid_spec,
        out_shape=jax.ShapeDtypeStruct((B, Ts, C_A), F32),
        compiler_params=_cparams(("parallel", "arbitrary")),
        name="dsa_sample",
    )(page_table, ck4, cv4, cki4, z3, z3, z3, kn3, z3, kin3, P["q_norm8"], bias_s, P["gseg8"], P["tri"], P["ones"])


def _out_proj_kernel(x_ref, orw_ref, ocv_ref, oat_ref, w_ref, y_ref):
    acc = _dot(orw_ref[...].astype(BF16), w_ref[0:C_R, :])
    acc = acc + _dot(ocv_ref[...].astype(BF16), w_ref[C_R:C_R + C_CONV, :])
    acc = acc + _dot(oat_ref[...].astype(BF16), w_ref[C_R + C_CONV:, :])
    y_ref[...] = x_ref[...] + acc


def _out_proj(x2, orw, ocv, oat, w_bf, tm):
    M, D = x2.shape
    row = lambda n: pl.BlockSpec((tm, n), lambda i: (i, 0))
    return pl.pallas_call(
        _out_proj_kernel,
        grid=(M // tm,),
        in_specs=[row(D), row(C_R), row(C_CONV), row(C_A), pl.BlockSpec((D, D), lambda i: (0, 0))],
        out_specs=row(D),
        out_shape=jax.ShapeDtypeStruct((M, D), F32),
        compiler_params=_cparams(("parallel",)),
        name="out_proj",
    )(x2, orw, ocv, oat, w_bf)


def _gelu(x):
    return 0.5 * x * (1.0 + lax.erf(x * (2.0 ** -0.5)))


def _ffn_kernel(x_ref, p_ref, ng_ref, upg_ref, upv_ref, cw_ref, cb_ref, dn_ref, h_ref, h2_ref, pg_ref, pgate_ref,
                pproj_ref,
                xo_ref, tail_ref,
                xn_s, acc_s, hist_s, *, tm, nF, seq_tiles, seq_len):
    i = pl.program_id(0)
    f = pl.program_id(1)

    @pl.when(f == 0)
    def _():
        xn_s[...] = _rms(x_ref[...], ng_ref[...]).astype(BF16)
        acc_s[...] = jnp.zeros(acc_s.shape, F32)

    xn = xn_s[...]
    g = _dot(xn, upg_ref[...])
    val = _dot(xn, upv_ref[...])
    row = lax.broadcasted_iota(I32, g.shape, 0)
    if seq_tiles is not None:
        first = (i % seq_tiles) == 0
        hs = hist_s[f]
        h0 = h_ref[0]
        hm2 = jnp.where(first, h0[0:1, :], hs[SUBLANES - 2:SUBLANES - 1, :])
        hm1 = jnp.where(first, h0[1:2, :], hs[SUBLANES - 1:SUBLANES, :])
        g1 = jnp.where(row == 0, hm1, pltpu.roll(g, 1, 0))
        g2 = jnp.where(row == 0, hm2, jnp.where(row == 1, hm1, pltpu.roll(g, 2, 0)))
        slab = g[tm - SUBLANES:tm, :]
        hist_s[f] = slab
        tail_ref[0] = slab
    else:
        t = row % seq_len
        g1 = jnp.where(t == 0, 0.0, pltpu.roll(g, 1, 0)) + h_ref[...]
        g2 = jnp.where(t < 2, 0.0, pltpu.roll(g, 2, 0)) + h2_ref[...]
        tail_ref[...] = g
    cw = cw_ref[...]
    gate = cw[0:1, :] * g2 + cw[1:2, :] * g1 + cw[2:3, :] * g + cb_ref[...]
    act = (_gelu(gate) * val).astype(BF16)
    acc_s[...] += _dot(act, dn_ref[...])

    @pl.when(f == nF - 1)
    def _():
        x2 = x_ref[...] + acc_s[...]
        xn2 = _rms(x2, pg_ref[...]).astype(BF16)
        gate2 = _sigmoid(_dot(xn2, pgate_ref[...]))
        xo_ref[...] = x2 + _dot(p_ref[...].astype(BF16), pproj_ref[...]) * gate2


def _ffn(x2, p2, ffn0, P, tm, B, T):
    M, D = x2.shape
    tf = 256
    nF = D_FF // tf
    DP = p2.shape[1]
    whole = tm % T == 0
    if whole:
        seq_tiles, seq_len = None, T
        h1 = jnp.pad(ffn0[:, 1:2, :], ((0, 0), (0, T - 1), (0, 0))).reshape(M, D_FF)
        h2 = jnp.pad(ffn0, ((0, 0), (0, T - 2), (0, 0))).reshape(M, D_FF)
        h_specs = [pl.BlockSpec((tm, tf), lambda i, f: (i, f)), pl.BlockSpec((tm, tf), lambda i, f: (i, f))]
        tail_spec = pl.BlockSpec((tm, tf), lambda i, f: (i, f))
        tail_shape = jax.ShapeDtypeStruct((M, D_FF), F32)
    else:
        seq_tiles, seq_len = T // tm, T
        h1, h2 = ffn0, ffn0
        h_specs = [pl.BlockSpec((1, 2, tf), lambda i, f: (i // seq_tiles, 0, f)),
                   pl.BlockSpec((1, 2, tf), lambda i, f: (i // seq_tiles, 0, f))]
        tail_spec = pl.BlockSpec((1, SUBLANES, tf), lambda i, f: (i, 0, f))
        tail_shape = jax.ShapeDtypeStruct((M // tm, SUBLANES, D_FF), F32)
    cst = lambda shp: pl.BlockSpec(shp, lambda i, f: tuple(0 for _ in shp))
    xo, tail = pl.pallas_call(
        functools.partial(_ffn_kernel, tm=tm, nF=nF, seq_tiles=seq_tiles, seq_len=seq_len),
        grid=(M // tm, nF),
        in_specs=[pl.BlockSpec((tm, D), lambda i, f: (i, 0)),
                  pl.BlockSpec((tm, DP), lambda i, f: (i, 0)),
                  cst((1, D)),
                  pl.BlockSpec((D, tf), lambda i, f: (0, f)),
                  pl.BlockSpec((D, tf), lambda i, f: (0, nF + f)),
                  pl.BlockSpec((3, tf), lambda i, f: (0, f)),
                  pl.BlockSpec((1, tf), lambda i, f: (0, f)),
                  pl.BlockSpec((tf, D), lambda i, f: (f, 0))] + h_specs +
                 [cst((1, D)), cst((D, D)), cst((DP, D))],
        out_specs=[pl.BlockSpec((tm, D), lambda i, f: (i, 0)), tail_spec],
        out_shape=[jax.ShapeDtypeStruct((M, D), F32), tail_shape],
        scratch_shapes=[pltpu.VMEM((tm, D), BF16), pltpu.VMEM((tm, D), F32), pltpu.VMEM((nF, SUBLANES, tf), F32)],
        compiler_params=_cparams(("parallel", "arbitrary")),
        name="conv_ffn_ple",
    )(x2, p2, P["norm_ffn"], P["ffn_up"], P["ffn_up"], P["ffn_conv_w"], P["ffn_conv_b"], P["ffn_down"], h1, h2,
      P["ple_norm"], P["ple_gate"], P["ple_proj"])
    if whole:
        ffn_T = tail.reshape(B, T, D_FF)[:, T - 2:, :]
    else:
        ffn_T = tail.reshape(B, seq_tiles, SUBLANES, D_FF)[:, seq_tiles - 1, SUBLANES - 2:, :]
    return xo, ffn_T


def _seg_ones(n, seg):
    i = np.arange(n)
    return jnp.asarray((i[:, None] // seg) == (i[None, :] // seg), dtype=BF16)


def _hl(w):
    hi = w.astype(BF16)
    return hi, (w - hi.astype(F32)).astype(BF16)


def _layer_params(i, W):
    P = {}
    w_in = W["w_in"][i]
    D = w_in.shape[0]
    P["w_in"] = jnp.concatenate([w_in[:, :N_RWKV_COLS], jnp.zeros((D, 1024 - N_RWKV_COLS), F32),
                                 w_in[:, N_RWKV_COLS:], jnp.zeros((D, NZ - 128 - N_IN), F32)], axis=1).astype(BF16)
    row = lambda v: v.reshape(1, -1)
    P["norm_mix"] = row(W["norm_mix"][i])
    P["mu"] = row(W["mu_shift"][i])
    P["w0"] = row(W["w0"][i])
    P["a0"] = row(W["a0"][i])
    z = lambda r: jnp.zeros((r, C_R), F32)
    P["wl_hi"], P["wl_lo"] = _hl(jnp.concatenate([W["w_lora"][i], z(96)], axis=0))
    P["al_hi"], P["al_lo"] = _hl(jnp.concatenate([z(32), W["a_lora"][i], z(64)], axis=0))
    P["gl_hi"], P["gl_lo"] = _hl(jnp.concatenate([z(64), W["g_lora"][i]], axis=0))
    P["k_k"] = row(W["k_k"][i])
    P["k_a"] = row(W["k_a"][i])
    P["r_k"] = row(W["r_k"][i])
    P["lnx_g"] = row(W["lnx_g"][i])
    P["lnx_b"] = row(W["lnx_b"][i])
    P["conv_w"] = W["conv_w"][i]
    P["conv_b"] = row(W["conv_b"][i])
    P["conv_ln_g"] = row(W["conv_ln_g"][i])
    P["conv_ln_b"] = row(W["conv_ln_b"][i])
    P["q_norm8"] = row(jnp.tile(W["q_norm"][i], H_A))
    P["k_norm2"] = row(jnp.tile(W["k_norm"][i], N_KV))
    P["kidx_norm"] = row(W["kidx_norm"][i])
    P["w_out"] = W["w_out"][i].astype(BF16)
    P["norm_ffn"] = row(W["norm_ffn"][i])
    P["ffn_up"] = W["ffn_up"][i].astype(BF16)
    P["ffn_conv_w"] = W["ffn_conv_w"][i]
    P["ffn_conv_b"] = row(W["ffn_conv_b"][i])
    P["ffn_down"] = W["ffn_down"][i].astype(BF16)
    P["ple_norm"] = row(W["ple_norm"][i])
    P["ple_gate"] = W["ple_gate"][i].astype(BF16)
    P["ple_proj"] = W["ple_proj"][i].astype(BF16)
    P["gseg4"] = _seg_ones(C_R, HD)
    P["gseg2"] = _seg_ones(N_KV * HD, HD)
    P["gseg8"] = _seg_ones(C_A, HD)
    P["tri"] = jnp.asarray(np.arange(LANES)[:, None] <= np.arange(LANES)[None, :], dtype=BF16)
    P["ones"] = jnp.ones((LANES, LANES), BF16)
    return P


def _pick_tile(n, pref):
    t = min(n, pref)
    while n % t:
        t //= 2
    return t


def _layer(layer, x, p_i, st, P, att_fn, bb):
    B, T, D = x.shape
    M = B * T
    shift0, wkv0, conv0, ffn0 = st
    x2 = x.reshape(M, D)
    tm = _pick_tile(M, 512)
    z2 = _in_proj(x2, P["norm_mix"], P["w_in"], tm)
    z3 = z2.reshape(B, T, NZ)
    Tc = _pick_tile(T, 256)
    o_rw, wkv_T = _rwkv(z3, shift0.reshape(B, 1, N_RWKV_COLS), wkv0, P, bb, Tc)
    conv0p = jnp.pad(conv0, ((0, 0), (HIST - (CONV_W - 1), 0), (0, 0)))
    o_cv, conv_Tp = _conv(z3, conv0p, P, _pick_tile(T, 512))
    kn2, kin2 = _att_prep(z2, P, tm)
    kn3 = kn2.reshape(B, T, N_KV * HD)
    kin3 = kin2.reshape(B, T, D_IDX)
    o_at = att_fn(layer, z3, kn3, kin3, P)
    xa = _out_proj(x2, o_rw.reshape(M, C_R), o_cv.reshape(M, C_CONV), o_at.reshape(M, C_A), P["w_out"], tm)
    tmf = M if M <= 1024 else _pick_tile(T, 1024)
    xo, ffn_T = _ffn(xa, p_i.reshape(M, -1), ffn0, P, tmf, B, T)
    k_new = kn3.reshape(B, T, N_KV, HD)
    v_new = z3[:, :, ZC_V * LANES:(ZC_V + 1) * LANES].reshape(B, T, N_KV, HD)
    shift_T = z3[:, T - 1, :N_RWKV_COLS]
    conv_T = conv_Tp[:, HIST - (CONV_W - 1):, :]
    return xo.reshape(B, T, D), (k_new, v_new, kin3, wkv_T, shift_T, conv_T, ffn_T)


def kernel(x_prompt, x_sample, cache_k, cache_v, cache_kidx, state_wkv, state_shift, state_conv, state_ffn, page_table, p_prompt, p_sample, norm_mix, w_in, w_out, mu_shift, w0, w_lora, a0, a_lora, g_lora, k_k, k_a, r_k, lnx_g, lnx_b, conv_w, conv_b, conv_ln_g, conv_ln_b, q_norm, k_norm, kidx_norm, rel_bias, norm_ffn, ffn_up, ffn_conv_w, ffn_conv_b, ffn_down, ple_norm, ple_proj, ple_gate):
    W = dict(norm_mix=norm_mix, w_in=w_in, w_out=w_out, mu_shift=mu_shift, w0=w0, w_lora=w_lora, a0=a0,
             a_lora=a_lora, g_lora=g_lora, k_k=k_k, k_a=k_a, r_k=r_k, lnx_g=lnx_g, lnx_b=lnx_b,
             conv_w=conv_w, conv_b=conv_b, conv_ln_g=conv_ln_g, conv_ln_b=conv_ln_b, q_norm=q_norm,
             k_norm=k_norm, kidx_norm=kidx_norm, norm_ffn=norm_ffn, ffn_up=ffn_up,
             ffn_conv_w=ffn_conv_w, ffn_conv_b=ffn_conv_b, ffn_down=ffn_down, ple_norm=ple_norm,
             ple_proj=ple_proj, ple_gate=ple_gate)
    depth = w_in.shape[0]
    params = [_layer_params(i, W) for i in range(depth)]

    B, T, D = x_prompt.shape
    tq = LANES
    topk_p = min(TOPK_MAX, T // 4)
    t_ = np.arange(tq)[:, None]
    s_ = np.arange(LANES)[None, :]
    tiles = np.stack([_rel_bucket_np(np.maximum(t_ - s_, 0)), _rel_bucket_np(tq + t_ - s_),
                      np.full((tq, LANES), N_BUCKETS - 1, np.int32)])
    idx_p = np.broadcast_to(tiles[:, None], (3, H_A, tq, LANES)).reshape(3 * H_A, tq, LANES)
    bias_p = _bias_expand(rel_bias, jnp.asarray(idx_p)).reshape(3, H_A, tq, LANES)

    def att_p(layer, z3, kn3, kin3, P):
        return _att_prompt(z3, kn3, kin3, P, bias_p, topk_p, tq)

    st0 = (jnp.zeros((B, N_RWKV_COLS), F32), jnp.zeros((B, H_R, HD, HD), F32),
           jnp.zeros((B, CONV_W - 1, C_CONV), F32), jnp.zeros((B, 2, D_FF), F32))
    x = x_prompt
    new_p = []
    for i in range(depth):
        x, new = _layer(i, x, p_prompt[i], st0, params[i], att_p, B)
        new_p.append(new)
    y_prompt = x

    Bs, Ts, _ = x_sample.shape
    npages = page_table.shape[1]
    psz = cache_k.shape[2]
    past_len = npages * psz
    topk_s = min(TOPK_MAX, (past_len + Ts) // 4)
    Ls = past_len + psz
    qpos = past_len + np.arange(Ts)[:, None]
    bk = _rel_bucket_np(np.maximum(qpos - np.arange(Ls)[None, :], 0))
    idx_s = np.broadcast_to(bk[None], (H_A, Ts, Ls))
    bias_s = _bias_expand(rel_bias, jnp.asarray(idx_s)).reshape(N_KV, G_A * Ts, Ls)
    n_pool = cache_k.shape[1]
    ck4 = cache_k.reshape(depth, n_pool, psz, N_KV * HD)
    cv4 = cache_v.reshape(depth, n_pool, psz, N_KV * HD)

    def att_s(layer, z3, kn3, kin3, P):
        return _att_sample(layer, page_table, ck4, cv4, cache_kidx, z3, kn3, kin3, P, bias_s, topk_s)

    x = x_sample
    new_s = []
    for i in range(depth):
        st = (state_shift[i], state_wkv[i], state_conv[i], state_ffn[i])
        x, new = _layer(i, x, p_sample[i], st, params[i], att_s, 4)
        new_s.append(new)
    y_sample = x

    stack = lambda lst: tuple(jnp.stack(t) for t in zip(*lst))
    return (y_prompt, y_sample) + stack(new_p) + stack(new_s)
```

```python
import functools
import math

import numpy as np
import jax
import jax.numpy as jnp
from jax import lax
from jax.experimental import pallas as pl
from jax.experimental.pallas import tpu as pltpu

F32 = jnp.float32
BF16 = jnp.bfloat16
I32 = jnp.int32

HD = 64
C_R = 256
H_R = C_R // HD
C_CONV = 256
CONV_W = 31
C_A = 512
H_A = C_A // HD
N_KV = 2
G_A = H_A // N_KV
H_I = 4
D_IDX = 64
TOPK_MAX = 256
N_BUCKETS = 32
MAX_DISTANCE = 128
D_FF = 2816
NORM_EPS = 1e-6
RWKV_LN_EPS = 64e-5
CONV_LN_EPS = 1e-5
N_RWKV_COLS = 896
N_IN = 2500

LANES = 128
SUBLANES = 8
VMEM_LIMIT = 56 * 1024 * 1024

NZ = 2688
ZC_CONV = 1024 // 512
ZC_Q = 1536 // 512
ZC_K = 2048 // 128
ZC_V = 2176 // 128
ZC_QI = 2304 // 256
ZC_KW = 2560 // 128

HIST = 32
NEG = -1e30
INT_MIN = -2 ** 31
KEY_NEG_INF = int(np.array(0xFF800000 ^ 0x7FFFFFFF, dtype=np.uint32).view(np.int32))


def _cparams(sem):
    return pltpu.CompilerParams(dimension_semantics=sem, vmem_limit_bytes=VMEM_LIMIT)


def _split2(x):
    hi = x.astype(BF16)
    lo = (x - hi.astype(F32)).astype(BF16)
    return hi, lo


def _dot(a, b):
    return jnp.dot(a, b, preferred_element_type=F32)


def _dot_nt(a, b):
    return lax.dot_general(a, b, (((1,), (1,)), ((), ())), preferred_element_type=F32)


def _dot_exactw(x, w_bf):
    hi, lo = _split2(x)
    return _dot(hi, w_bf) + _dot(lo, w_bf)


def _dot_hl(x, w_hi, w_lo):
    hi, lo = _split2(x)
    return _dot(hi, w_hi) + (_dot(lo, w_hi) + _dot(hi, w_lo))


def _sigmoid(x):
    return 1.0 / (1.0 + jnp.exp(-x))


def _rms(x, g):
    ms = jnp.mean(x * x, axis=-1, keepdims=True)
    return x * lax.rsqrt(ms + NORM_EPS) * g


def _in_proj_kernel(x_ref, g_ref, w_ref, z_ref, xn_s):
    @pl.when(pl.program_id(1) == 0)
    def _():
        xn_s[...] = _rms(x_ref[...], g_ref[...]).astype(BF16)

    z_ref[...] = _dot(xn_s[...], w_ref[...])


def _in_proj(x2, g, w_bf, tm):
    M, D = x2.shape
    tn = 896
    return pl.pallas_call(
        _in_proj_kernel,
        grid=(M // tm, NZ // tn),
        in_specs=[pl.BlockSpec((tm, D), lambda i, j: (i, 0)),
                  pl.BlockSpec((1, D), lambda i, j: (0, 0)),
                  pl.BlockSpec((D, tn), lambda i, j: (0, j))],
        out_specs=pl.BlockSpec((tm, tn), lambda i, j: (i, j)),
        out_shape=jax.ShapeDtypeStruct((M, NZ), F32),
        scratch_shapes=[pltpu.VMEM((tm, D), BF16)],
        compiler_params=_cparams(("parallel", "arbitrary")),
        name="in_proj",
    )(x2, g, w_bf)


def _rwkv_kernel(z_ref, sh0_ref, wkv0_ref, mu_ref, w0_ref, wl_hi, wl_lo, a0_ref, al_hi, al_lo, gl_hi, gl_lo,
                 kk_ref, ka_ref, rk_ref, lg_ref, lb_ref, gseg_ref,
                 o_ref, st_ref,
                 S_s, carry_s, w_s, k_s, v_s, a_s, b_s, r_s, y_s, bon_s, g_s, *, bb, Tc):
    c = pl.program_id(1)

    @pl.when(c == 0)
    def _():
        S_s[...] = wkv0_ref[...]
        carry_s[...] = sh0_ref[...]

    gseg = gseg_ref[...]
    mu = mu_ref[...]
    rid = lax.broadcasted_iota(I32, (Tc, N_RWKV_COLS), 0)
    for b in range(bb):
        z = z_ref[b]
        prev = jnp.where(rid == 0, carry_s[b], pltpu.roll(z, 1, 0))
        carry_s[b] = z[Tc - 1:Tc, :]
        zs = z + (prev - z) * mu
        r = zs[:, 0:C_R]
        k = zs[:, C_R:2 * C_R]
        v = zs[:, 2 * C_R:3 * C_R]
        t6 = zs[:, 3 * C_R:]
        lw = w0_ref[...] + _dot_hl(jnp.tanh(t6), wl_hi[...], wl_lo[...])
        nl = -lw
        softplus = jnp.maximum(nl, 0.0) + jnp.log(1.0 + jnp.exp(-jnp.abs(nl)))
        decay = jnp.exp(-jnp.exp(-softplus - 0.5))
        a = _sigmoid(a0_ref[...] + _dot_hl(t6, al_hi[...], al_lo[...]))
        g_s[b] = _dot_hl(_sigmoid(t6), gl_hi[...], gl_lo[...])
        kk = k * kk_ref[...]
        nrm = jnp.sqrt(_dot_exactw(kk * kk, gseg))
        kk = kk / jnp.maximum(nrm, 1e-12)
        k2 = k * (1.0 + (a - 1.0) * ka_ref[...])
        bon_s[b] = _dot_exactw(r * k2 * rk_ref[...], gseg) * v
        na = -kk
        kb = kk * a
        for h in range(H_R):
            sl = slice(h * HD, (h + 1) * HD)
            w_s[b, h] = decay[:, sl]
            k_s[b, h] = k2[:, sl]
            v_s[b, h] = v[:, sl]
            a_s[b, h] = na[:, sl]
            b_s[b, h] = kb[:, sl]
            r_s[b, h] = r[:, sl]

    eye = (lax.broadcasted_iota(I32, (HD, HD), 0) == lax.broadcasted_iota(I32, (HD, HD), 1)).astype(F32)
    chains = [(b, h) for b in range(bb) for h in range(H_R)]

    def sub(i, carry):
        t0 = pl.multiple_of(i * SUBLANES, SUBLANES)
        S = [S_s[b, h] for b, h in chains]
        for j in range(SUBLANES):
            t = t0 + j
            for ci, (b, h) in enumerate(chains):
                row = lambda ref: ref[b, h, pl.ds(t, 1), :]
                sa = jnp.sum(S[ci] * row(a_s), axis=1, keepdims=True)
                vc = jnp.sum(eye * row(v_s), axis=1, keepdims=True)
                Sn = S[ci] * row(w_s) + sa * row(b_s) + vc * row(k_s)
                S[ci] = Sn
                y_s[b, h, pl.ds(t, 1), :] = _dot_nt(row(r_s).astype(BF16), Sn.astype(BF16))
        for ci, (b, h) in enumerate(chains):
            S_s[b, h] = S[ci]
        return carry

    lax.fori_loop(0, Tc // SUBLANES, sub, 0)

    for b in range(bb):
        y = jnp.concatenate([y_s[b, h] for h in range(H_R)], axis=-1)
        mean = _dot_exactw(y, gseg) * (1.0 / HD)
        d = y - mean
        var = _dot_exactw(d * d, gseg) * (1.0 / HD)
        yn = d * lax.rsqrt(var + RWKV_LN_EPS) * lg_ref[...] + lb_ref[...]
        o_ref[b] = (yn + bon_s[b]) * g_s[b]
    st_ref[...] = S_s[...]


def _rwkv(z3, shift0, wkv0, P, bb, Tc):
    B, T, _ = z3.shape
    vec = lambda n: pl.BlockSpec((1, n), lambda bi, c: (0, 0))
    mat = lambda r, n: pl.BlockSpec((r, n), lambda bi, c: (0, 0))
    kern = functools.partial(_rwkv_kernel, bb=bb, Tc=Tc)
    hv = lambda: pltpu.VMEM((bb, H_R, Tc, HD), F32)
    return pl.pallas_call(
        kern,
        grid=(B // bb, T // Tc),
        in_specs=[pl.BlockSpec((bb, Tc, N_RWKV_COLS), lambda bi, c: (bi, c, 0)),
                  pl.BlockSpec((bb, 1, N_RWKV_COLS), lambda bi, c: (bi, 0, 0)),
                  pl.BlockSpec((bb, H_R, HD, HD), lambda bi, c: (bi, 0, 0, 0)),
                  vec(N_RWKV_COLS), vec(C_R), mat(LANES, C_R), mat(LANES, C_R), vec(C_R), mat(LANES, C_R),
                  mat(LANES, C_R), mat(LANES, C_R), mat(LANES, C_R),
                  vec(C_R), vec(C_R), vec(C_R), vec(C_R), vec(C_R), mat(C_R, C_R)],
        out_specs=[pl.BlockSpec((bb, Tc, C_R), lambda bi, c: (bi, c, 0)),
                   pl.BlockSpec((bb, H_R, HD, HD), lambda bi, c: (bi, 0, 0, 0))],
        out_shape=[jax.ShapeDtypeStruct((B, T, C_R), F32),
                   jax.ShapeDtypeStruct((B, H_R, HD, HD), F32)],
        scratch_shapes=[pltpu.VMEM((bb, H_R, HD, HD), F32),
                        pltpu.VMEM((bb, 1, N_RWKV_COLS), F32),
                        hv(), hv(), hv(), hv(), hv(), hv(), hv(),
                        pltpu.VMEM((bb, Tc, C_R), F32),
                        pltpu.VMEM((bb, Tc, C_R), F32)],
        compiler_params=_cparams(("parallel", "arbitrary")),
        name="rwkv7",
    )(z3, shift0, wkv0, P["mu"], P["w0"], P["wl_hi"], P["wl_lo"], P["a0"], P["al_hi"], P["al_lo"],
      P["gl_hi"], P["gl_lo"], P["k_k"], P["k_a"], P["r_k"], P["lnx_g"], P["lnx_b"], P["gseg4"])


def _conv_kernel(z_ref, c0_ref, w_ref, b_ref, lg_ref, lb_ref, o_ref, ct_ref, buf, *, Tc):
    @pl.when(pl.program_id(1) == 0)
    def _():
        buf[0:HIST, :] = c0_ref[0]

    z = z_ref[0]
    u = z[:, 0:C_CONV] * _sigmoid(z[:, C_CONV:])
    buf[HIST:HIST + Tc, :] = u
    w = w_ref[...]
    acc = jnp.zeros((Tc, C_CONV), F32) + b_ref[...]
    off = HIST - (CONV_W - 1)
    for j in range(CONV_W):
        acc = acc + w[j:j + 1, :] * buf[off + j:off + j + Tc, :]
    mu = jnp.mean(acc, axis=-1, keepdims=True)
    d = acc - mu
    var = jnp.mean(d * d, axis=-1, keepdims=True)
    cn = d * lax.rsqrt(var + CONV_LN_EPS) * lg_ref[...] + lb_ref[...]
    o_ref[0] = cn * _sigmoid(cn)
    hist = buf[Tc:Tc + HIST, :]
    buf[0:HIST, :] = hist
    ct_ref[0] = hist


def _conv(z3, conv0p, P, Tc):
    B, T, _ = z3.shape
    vec = pl.BlockSpec((1, C_CONV), lambda b, c: (0, 0))
    return pl.pallas_call(
        functools.partial(_conv_kernel, Tc=Tc),
        grid=(B, T // Tc),
        in_specs=[pl.BlockSpec((1, Tc, 2 * C_CONV), lambda b, c: (b, c, ZC_CONV)),
                  pl.BlockSpec((1, HIST, C_CONV), lambda b, c: (b, 0, 0)),
                  pl.BlockSpec((CONV_W, C_CONV), lambda b, c: (0, 0)),
                  vec, vec, vec],
        out_specs=[pl.BlockSpec((1, Tc, C_CONV), lambda b, c: (b, c, 0)),
                   pl.BlockSpec((1, HIST, C_CONV), lambda b, c: (b, 0, 0))],
        out_shape=[jax.ShapeDtypeStruct((B, T, C_CONV), F32),
                   jax.ShapeDtypeStruct((B, HIST, C_CONV), F32)],
        scratch_shapes=[pltpu.VMEM((HIST + Tc, C_CONV), F32)],
        compiler_params=_cparams(("parallel", "arbitrary")),
        name="conformer_conv",
    )(z3, conv0p, P["conv_w"], P["conv_b"], P["conv_ln_g"], P["conv_ln_b"])


def _att_prep_kernel(k_ref, kw_ref, kg_ref, kig_ref, gseg_ref, kn_ref, kin_ref):
    k = k_ref[...]
    ms = _dot_exactw(k * k, gseg_ref[...]) * (1.0 / HD)
    kn_ref[...] = k * lax.rsqrt(ms + NORM_EPS) * kg_ref[...]
    kw = kw_ref[...]
    ki = kw[:, 0:D_IDX]
    msi = jnp.mean(ki * ki, axis=-1, keepdims=True)
    kin_ref[...] = ki * lax.rsqrt(msi + NORM_EPS) * kig_ref[...]


def _att_prep(z2, P, tm):
    M = z2.shape[0]
    return pl.pallas_call(
        _att_prep_kernel,
        grid=(M // tm,),
        in_specs=[pl.BlockSpec((tm, LANES), lambda i: (i, ZC_K)),
                  pl.BlockSpec((tm, LANES), lambda i: (i, ZC_KW)),
                  pl.BlockSpec((1, LANES), lambda i: (0, 0)),
                  pl.BlockSpec((1, D_IDX), lambda i: (0, 0)),
                  pl.BlockSpec((LANES, LANES), lambda i: (0, 0))],
        out_specs=[pl.BlockSpec((tm, LANES), lambda i: (i, 0)),
                   pl.BlockSpec((tm, D_IDX), lambda i: (i, 0))],
        out_shape=[jax.ShapeDtypeStruct((M, N_KV * HD), F32),
                   jax.ShapeDtypeStruct((M, D_IDX), F32)],
        compiler_params=_cparams(("parallel",)),
        name="att_prep",
    )(z2, z2, P["k_norm2"], P["kidx_norm"], P["gseg2"])


def _bias_kernel(rb_ref, idx_ref, o_ref):
    h = pl.program_id(0) % H_A
    idx = idx_ref[0]
    out = jnp.zeros(idx.shape, F32)
    for bk in range(N_BUCKETS):
        out = jnp.where(idx == bk, rb_ref[bk, h], out)
    o_ref[0] = out


def _bias_expand(rel_bias, idx):
    S, R, L = idx.shape
    return pl.pallas_call(
        _bias_kernel,
        grid=(S,),
        in_specs=[pl.BlockSpec(memory_space=pltpu.SMEM),
                  pl.BlockSpec((1, R, L), lambda s: (s, 0, 0))],
        out_specs=pl.BlockSpec((1, R, L), lambda s: (s, 0, 0)),
        out_shape=jax.ShapeDtypeStruct((S, R, L), F32),
        compiler_params=_cparams(("parallel",)),
        name="rel_bias_tiles",
    )(rel_bias, idx)


def _rel_bucket_np(dist):
    max_exact = N_BUCKETS // 2
    d_f = np.maximum(dist, max_exact).astype(np.float32)
    large = max_exact + (np.log(d_f / np.float32(max_exact)) / np.float32(math.log(MAX_DISTANCE / max_exact))
                         * np.float32(N_BUCKETS - max_exact)).astype(np.int32)
    return np.where(dist < max_exact, dist, np.minimum(large, N_BUCKETS - 1)).astype(np.int32)


def _sort_key(s):
    bits = pltpu.bitcast(s + 0.0, I32)
    return jnp.where(bits < 0, bits ^ 0x7FFFFFFF, bits)


def _q_prep(q, qg, gseg8):
    ms = _dot_exactw(q * q, gseg8) * (1.0 / HD)
    return (q * lax.rsqrt(ms + NORM_EPS) * qg * (HD ** -0.5)).astype(BF16)


def _attp_kernel(q_ref, qi_ref, kw_ref, kn_ref, v_ref, kin_ref, qg_ref, bias_ref, gseg_ref, tri_ref, ones_ref,
                 o_ref,
                 kk_s, vv_s, ki_s, qs_s, qi_s, wb_s, key_s, madd_s, s_s, *, tq, topk, kg, ag):
    j = pl.program_id(1)
    nb = j + 1
    KW = kg * LANES
    AW = ag * LANES
    ngk = (nb + kg - 1) // kg
    nga = (nb + ag - 1) // ag
    R = G_A * tq

    @pl.when(j == 0)
    def _():
        kn = kn_ref[0]
        v = v_ref[0]
        for n in range(N_KV):
            kk_s[n] = kn[:, n * HD:(n + 1) * HD].astype(BF16)
            vv_s[n] = v[:, n * HD:(n + 1) * HD].astype(BF16)
        ki_s[...] = kin_ref[0].astype(BF16)

    qn = _q_prep(q_ref[0], qg_ref[...], gseg_ref[...])
    for n in range(N_KV):
        qs_s[n] = jnp.concatenate([qn[:, (n * G_A + g) * HD:(n * G_A + g + 1) * HD] for g in range(G_A)], axis=0)
    qi = (qi_ref[0] * (D_IDX ** -0.5)).astype(BF16)
    kw = kw_ref[0]
    for h in range(H_I):
        qi_s[h] = qi[:, h * D_IDX:(h + 1) * D_IDX]
        wb_s[h] = jnp.broadcast_to(kw[:, D_IDX + h:D_IDX + h + 1] * (H_I ** -0.5), (tq, KW))

    row = lax.broadcasted_iota(I32, (tq, KW), 0) + j * tq
    col0 = lax.broadcasted_iota(I32, (tq, KW), 1)

    def idx_body(gi, carry):
        off = pl.multiple_of(gi * KW, KW)
        kc = ki_s[pl.ds(off, KW), :]
        s = jnp.zeros((tq, KW), F32)
        for h in range(H_I):
            s = s + jnp.maximum(_dot_nt(qi_s[h], kc), 0.0) * wb_s[h]
        s = jnp.where(col0 + off <= row, s, -jnp.inf)
        key_s[:, pl.ds(off, KW)] = _sort_key(s)
        return carry

    lax.fori_loop(0, ngk, idx_body, 0)

    def key_tiles(gi):
        off = pl.multiple_of(gi * KW, KW)
        key = key_s[:, pl.ds(off, KW)]
        return [key[:, u * LANES:(u + 1) * LANES] for u in range(kg)]

    def lane_total(acc):
        return jnp.broadcast_to(jnp.sum(acc, axis=1, keepdims=True), (tq, LANES))

    def bit_body(it, cur):
        cand = cur + lax.shift_left(jnp.int32(1), jnp.asarray(31 - it, dtype=I32))

        def body(gi, acc):
            for kt in key_tiles(gi):
                acc = acc + jnp.where(kt >= cand, 1.0, 0.0)
            return acc

        cnt = lane_total(lax.fori_loop(0, ngk, body, jnp.zeros((tq, LANES), F32)))
        return jnp.where(cnt >= float(topk), cand, cur)

    tau = lax.fori_loop(0, 32, bit_body, jnp.full((tq, LANES), INT_MIN, I32))

    def cnt2_body(gi, carry):
        cge, cgt = carry
        for kt in key_tiles(gi):
            cge = cge + jnp.where(kt >= tau, 1.0, 0.0)
            cgt = cgt + jnp.where(kt > tau, 1.0, 0.0)
        return cge, cgt

    zero = jnp.zeros((tq, LANES), F32)
    cge, cgt = lax.fori_loop(0, ngk, cnt2_body, (zero, zero))
    need = float(topk) - lane_total(cgt)
    ties = jnp.max(lane_total(cge)) > float(topk)

    @pl.when(jnp.logical_not(ties))
    def _():
        def body(gi, carry):
            off = pl.multiple_of(gi * KW, KW)
            for u, kt in enumerate(key_tiles(gi)):
                take = jnp.where(kt >= tau, jnp.where(kt > KEY_NEG_INF, 0.0, NEG), NEG)
                madd_s[:, pl.ds(off + u * LANES, LANES)] = take
            return carry

        lax.fori_loop(0, ngk, body, 0)

    @pl.when(ties)
    def _():
        tri = tri_ref[...]
        ones = ones_ref[...]

        def body(c, run):
            off = pl.multiple_of(c * LANES, LANES)
            key = key_s[:, pl.ds(off, LANES)]
            eq = jnp.where(key == tau, 1.0, 0.0)
            eqb = eq.astype(BF16)
            pre = _dot(eqb, tri) + run
            take = jnp.where(key > tau, 1.0, jnp.where(pre <= need, eq, 0.0))
            take = jnp.where(key > KEY_NEG_INF, take, 0.0)
            madd_s[:, pl.ds(off, LANES)] = jnp.where(take > 0.5, 0.0, NEG)
            return run + _dot(eqb, ones)

        lax.fori_loop(0, ngk * kg, body, jnp.zeros((tq, LANES), F32))

    outs = []
    for n in range(N_KV):
        qs = qs_s[n]

        def pass_a(gi, mx, n=n, qs=qs):
            off = pl.multiple_of(gi * AW, AW)
            s = _dot_nt(qs, kk_s[n, pl.ds(off, AW), :])
            for u in range(ag):
                dd = jnp.clip(j - (gi * ag + u), 0, 2)
                mt = madd_s[:, pl.ds(off + u * LANES, LANES)]
                su = s[:, u * LANES:(u + 1) * LANES] + bias_ref[dd, n] + jnp.concatenate([mt] * G_A, axis=0)
                s_s[:, pl.ds(off + u * LANES, LANES)] = su
                mx = jnp.maximum(mx, su)
            return mx

        mx = lax.fori_loop(0, nga, pass_a, jnp.full((R, LANES), NEG, F32))
        mb = jnp.broadcast_to(jnp.max(mx, axis=1, keepdims=True), (R, LANES))

        def pass_b(gi, carry, n=n, mb=mb):
            lacc, acc = carry
            off = pl.multiple_of(gi * AW, AW)
            ps = []
            for u in range(ag):
                p = jnp.exp(s_s[:, pl.ds(off + u * LANES, LANES)] - mb)
                lacc = lacc + p
                ps.append(p.astype(BF16))
            pcat = ps[0] if ag == 1 else jnp.concatenate(ps, axis=1)
            return lacc, acc + _dot(pcat, vv_s[n, pl.ds(off, AW), :])

        lacc, acc = lax.fori_loop(0, nga, pass_b, (jnp.zeros((R, LANES), F32), jnp.zeros((R, HD), F32)))
        o = acc / jnp.sum(lacc, axis=1, keepdims=True)
        for g in range(G_A):
            outs.append(o[g * tq:(g + 1) * tq, :])
    o_ref[0] = jnp.concatenate(outs, axis=-1)


def _att_prompt(z3, kn3, kin3, P, bias_p, topk, tq):
    B, T, _ = z3.shape
    nblk = T // LANES
    kg = 4 if nblk % 4 == 0 else (2 if nblk % 2 == 0 else 1)
    ag = min(kg, 2)
    cst = lambda shp: pl.BlockSpec(shp, lambda b, j: tuple(0 for _ in shp))
    return pl.pallas_call(
        functools.partial(_attp_kernel, tq=tq, topk=topk, kg=kg, ag=ag),
        grid=(B, T // tq),
        in_specs=[pl.BlockSpec((1, tq, C_A), lambda b, j: (b, j, ZC_Q)),
                  pl.BlockSpec((1, tq, H_I * D_IDX), lambda b, j: (b, j, ZC_QI)),
                  pl.BlockSpec((1, tq, LANES), lambda b, j: (b, j, ZC_KW)),
                  pl.BlockSpec((1, T, N_KV * HD), lambda b, j: (b, 0, 0)),
                  pl.BlockSpec((1, T, N_KV * HD), lambda b, j: (b, 0, ZC_V)),
                  pl.BlockSpec((1, T, D_IDX), lambda b, j: (b, 0, 0)),
                  cst((1, C_A)), cst((3, N_KV, G_A * tq, LANES)), cst((C_A, C_A)), cst((LANES, LANES)),
                  cst((LANES, LANES))],
        out_specs=pl.BlockSpec((1, tq, C_A), lambda b, j: (b, j, 0)),
        out_shape=jax.ShapeDtypeStruct((B, T, C_A), F32),
        scratch_shapes=[pltpu.VMEM((N_KV, T, HD), BF16),
                        pltpu.VMEM((N_KV, T, HD), BF16),
                        pltpu.VMEM((T, D_IDX), BF16),
                        pltpu.VMEM((N_KV, G_A * tq, HD), BF16),
                        pltpu.VMEM((H_I, tq, D_IDX), BF16),
                        pltpu.VMEM((H_I, tq, kg * LANES), F32),
                        pltpu.VMEM((tq, T), I32),
                        pltpu.VMEM((tq, T), F32),
                        pltpu.VMEM((G_A * tq, T), F32)],
        compiler_params=_cparams(("parallel", "arbitrary")),
        name="dsa_prompt",
    )(z3, z3, z3, kn3, z3, kin3, P["q_norm8"], bias_p.reshape(3, N_KV, G_A * tq, LANES), P["gseg8"], P["tri"],
      P["ones"])


def _pad_transpose(x):
    r, c = x.shape
    if c < LANES:
        x = jnp.concatenate([x, jnp.zeros((r, LANES - c), F32)], axis=1)
    x = jnp.concatenate([x, jnp.zeros((LANES - r, LANES), F32)], axis=0)
    return jnp.transpose(x)


def _atts_kernel(pt_ref, ckT_ref, cvT_ref, ckiT_ref, q_ref, qi_ref, kw_ref, kn_ref, vn_ref, kin_ref, qg_ref,
                 bias_ref, gseg_ref, tri_ref, ones_ref,
                 o_ref,
                 kT_s, vT_s, kiT_s, madd_s, sem, *, layer, npages, nbatch, Ts, topk):
    b = pl.program_id(0)
    slot = b % 2
    P = LANES
    L = (npages + 1) * P

    def page_copies(seq, sl, p):
        pg = pt_ref[seq, p]
        col = pl.multiple_of(p * P, P)
        return (pltpu.make_async_copy(ckT_ref.at[layer, pg], kT_s.at[sl, :, :, pl.ds(col, P)], sem.at[sl, 0]),
                pltpu.make_async_copy(cvT_ref.at[layer, pg], vT_s.at[sl, :, :, pl.ds(col, P)], sem.at[sl, 1]),
                pltpu.make_async_copy(ckiT_ref.at[layer, pg], kiT_s.at[sl, :, pl.ds(col, P)], sem.at[sl, 2]))

    def start_pages(seq, sl):
        def body(p, carry):
            for cp in page_copies(seq, sl, p):
                cp.start()
            return carry
        lax.fori_loop(0, npages, body, 0)

    def wait_pages(seq, sl):
        def body(p, carry):
            for cp in page_copies(seq, sl, p):
                cp.wait()
            return carry
        lax.fori_loop(0, npages, body, 0)

    @pl.when(b == 0)
    def _():
        start_pages(0, 0)

    @pl.when(b + 1 < nbatch)
    def _():
        start_pages(b + 1, 1 - slot)

    knT = _pad_transpose(kn_ref[0])
    vnT = _pad_transpose(vn_ref[0])
    for n in range(N_KV):
        kT_s[slot, n, :, npages * P:L] = knT[n * HD:(n + 1) * HD, :]
        vT_s[slot, n, :, npages * P:L] = vnT[n * HD:(n + 1) * HD, :]
    kiT_s[slot, :, npages * P:L] = _pad_transpose(kin_ref[0])[0:D_IDX, :]

    qn = _q_prep(q_ref[0], qg_ref[...], gseg_ref[...])
    qi = (qi_ref[0] * (D_IDX ** -0.5)).astype(BF16)
    qis = jnp.concatenate([qi[:, h * D_IDX:(h + 1) * D_IDX] for h in range(H_I)], axis=0)
    kw = kw_ref[0]

    wait_pages(b, slot)

    d = _dot(qis, kiT_s[slot].astype(BF16))
    s = jnp.zeros((Ts, L), F32)
    for h in range(H_I):
        s = s + jnp.maximum(d[h * Ts:(h + 1) * Ts, :], 0.0) * (kw[:, D_IDX + h:D_IDX + h + 1] * (H_I ** -0.5))
    col = lax.broadcasted_iota(I32, (Ts, L), 1)
    row = lax.broadcasted_iota(I32, (Ts, L), 0) + npages * P
    key = _sort_key(jnp.where(col <= row, s, -jnp.inf))

    def bit_body(it, cur):
        cand = cur + lax.shift_left(jnp.int32(1), jnp.asarray(31 - it, dtype=I32))
        cnt = jnp.sum(jnp.where(key >= cand, 1.0, 0.0), axis=1, keepdims=True)
        return jnp.where(cnt >= float(topk), cand, cur)

    tau = lax.fori_loop(0, 32, bit_body, jnp.full((Ts, 1), INT_MIN, I32))
    need = float(topk) - jnp.sum(jnp.where(key > tau, 1.0, 0.0), axis=1, keepdims=True)
    cge = jnp.sum(jnp.where(key >= tau, 1.0, 0.0), axis=1, keepdims=True)
    ties = jnp.max(cge) > float(topk)

    @pl.when(jnp.logical_not(ties))
    def _():
        madd_s[...] = jnp.where(key >= tau, jnp.where(key > KEY_NEG_INF, 0.0, NEG), NEG)

    @pl.when(ties)
    def _():
        tri = tri_ref[...]
        ones = ones_ref[...]
        eq = jnp.where(key == tau, 1.0, 0.0)
        run = jnp.zeros((Ts, LANES), F32)
        for c in range(L // LANES):
            sl = slice(c * LANES, (c + 1) * LANES)
            eqc = eq[:, sl]
            eqb = eqc.astype(BF16)
            pre = _dot(eqb, tri) + run
            take = jnp.where(key[:, sl] > tau, 1.0, jnp.where(pre <= need, eqc, 0.0))
            take = jnp.where(key[:, sl] > KEY_NEG_INF, take, 0.0)
            madd_s[:, sl] = jnp.where(take > 0.5, 0.0, NEG)
            run = run + _dot(eqb, ones)

    madd = madd_s[...]
    madd4 = jnp.concatenate([madd] * G_A, axis=0)
    outs = []
    for n in range(N_KV):
        qs = jnp.concatenate([qn[:, (n * G_A + g) * HD:(n * G_A + g + 1) * HD] for g in range(G_A)], axis=0)
        sc = _dot(qs, kT_s[slot, n].astype(BF16)) + bias_ref[n] + madd4
        m = jnp.max(sc, axis=1, keepdims=True)
        pe = jnp.exp(sc - m)
        l = jnp.sum(pe, axis=1, keepdims=True)
        o = _dot_nt(pe.astype(BF16), vT_s[slot, n].astype(BF16)) / l
        for g in range(G_A):
            outs.append(o[g * Ts:(g + 1) * Ts, :])
    o_ref[0] = jnp.concatenate(outs, axis=-1)


def _att_sample(layer, page_table, ckT, cvT, ckiT, z3, kn3, kin3, P, bias_s, topk):
    B, Ts, _ = z3.shape
    npages = page_table.shape[1]
    L = (npages + 1) * LANES
    hbm = pl.BlockSpec(memory_space=pl.ANY)
    cst = lambda shp: pl.BlockSpec(shp, lambda b, pt: tuple(0 for _ in shp))
    grid_spec = pltpu.PrefetchScalarGridSpec(
        num_scalar_prefetch=1,
        grid=(B,),
        in_specs=[hbm, hbm, hbm,
                  pl.BlockSpec((1, Ts, C_A), lambda b, pt: (b, 0, ZC_Q)),
                  pl.BlockSpec((1, Ts, H_I * D_IDX), lambda b, pt: (b, 0, ZC_QI)),
                  pl.BlockSpec((1, Ts, LANES), lambda b, pt: (b, 0, ZC_KW)),
                  pl.BlockSpec((1, Ts, N_KV * HD), lambda b, pt: (b, 0, 0)),
                  pl.BlockSpec((1, Ts, N_KV * HD), lambda b, pt: (b, 0, ZC_V)),
                  pl.BlockSpec((1, Ts, D_IDX), lambda b, pt: (b, 0, 0)),
                  cst((1, C_A)), cst((N_KV, G_A * Ts, L)), cst((C_A, C_A)), cst((LANES, LANES)),
                  cst((LANES, LANES))],
        out_specs=pl.BlockSpec((1, Ts, C_A), lambda b, pt: (b, 0, 0)),
        scratch_shapes=[pltpu.VMEM((2, N_KV, HD, L), F32),
                        pltpu.VMEM((2, N_KV, HD, L), F32),
                        pltpu.VMEM((2, D_IDX, L), F32),
                        pltpu.VMEM((Ts, L), F32),
                        pltpu.SemaphoreType.DMA((2, 3))])
    return pl.pallas_call(
        functools.partial(_atts_kernel, layer=layer, npages=npages, nbatch=B, Ts=Ts, topk=topk),
        grid_spec=grid_spec,
        out_shape=jax.ShapeDtypeStruct((B, Ts, C_A), F32),
        compiler_params=_cparams(("arbitrary",)),
        name="dsa_sample",
    )(page_table, ckT, cvT, ckiT, z3, z3, z3, kn3, z3, kin3, P["q_norm8"], bias_s, P["gseg8"], P["tri"], P["ones"])


def _out_proj_kernel(x_ref, orw_ref, ocv_ref, oat_ref, w_ref, y_ref):
    acc = _dot(orw_ref[...].astype(BF16), w_ref[0:C_R, :])
    acc = acc + _dot(ocv_ref[...].astype(BF16), w_ref[C_R:C_R + C_CONV, :])
    acc = acc + _dot(oat_ref[...].astype(BF16), w_ref[C_R + C_CONV:, :])
    y_ref[...] = x_ref[...] + acc


def _out_proj(x2, orw, ocv, oat, w_bf, tm):
    M, D = x2.shape
    row = lambda n: pl.BlockSpec((tm, n), lambda i: (i, 0))
    return pl.pallas_call(
        _out_proj_kernel,
        grid=(M // tm,),
        in_specs=[row(D), row(C_R), row(C_CONV), row(C_A), pl.BlockSpec((D, D), lambda i: (0, 0))],
        out_specs=row(D),
        out_shape=jax.ShapeDtypeStruct((M, D), F32),
        compiler_params=_cparams(("parallel",)),
        name="out_proj",
    )(x2, orw, ocv, oat, w_bf)


def _gelu(x):
    return 0.5 * x * (1.0 + lax.erf(x * (2.0 ** -0.5)))


def _ffn_kernel(x_ref, p_ref, ng_ref, upg_ref, upv_ref, cw_ref, cb_ref, dn_ref, h_ref, h2_ref, pg_ref, pgate_ref,
                pproj_ref,
                xo_ref, tail_ref,
                xn_s, acc_s, hist_s, *, tm, nF, seq_tiles, seq_len):
    i = pl.program_id(0)
    f = pl.program_id(1)

    @pl.when(f == 0)
    def _():
        xn_s[...] = _rms(x_ref[...], ng_ref[...]).astype(BF16)
        acc_s[...] = jnp.zeros(acc_s.shape, F32)

    xn = xn_s[...]
    g = _dot(xn, upg_ref[...])
    val = _dot(xn, upv_ref[...])
    row = lax.broadcasted_iota(I32, g.shape, 0)
    if seq_tiles is not None:
        first = (i % seq_tiles) == 0
        hs = hist_s[f]
        h0 = h_ref[0]
        hm2 = jnp.where(first, h0[0:1, :], hs[SUBLANES - 2:SUBLANES - 1, :])
        hm1 = jnp.where(first, h0[1:2, :], hs[SUBLANES - 1:SUBLANES, :])
        g1 = jnp.where(row == 0, hm1, pltpu.roll(g, 1, 0))
        g2 = jnp.where(row == 0, hm2, jnp.where(row == 1, hm1, pltpu.roll(g, 2, 0)))
        slab = g[tm - SUBLANES:tm, :]
        hist_s[f] = slab
        tail_ref[0] = slab
    else:
        t = row % seq_len
        g1 = jnp.where(t == 0, 0.0, pltpu.roll(g, 1, 0)) + h_ref[...]
        g2 = jnp.where(t < 2, 0.0, pltpu.roll(g, 2, 0)) + h2_ref[...]
        tail_ref[...] = g
    cw = cw_ref[...]
    gate = cw[0:1, :] * g2 + cw[1:2, :] * g1 + cw[2:3, :] * g + cb_ref[...]
    act = (_gelu(gate) * val).astype(BF16)
    acc_s[...] += _dot(act, dn_ref[...])

    @pl.when(f == nF - 1)
    def _():
        x2 = x_ref[...] + acc_s[...]
        xn2 = _rms(x2, pg_ref[...]).astype(BF16)
        gate2 = _sigmoid(_dot(xn2, pgate_ref[...]))
        xo_ref[...] = x2 + _dot(p_ref[...].astype(BF16), pproj_ref[...]) * gate2


def _ffn(x2, p2, ffn0, P, tm, B, T):
    M, D = x2.shape
    tf = 256
    nF = D_FF // tf
    DP = p2.shape[1]
    whole = tm % T == 0
    if whole:
        seq_tiles, seq_len = None, T
        h1 = jnp.pad(ffn0[:, 1:2, :], ((0, 0), (0, T - 1), (0, 0))).reshape(M, D_FF)
        h2 = jnp.pad(ffn0, ((0, 0), (0, T - 2), (0, 0))).reshape(M, D_FF)
        h_specs = [pl.BlockSpec((tm, tf), lambda i, f: (i, f)), pl.BlockSpec((tm, tf), lambda i, f: (i, f))]
        tail_spec = pl.BlockSpec((tm, tf), lambda i, f: (i, f))
        tail_shape = jax.ShapeDtypeStruct((M, D_FF), F32)
    else:
        seq_tiles, seq_len = T // tm, T
        h1, h2 = ffn0, ffn0
        h_specs = [pl.BlockSpec((1, 2, tf), lambda i, f: (i // seq_tiles, 0, f)),
                   pl.BlockSpec((1, 2, tf), lambda i, f: (i // seq_tiles, 0, f))]
        tail_spec = pl.BlockSpec((1, SUBLANES, tf), lambda i, f: (i, 0, f))
        tail_shape = jax.ShapeDtypeStruct((M // tm, SUBLANES, D_FF), F32)
    cst = lambda shp: pl.BlockSpec(shp, lambda i, f: tuple(0 for _ in shp))
    xo, tail = pl.pallas_call(
        functools.partial(_ffn_kernel, tm=tm, nF=nF, seq_tiles=seq_tiles, seq_len=seq_len),
        grid=(M // tm, nF),
        in_specs=[pl.BlockSpec((tm, D), lambda i, f: (i, 0)),
                  pl.BlockSpec((tm, DP), lambda i, f: (i, 0)),
                  cst((1, D)),
                  pl.BlockSpec((D, tf), lambda i, f: (0, f)),
                  pl.BlockSpec((D, tf), lambda i, f: (0, nF + f)),
                  pl.BlockSpec((3, tf), lambda i, f: (0, f)),
                  pl.BlockSpec((1, tf), lambda i, f: (0, f)),
                  pl.BlockSpec((tf, D), lambda i, f: (f, 0))] + h_specs +
                 [cst((1, D)), cst((D, D)), cst((DP, D))],
        out_specs=[pl.BlockSpec((tm, D), lambda i, f: (i, 0)), tail_spec],
        out_shape=[jax.ShapeDtypeStruct((M, D), F32), tail_shape],
        scratch_shapes=[pltpu.VMEM((tm, D), BF16), pltpu.VMEM((tm, D), F32), pltpu.VMEM((nF, SUBLANES, tf), F32)],
        compiler_params=_cparams(("parallel", "arbitrary")),
        name="conv_ffn_ple",
    )(x2, p2, P["norm_ffn"], P["ffn_up"], P["ffn_up"], P["ffn_conv_w"], P["ffn_conv_b"], P["ffn_down"], h1, h2,
      P["ple_norm"], P["ple_gate"], P["ple_proj"])
    if whole:
        ffn_T = tail.reshape(B, T, D_FF)[:, T - 2:, :]
    else:
        ffn_T = tail.reshape(B, seq_tiles, SUBLANES, D_FF)[:, seq_tiles - 1, SUBLANES - 2:, :]
    return xo, ffn_T


def _seg_ones(n, seg):
    i = np.arange(n)
    return jnp.asarray((i[:, None] // seg) == (i[None, :] // seg), dtype=BF16)


def _hl(w):
    hi = w.astype(BF16)
    return hi, (w - hi.astype(F32)).astype(BF16)


def _layer_params(i, W):
    P = {}
    w_in = W["w_in"][i]
    D = w_in.shape[0]
    P["w_in"] = jnp.concatenate([w_in[:, :N_RWKV_COLS], jnp.zeros((D, 1024 - N_RWKV_COLS), F32),
                                 w_in[:, N_RWKV_COLS:], jnp.zeros((D, NZ - 128 - N_IN), F32)], axis=1).astype(BF16)
    row = lambda v: v.reshape(1, -1)
    P["norm_mix"] = row(W["norm_mix"][i])
    P["mu"] = row(W["mu_shift"][i])
    P["w0"] = row(W["w0"][i])
    P["a0"] = row(W["a0"][i])
    z = lambda r: jnp.zeros((r, C_R), F32)
    P["wl_hi"], P["wl_lo"] = _hl(jnp.concatenate([W["w_lora"][i], z(96)], axis=0))
    P["al_hi"], P["al_lo"] = _hl(jnp.concatenate([z(32), W["a_lora"][i], z(64)], axis=0))
    P["gl_hi"], P["gl_lo"] = _hl(jnp.concatenate([z(64), W["g_lora"][i]], axis=0))
    P["k_k"] = row(W["k_k"][i])
    P["k_a"] = row(W["k_a"][i])
    P["r_k"] = row(W["r_k"][i])
    P["lnx_g"] = row(W["lnx_g"][i])
    P["lnx_b"] = row(W["lnx_b"][i])
    P["conv_w"] = W["conv_w"][i]
    P["conv_b"] = row(W["conv_b"][i])
    P["conv_ln_g"] = row(W["conv_ln_g"][i])
    P["conv_ln_b"] = row(W["conv_ln_b"][i])
    P["q_norm8"] = row(jnp.tile(W["q_norm"][i], H_A))
    P["k_norm2"] = row(jnp.tile(W["k_norm"][i], N_KV))
    P["kidx_norm"] = row(W["kidx_norm"][i])
    P["w_out"] = W["w_out"][i].astype(BF16)
    P["norm_ffn"] = row(W["norm_ffn"][i])
    P["ffn_up"] = W["ffn_up"][i].astype(BF16)
    P["ffn_conv_w"] = W["ffn_conv_w"][i]
    P["ffn_conv_b"] = row(W["ffn_conv_b"][i])
    P["ffn_down"] = W["ffn_down"][i].astype(BF16)
    P["ple_norm"] = row(W["ple_norm"][i])
    P["ple_gate"] = W["ple_gate"][i].astype(BF16)
    P["ple_proj"] = W["ple_proj"][i].astype(BF16)
    P["gseg4"] = _seg_ones(C_R, HD)
    P["gseg2"] = _seg_ones(N_KV * HD, HD)
    P["gseg8"] = _seg_ones(C_A, HD)
    P["tri"] = jnp.asarray(np.arange(LANES)[:, None] <= np.arange(LANES)[None, :], dtype=BF16)
    P["ones"] = jnp.ones((LANES, LANES), BF16)
    return P


def _pick_tile(n, pref):
    t = min(n, pref)
    while n % t:
        t //= 2
    return t


def _layer(layer, x, p_i, st, P, att_fn, bb):
    B, T, D = x.shape
    M = B * T
    shift0, wkv0, conv0, ffn0 = st
    x2 = x.reshape(M, D)
    tm = _pick_tile(M, 512)
    z2 = _in_proj(x2, P["norm_mix"], P["w_in"], tm)
    z3 = z2.reshape(B, T, NZ)
    Tc = _pick_tile(T, 256)
    o_rw, wkv_T = _rwkv(z3, shift0.reshape(B, 1, N_RWKV_COLS), wkv0, P, bb, Tc)
    conv0p = jnp.pad(conv0, ((0, 0), (HIST - (CONV_W - 1), 0), (0, 0)))
    o_cv, conv_Tp = _conv(z3, conv0p, P, _pick_tile(T, 512))
    kn2, kin2 = _att_prep(z2, P, tm)
    kn3 = kn2.reshape(B, T, N_KV * HD)
    kin3 = kin2.reshape(B, T, D_IDX)
    o_at = att_fn(layer, z3, kn3, kin3, P)
    xa = _out_proj(x2, o_rw.reshape(M, C_R), o_cv.reshape(M, C_CONV), o_at.reshape(M, C_A), P["w_out"], tm)
    tmf = M if M <= 1024 else _pick_tile(T, 1024)
    xo, ffn_T = _ffn(xa, p_i.reshape(M, -1), ffn0, P, tmf, B, T)
    k_new = kn3.reshape(B, T, N_KV, HD)
    v_new = z3[:, :, ZC_V * LANES:(ZC_V + 1) * LANES].reshape(B, T, N_KV, HD)
    shift_T = z3[:, T - 1, :N_RWKV_COLS]
    conv_T = conv_Tp[:, HIST - (CONV_W - 1):, :]
    return xo.reshape(B, T, D), (k_new, v_new, kin3, wkv_T, shift_T, conv_T, ffn_T)


def kernel(x_prompt, x_sample, cache_k, cache_v, cache_kidx, state_wkv, state_shift, state_conv, state_ffn, page_table, p_prompt, p_sample, norm_mix, w_in, w_out, mu_shift, w0, w_lora, a0, a_lora, g_lora, k_k, k_a, r_k, lnx_g, lnx_b, conv_w, conv_b, conv_ln_g, conv_ln_b, q_norm, k_norm, kidx_norm, rel_bias, norm_ffn, ffn_up, ffn_conv_w, ffn_conv_b, ffn_down, ple_norm, ple_proj, ple_gate):
    W = dict(norm_mix=norm_mix, w_in=w_in, w_out=w_out, mu_shift=mu_shift, w0=w0, w_lora=w_lora, a0=a0,
             a_lora=a_lora, g_lora=g_lora, k_k=k_k, k_a=k_a, r_k=r_k, lnx_g=lnx_g, lnx_b=lnx_b,
             conv_w=conv_w, conv_b=conv_b, conv_ln_g=conv_ln_g, conv_ln_b=conv_ln_b, q_norm=q_norm,
             k_norm=k_norm, kidx_norm=kidx_norm, norm_ffn=norm_ffn, ffn_up=ffn_up,
             ffn_conv_w=ffn_conv_w, ffn_conv_b=ffn_conv_b, ffn_down=ffn_down, ple_norm=ple_norm,
             ple_proj=ple_proj, ple_gate=ple_gate)
    depth = w_in.shape[0]
    params = [_layer_params(i, W) for i in range(depth)]

    B, T, D = x_prompt.shape
    tq = LANES
    topk_p = min(TOPK_MAX, T // 4)
    t_ = np.arange(tq)[:, None]
    s_ = np.arange(LANES)[None, :]
    tiles = np.stack([_rel_bucket_np(np.maximum(t_ - s_, 0)), _rel_bucket_np(tq + t_ - s_),
                      np.full((tq, LANES), N_BUCKETS - 1, np.int32)])
    idx_p = np.broadcast_to(tiles[:, None], (3, H_A, tq, LANES)).reshape(3 * H_A, tq, LANES)
    bias_p = _bias_expand(rel_bias, jnp.asarray(idx_p)).reshape(3, H_A, tq, LANES)

    def att_p(layer, z3, kn3, kin3, P):
        return _att_prompt(z3, kn3, kin3, P, bias_p, topk_p, tq)

    st0 = (jnp.zeros((B, N_RWKV_COLS), F32), jnp.zeros((B, H_R, HD, HD), F32),
           jnp.zeros((B, CONV_W - 1, C_CONV), F32), jnp.zeros((B, 2, D_FF), F32))
    x = x_prompt
    new_p = []
    for i in range(depth):
        x, new = _layer(i, x, p_prompt[i], st0, params[i], att_p, B)
        new_p.append(new)
    y_prompt = x

    Bs, Ts, _ = x_sample.shape
    npages = page_table.shape[1]
    psz = cache_k.shape[2]
    past_len = npages * psz
    topk_s = min(TOPK_MAX, (past_len + Ts) // 4)
    Ls = past_len + psz
    qpos = past_len + np.arange(Ts)[:, None]
    bk = _rel_bucket_np(np.maximum(qpos - np.arange(Ls)[None, :], 0))
    idx_s = np.broadcast_to(bk[None], (H_A, Ts, Ls))
    bias_s = _bias_expand(rel_bias, jnp.asarray(idx_s)).reshape(N_KV, G_A * Ts, Ls)
    ckT = jnp.transpose(cache_k, (0, 1, 3, 4, 2))
    cvT = jnp.transpose(cache_v, (0, 1, 3, 4, 2))
    ckiT = jnp.transpose(cache_kidx, (0, 1, 3, 2))

    def att_s(layer, z3, kn3, kin3, P):
        return _att_sample(layer, page_table, ckT, cvT, ckiT, z3, kn3, kin3, P, bias_s, topk_s)

    x = x_sample
    new_s = []
    for i in range(depth):
        st = (state_shift[i], state_wkv[i], state_conv[i], state_ffn[i])
        x, new = _layer(i, x, p_sample[i], st, params[i], att_s, 4)
        new_s.append(new)
    y_sample = x

    stack = lambda lst: tuple(jnp.stack(t) for t in zip(*lst))
    return (y_prompt, y_sample) + stack(new_p) + stack(new_s)
```

```python
import functools
import math

import numpy as np
import jax
import jax.numpy as jnp
from jax import lax
from jax.experimental import pallas as pl
from jax.experimental.pallas import tpu as pltpu

F32 = jnp.float32
BF16 = jnp.bfloat16
I32 = jnp.int32
I16 = jnp.int16

HD = 64
C_R = 256
H_R = C_R // HD
C_CONV = 256
CONV_W = 31
C_A = 512
H_A = C_A // HD
N_KV = 2
G_A = H_A // N_KV
H_I = 4
D_IDX = 64
TOPK_MAX = 256
N_BUCKETS = 32
MAX_DISTANCE = 128
D_FF = 2816
NORM_EPS = 1e-6
RWKV_LN_EPS = 64e-5
CONV_LN_EPS = 1e-5
N_RWKV_COLS = 896
N_IN = 2500

LANES = 128
SUBLANES = 8
VMEM_LIMIT = 56 * 1024 * 1024

NZ = 2688
ZC_CONV = 1024 // 512
ZC_Q = 1536 // 512
ZC_K = 2048 // 128
ZC_V = 2176 // 128
ZC_QI = 2304 // 256
ZC_KW = 2560 // 128

HIST = 32
NEG = -1e30
INT_MIN = -2 ** 31
I16_MIN = -2 ** 15
KEY_NEG_INF = int(np.array(0xFF800000 ^ 0x7FFFFFFF, dtype=np.uint32).view(np.int32))


def _cparams(sem):
    return pltpu.CompilerParams(dimension_semantics=sem, vmem_limit_bytes=VMEM_LIMIT)


def _split2(x):
    hi = x.astype(BF16)
    lo = (x - hi.astype(F32)).astype(BF16)
    return hi, lo


def _dot(a, b):
    return jnp.dot(a, b, preferred_element_type=F32)


def _dot_nt(a, b):
    return lax.dot_general(a, b, (((1,), (1,)), ((), ())), preferred_element_type=F32)


def _dot_exactw(x, w_bf):
    hi, lo = _split2(x)
    return _dot(hi, w_bf) + _dot(lo, w_bf)


def _dot_hl(x, w_hi, w_lo):
    hi, lo = _split2(x)
    return _dot(hi, w_hi) + (_dot(lo, w_hi) + _dot(hi, w_lo))


def _sigmoid(x):
    return 1.0 / (1.0 + jnp.exp(-x))


def _rms(x, g):
    ms = jnp.mean(x * x, axis=-1, keepdims=True)
    return x * lax.rsqrt(ms + NORM_EPS) * g


def _in_proj_kernel(x_ref, g_ref, w_ref, z_ref, xn_s):
    @pl.when(pl.program_id(1) == 0)
    def _():
        xn_s[...] = _rms(x_ref[...], g_ref[...]).astype(BF16)

    z_ref[...] = _dot(xn_s[...], w_ref[...])


def _in_proj(x2, g, w_bf, tm):
    M, D = x2.shape
    tn = 896
    return pl.pallas_call(
        _in_proj_kernel,
        grid=(M // tm, NZ // tn),
        in_specs=[pl.BlockSpec((tm, D), lambda i, j: (i, 0)),
                  pl.BlockSpec((1, D), lambda i, j: (0, 0)),
                  pl.BlockSpec((D, tn), lambda i, j: (0, j))],
        out_specs=pl.BlockSpec((tm, tn), lambda i, j: (i, j)),
        out_shape=jax.ShapeDtypeStruct((M, NZ), F32),
        scratch_shapes=[pltpu.VMEM((tm, D), BF16)],
        compiler_params=_cparams(("parallel", "arbitrary")),
        name="in_proj",
    )(x2, g, w_bf)


def _rwkv_kernel(z_ref, sh0_ref, wkv0_ref, mu_ref, w0_ref, wl_hi, wl_lo, a0_ref, al_hi, al_lo, gl_hi, gl_lo,
                 kk_ref, ka_ref, rk_ref, lg_ref, lb_ref, gseg_ref,
                 o_ref, st_ref,
                 S_s, carry_s, w_s, k_s, v_s, a_s, b_s, r_s, y_s, bon_s, g_s, *, bb, Tc):
    c = pl.program_id(1)

    @pl.when(c == 0)
    def _():
        S_s[...] = wkv0_ref[...]
        carry_s[...] = sh0_ref[...]

    gseg = gseg_ref[...]
    mu = mu_ref[...]
    rid = lax.broadcasted_iota(I32, (Tc, N_RWKV_COLS), 0)
    for b in range(bb):
        z = z_ref[b]
        prev = jnp.where(rid == 0, carry_s[b], pltpu.roll(z, 1, 0))
        carry_s[b] = z[Tc - 1:Tc, :]
        zs = z + (prev - z) * mu
        r = zs[:, 0:C_R]
        k = zs[:, C_R:2 * C_R]
        v = zs[:, 2 * C_R:3 * C_R]
        t6 = zs[:, 3 * C_R:]
        lw = w0_ref[...] + _dot_hl(jnp.tanh(t6), wl_hi[...], wl_lo[...])
        nl = -lw
        softplus = jnp.maximum(nl, 0.0) + jnp.log(1.0 + jnp.exp(-jnp.abs(nl)))
        decay = jnp.exp(-jnp.exp(-softplus - 0.5))
        a = _sigmoid(a0_ref[...] + _dot_hl(t6, al_hi[...], al_lo[...]))
        g_s[b] = _dot_hl(_sigmoid(t6), gl_hi[...], gl_lo[...])
        kk = k * kk_ref[...]
        nrm = jnp.sqrt(_dot_exactw(kk * kk, gseg))
        kk = kk / jnp.maximum(nrm, 1e-12)
        k2 = k * (1.0 + (a - 1.0) * ka_ref[...])
        bon_s[b] = _dot_exactw(r * k2 * rk_ref[...], gseg) * v
        na = -kk
        kb = kk * a
        for h in range(H_R):
            sl = slice(h * HD, (h + 1) * HD)
            w_s[b, h] = decay[:, sl]
            k_s[b, h] = k2[:, sl]
            v_s[b, h] = v[:, sl]
            a_s[b, h] = na[:, sl]
            b_s[b, h] = kb[:, sl]
            r_s[b, h] = r[:, sl]

    eye = (lax.broadcasted_iota(I32, (HD, HD), 0) == lax.broadcasted_iota(I32, (HD, HD), 1)).astype(F32)
    chains = [(b, h) for b in range(bb) for h in range(H_R)]

    def sub(i, carry):
        t0 = pl.multiple_of(i * SUBLANES, SUBLANES)
        S = [S_s[b, h] for b, h in chains]
        for j in range(SUBLANES):
            t = t0 + j
            for ci, (b, h) in enumerate(chains):
                row = lambda ref: ref[b, h, pl.ds(t, 1), :]
                sa = jnp.sum(S[ci] * row(a_s), axis=1, keepdims=True)
                vc = jnp.sum(eye * row(v_s), axis=1, keepdims=True)
                Sn = S[ci] * row(w_s) + sa * row(b_s) + vc * row(k_s)
                S[ci] = Sn
                y_s[b, h, pl.ds(t, 1), :] = _dot_nt(row(r_s).astype(BF16), Sn.astype(BF16))
        for ci, (b, h) in enumerate(chains):
            S_s[b, h] = S[ci]
        return carry

    lax.fori_loop(0, Tc // SUBLANES, sub, 0)

    for b in range(bb):
        y = jnp.concatenate([y_s[b, h] for h in range(H_R)], axis=-1)
        mean = _dot_exactw(y, gseg) * (1.0 / HD)
        d = y - mean
        var = _dot_exactw(d * d, gseg) * (1.0 / HD)
        yn = d * lax.rsqrt(var + RWKV_LN_EPS) * lg_ref[...] + lb_ref[...]
        o_ref[b] = (yn + bon_s[b]) * g_s[b]
    st_ref[...] = S_s[...]


def _rwkv(z3, shift0, wkv0, P, bb, Tc):
    B, T, _ = z3.shape
    vec = lambda n: pl.BlockSpec((1, n), lambda bi, c: (0, 0))
    mat = lambda r, n: pl.BlockSpec((r, n), lambda bi, c: (0, 0))
    kern = functools.partial(_rwkv_kernel, bb=bb, Tc=Tc)
    hv = lambda: pltpu.VMEM((bb, H_R, Tc, HD), F32)
    return pl.pallas_call(
        kern,
        grid=(B // bb, T // Tc),
        in_specs=[pl.BlockSpec((bb, Tc, N_RWKV_COLS), lambda bi, c: (bi, c, 0)),
                  pl.BlockSpec((bb, 1, N_RWKV_COLS), lambda bi, c: (bi, 0, 0)),
                  pl.BlockSpec((bb, H_R, HD, HD), lambda bi, c: (bi, 0, 0, 0)),
                  vec(N_RWKV_COLS), vec(C_R), mat(LANES, C_R), mat(LANES, C_R), vec(C_R), mat(LANES, C_R),
                  mat(LANES, C_R), mat(LANES, C_R), mat(LANES, C_R),
                  vec(C_R), vec(C_R), vec(C_R), vec(C_R), vec(C_R), mat(C_R, C_R)],
        out_specs=[pl.BlockSpec((bb, Tc, C_R), lambda bi, c: (bi, c, 0)),
                   pl.BlockSpec((bb, H_R, HD, HD), lambda bi, c: (bi, 0, 0, 0))],
        out_shape=[jax.ShapeDtypeStruct((B, T, C_R), F32),
                   jax.ShapeDtypeStruct((B, H_R, HD, HD), F32)],
        scratch_shapes=[pltpu.VMEM((bb, H_R, HD, HD), F32),
                        pltpu.VMEM((bb, 1, N_RWKV_COLS), F32),
                        hv(), hv(), hv(), hv(), hv(), hv(), hv(),
                        pltpu.VMEM((bb, Tc, C_R), F32),
                        pltpu.VMEM((bb, Tc, C_R), F32)],
        compiler_params=_cparams(("parallel", "arbitrary")),
        name="rwkv7",
    )(z3, shift0, wkv0, P["mu"], P["w0"], P["wl_hi"], P["wl_lo"], P["a0"], P["al_hi"], P["al_lo"],
      P["gl_hi"], P["gl_lo"], P["k_k"], P["k_a"], P["r_k"], P["lnx_g"], P["lnx_b"], P["gseg4"])


def _conv_kernel(z_ref, c0_ref, w_ref, b_ref, lg_ref, lb_ref, o_ref, ct_ref, buf, *, Tc):
    @pl.when(pl.program_id(1) == 0)
    def _():
        buf[0:HIST, :] = c0_ref[0]

    z = z_ref[0]
    u = z[:, 0:C_CONV] * _sigmoid(z[:, C_CONV:])
    buf[HIST:HIST + Tc, :] = u
    w = w_ref[...]
    acc = jnp.zeros((Tc, C_CONV), F32) + b_ref[...]
    off = HIST - (CONV_W - 1)
    for j in range(CONV_W):
        acc = acc + w[j:j + 1, :] * buf[off + j:off + j + Tc, :]
    mu = jnp.mean(acc, axis=-1, keepdims=True)
    d = acc - mu
    var = jnp.mean(d * d, axis=-1, keepdims=True)
    cn = d * lax.rsqrt(var + CONV_LN_EPS) * lg_ref[...] + lb_ref[...]
    o_ref[0] = cn * _sigmoid(cn)
    hist = buf[Tc:Tc + HIST, :]
    buf[0:HIST, :] = hist
    ct_ref[0] = hist


def _conv(z3, conv0p, P, Tc):
    B, T, _ = z3.shape
    vec = pl.BlockSpec((1, C_CONV), lambda b, c: (0, 0))
    return pl.pallas_call(
        functools.partial(_conv_kernel, Tc=Tc),
        grid=(B, T // Tc),
        in_specs=[pl.BlockSpec((1, Tc, 2 * C_CONV), lambda b, c: (b, c, ZC_CONV)),
                  pl.BlockSpec((1, HIST, C_CONV), lambda b, c: (b, 0, 0)),
                  pl.BlockSpec((CONV_W, C_CONV), lambda b, c: (0, 0)),
                  vec, vec, vec],
        out_specs=[pl.BlockSpec((1, Tc, C_CONV), lambda b, c: (b, c, 0)),
                   pl.BlockSpec((1, HIST, C_CONV), lambda b, c: (b, 0, 0))],
        out_shape=[jax.ShapeDtypeStruct((B, T, C_CONV), F32),
                   jax.ShapeDtypeStruct((B, HIST, C_CONV), F32)],
        scratch_shapes=[pltpu.VMEM((HIST + Tc, C_CONV), F32)],
        compiler_params=_cparams(("parallel", "arbitrary")),
        name="conformer_conv",
    )(z3, conv0p, P["conv_w"], P["conv_b"], P["conv_ln_g"], P["conv_ln_b"])


def _att_prep_kernel(k_ref, kw_ref, kg_ref, kig_ref, gseg_ref, kn_ref, kin_ref):
    k = k_ref[...]
    ms = _dot_exactw(k * k, gseg_ref[...]) * (1.0 / HD)
    kn_ref[...] = k * lax.rsqrt(ms + NORM_EPS) * kg_ref[...]
    kw = kw_ref[...]
    ki = kw[:, 0:D_IDX]
    msi = jnp.mean(ki * ki, axis=-1, keepdims=True)
    kin_ref[...] = ki * lax.rsqrt(msi + NORM_EPS) * kig_ref[...]


def _att_prep(z2, P, tm):
    M = z2.shape[0]
    return pl.pallas_call(
        _att_prep_kernel,
        grid=(M // tm,),
        in_specs=[pl.BlockSpec((tm, LANES), lambda i: (i, ZC_K)),
                  pl.BlockSpec((tm, LANES), lambda i: (i, ZC_KW)),
                  pl.BlockSpec((1, LANES), lambda i: (0, 0)),
                  pl.BlockSpec((1, D_IDX), lambda i: (0, 0)),
                  pl.BlockSpec((LANES, LANES), lambda i: (0, 0))],
        out_specs=[pl.BlockSpec((tm, LANES), lambda i: (i, 0)),
                   pl.BlockSpec((tm, D_IDX), lambda i: (i, 0))],
        out_shape=[jax.ShapeDtypeStruct((M, N_KV * HD), F32),
                   jax.ShapeDtypeStruct((M, D_IDX), F32)],
        compiler_params=_cparams(("parallel",)),
        name="att_prep",
    )(z2, z2, P["k_norm2"], P["kidx_norm"], P["gseg2"])


def _bias_kernel(rb_ref, idx_ref, o_ref):
    h = pl.program_id(0) % H_A
    idx = idx_ref[0]
    out = jnp.zeros(idx.shape, F32)
    for bk in range(N_BUCKETS):
        out = jnp.where(idx == bk, rb_ref[bk, h], out)
    o_ref[0] = out


def _bias_expand(rel_bias, idx):
    S, R, L = idx.shape
    return pl.pallas_call(
        _bias_kernel,
        grid=(S,),
        in_specs=[pl.BlockSpec(memory_space=pltpu.SMEM),
                  pl.BlockSpec((1, R, L), lambda s: (s, 0, 0))],
        out_specs=pl.BlockSpec((1, R, L), lambda s: (s, 0, 0)),
        out_shape=jax.ShapeDtypeStruct((S, R, L), F32),
        compiler_params=_cparams(("parallel",)),
        name="rel_bias_tiles",
    )(rel_bias, idx)


def _rel_bucket_np(dist):
    max_exact = N_BUCKETS // 2
    d_f = np.maximum(dist, max_exact).astype(np.float32)
    large = max_exact + (np.log(d_f / np.float32(max_exact)) / np.float32(math.log(MAX_DISTANCE / max_exact))
                         * np.float32(N_BUCKETS - max_exact)).astype(np.int32)
    return np.where(dist < max_exact, dist, np.minimum(large, N_BUCKETS - 1)).astype(np.int32)


def _sort_key(s):
    bits = pltpu.bitcast(s + 0.0, I32)
    return jnp.where(bits < 0, bits ^ 0x7FFFFFFF, bits)


def _q_prep(q, qg, gseg8):
    ms = _dot_exactw(q * q, gseg8) * (1.0 / HD)
    return (q * lax.rsqrt(ms + NORM_EPS) * qg * (HD ** -0.5)).astype(BF16)


def _attp_kernel(q_ref, qi_ref, kw_ref, kn_ref, v_ref, kin_ref, qg_ref, bias_ref, gseg_ref, tri_ref, ones_ref,
                 o_ref,
                 kk_s, vv_s, ki_s, qs_s, qi_s, wb_s, key_s, hi_s, lo_s, madd_s, s_s, m_s, l_s, acc_s,
                 *, tq, topk, kg, ag):
    j = pl.program_id(1)
    nb = j + 1
    KW = kg * LANES
    AW = ag * LANES
    ngk = (nb + kg - 1) // kg
    nga = (nb + ag - 1) // ag
    R = G_A * tq

    @pl.when(j == 0)
    def _():
        kn = kn_ref[0]
        v = v_ref[0]
        for n in range(N_KV):
            kk_s[n] = kn[:, n * HD:(n + 1) * HD].astype(BF16)
            vv_s[n] = v[:, n * HD:(n + 1) * HD].astype(BF16)
        ki_s[...] = kin_ref[0].astype(BF16)

    qn = _q_prep(q_ref[0], qg_ref[...], gseg_ref[...])
    for n in range(N_KV):
        qs_s[n] = jnp.concatenate([qn[:, (n * G_A + g) * HD:(n * G_A + g + 1) * HD] for g in range(G_A)], axis=0)
    qi = (qi_ref[0] * (D_IDX ** -0.5)).astype(BF16)
    kw = kw_ref[0]
    for h in range(H_I):
        qi_s[h] = qi[:, h * D_IDX:(h + 1) * D_IDX]
        wb_s[h] = jnp.broadcast_to(kw[:, D_IDX + h:D_IDX + h + 1] * (H_I ** -0.5), (tq, KW))

    row = lax.broadcasted_iota(I32, (tq, KW), 0) + j * tq
    col0 = lax.broadcasted_iota(I32, (tq, KW), 1)

    def idx_body(gi, carry):
        off = pl.multiple_of(gi * KW, KW)
        kc = ki_s[pl.ds(off, KW), :]
        s = jnp.zeros((tq, KW), F32)
        for h in range(H_I):
            s = s + jnp.maximum(_dot_nt(qi_s[h], kc), 0.0) * wb_s[h]
        s = jnp.where(col0 + off <= row, s, -jnp.inf)
        key = _sort_key(s)
        key_s[:, pl.ds(off, KW)] = key
        hi_s[:, pl.ds(off, KW)] = lax.shift_right_arithmetic(key, 16).astype(I16)
        lo_s[:, pl.ds(off, KW)] = ((key & 0xFFFF) + I16_MIN).astype(I16)
        return carry

    lax.fori_loop(0, ngk, idx_body, 0)

    def key_tiles(gi, ref=key_s):
        off = pl.multiple_of(gi * KW, KW)
        key = ref[:, pl.ds(off, KW)]
        return [key[:, u * LANES:(u + 1) * LANES] for u in range(kg)]

    def lane_total(acc):
        return jnp.broadcast_to(jnp.sum(acc.astype(F32), axis=1, keepdims=True), (tq, LANES))

    one16 = jnp.ones((tq, LANES), I16)
    zero16 = jnp.zeros((tq, LANES), I16)

    def search16(ref, kneed):
        def bit_body(it, cur):
            cand = cur + lax.shift_left(jnp.int32(1), jnp.asarray(15 - it, dtype=I32))
            c16 = cand.astype(I16)

            def body(gi, acc):
                for kt in key_tiles(gi, ref):
                    acc = acc + jnp.where(kt >= c16, one16, zero16)
                return acc

            cnt = lane_total(lax.fori_loop(0, ngk, body, zero16))
            return jnp.where(cnt >= kneed, cand, cur)

        return lax.fori_loop(0, 16, bit_body, jnp.full((tq, LANES), I16_MIN, I32))

    tau_hi = search16(hi_s, float(topk))
    th16 = tau_hi.astype(I16)
    lo_pad = jnp.full((tq, LANES), I16_MIN, I16)

    def mid_body(gi, acc):
        off = pl.multiple_of(gi * KW, KW)
        los = key_tiles(gi, lo_s)
        for u, ht in enumerate(key_tiles(gi, hi_s)):
            acc = acc + jnp.where(ht > th16, one16, zero16)
            lo_s[:, pl.ds(off + u * LANES, LANES)] = jnp.where(ht == th16, los[u], lo_pad)
        return acc

    need_lo = float(topk) - lane_total(lax.fori_loop(0, ngk, mid_body, zero16))
    tau_lo = search16(lo_s, need_lo)
    tau = lax.shift_left(tau_hi, 16) + (tau_lo - I16_MIN)

    def cnt2_body(gi, carry):
        cge, cgt = carry
        for kt in key_tiles(gi):
            cge = cge + jnp.where(kt >= tau, 1.0, 0.0)
            cgt = cgt + jnp.where(kt > tau, 1.0, 0.0)
        return cge, cgt

    zero = jnp.zeros((tq, LANES), F32)
    cge, cgt = lax.fori_loop(0, ngk, cnt2_body, (zero, zero))
    need = float(topk) - lane_total(cgt)
    ties = jnp.max(lane_total(cge)) > float(topk)

    @pl.when(jnp.logical_not(ties))
    def _():
        def body(gi, carry):
            off = pl.multiple_of(gi * KW, KW)
            for u, kt in enumerate(key_tiles(gi)):
                take = jnp.where(kt >= tau, jnp.where(kt > KEY_NEG_INF, 0.0, NEG), NEG)
                madd_s[:, pl.ds(off + u * LANES, LANES)] = take
            return carry

        lax.fori_loop(0, ngk, body, 0)

    @pl.when(ties)
    def _():
        tri = tri_ref[...]
        ones = ones_ref[...]

        def body(c, run):
            off = pl.multiple_of(c * LANES, LANES)
            key = key_s[:, pl.ds(off, LANES)]
            eq = jnp.where(key == tau, 1.0, 0.0)
            eqb = eq.astype(BF16)
            pre = _dot(eqb, tri) + run
            take = jnp.where(key > tau, 1.0, jnp.where(pre <= need, eq, 0.0))
            take = jnp.where(key > KEY_NEG_INF, take, 0.0)
            madd_s[:, pl.ds(off, LANES)] = jnp.where(take > 0.5, 0.0, NEG)
            return run + _dot(eqb, ones)

        lax.fori_loop(0, ngk * kg, body, jnp.zeros((tq, LANES), F32))

    outs = []
    for n in range(N_KV):
        qs = qs_s[n]

        m_s[...] = jnp.full((R, LANES), NEG, F32)
        l_s[...] = jnp.zeros((R, LANES), F32)
        acc_s[...] = jnp.zeros((R, HD), F32)

        def pass_a(gi, carry, n=n):
            off = pl.multiple_of(gi * AW, AW)
            kc = kk_s[n, pl.ds(off, AW), :]
            dds = [jnp.clip(j - (gi * ag + u), 0, 2) for u in range(ag)]
            for g in range(G_A):
                rows = pl.ds(g * tq, tq)
                s = _dot_nt(qs_s[n, rows, :], kc)
                mx = m_s[rows, :]
                for u in range(ag):
                    cols = pl.ds(off + u * LANES, LANES)
                    su = s[:, u * LANES:(u + 1) * LANES] + bias_ref[dds[u], n, rows, :] + madd_s[:, cols]
                    s_s[rows, cols] = su
                    mx = jnp.maximum(mx, su)
                m_s[rows, :] = mx
            return carry

        lax.fori_loop(0, nga, pass_a, 0)
        mb = jnp.broadcast_to(jnp.max(m_s[...], axis=1, keepdims=True), (R, LANES))
        m_s[...] = mb

        def pass_b(gi, carry, n=n):
            off = pl.multiple_of(gi * AW, AW)
            vc = vv_s[n, pl.ds(off, AW), :]
            for g in range(G_A):
                rows = pl.ds(g * tq, tq)
                mrow = m_s[rows, :]
                lacc = l_s[rows, :]
                ps = []
                for u in range(ag):
                    p = jnp.exp(s_s[rows, pl.ds(off + u * LANES, LANES)] - mrow)
                    lacc = lacc + p
                    ps.append(p.astype(BF16))
                l_s[rows, :] = lacc
                pcat = ps[0] if ag == 1 else jnp.concatenate(ps, axis=1)
                acc_s[rows, :] += _dot(pcat, vc)
            return carry

        lax.fori_loop(0, nga, pass_b, 0)
        o = acc_s[...] / jnp.sum(l_s[...], axis=1, keepdims=True)
        for g in range(G_A):
            outs.append(o[g * tq:(g + 1) * tq, :])
    o_ref[0] = jnp.concatenate(outs, axis=-1)


def _att_prompt(z3, kn3, kin3, P, bias_p, topk, tq):
    B, T, _ = z3.shape
    nblk = T // LANES
    kg = 4 if nblk % 4 == 0 else (2 if nblk % 2 == 0 else 1)
    ag = kg
    cst = lambda shp: pl.BlockSpec(shp, lambda b, j: tuple(0 for _ in shp))
    return pl.pallas_call(
        functools.partial(_attp_kernel, tq=tq, topk=topk, kg=kg, ag=ag),
        grid=(B, T // tq),
        in_specs=[pl.BlockSpec((1, tq, C_A), lambda b, j: (b, j, ZC_Q)),
                  pl.BlockSpec((1, tq, H_I * D_IDX), lambda b, j: (b, j, ZC_QI)),
                  pl.BlockSpec((1, tq, LANES), lambda b, j: (b, j, ZC_KW)),
                  pl.BlockSpec((1, T, N_KV * HD), lambda b, j: (b, 0, 0)),
                  pl.BlockSpec((1, T, N_KV * HD), lambda b, j: (b, 0, ZC_V)),
                  pl.BlockSpec((1, T, D_IDX), lambda b, j: (b, 0, 0)),
                  cst((1, C_A)), cst((3, N_KV, G_A * tq, LANES)), cst((C_A, C_A)), cst((LANES, LANES)),
                  cst((LANES, LANES))],
        out_specs=pl.BlockSpec((1, tq, C_A), lambda b, j: (b, j, 0)),
        out_shape=jax.ShapeDtypeStruct((B, T, C_A), F32),
        scratch_shapes=[pltpu.VMEM((N_KV, T, HD), BF16),
                        pltpu.VMEM((N_KV, T, HD), BF16),
                        pltpu.VMEM((T, D_IDX), BF16),
                        pltpu.VMEM((N_KV, G_A * tq, HD), BF16),
                        pltpu.VMEM((H_I, tq, D_IDX), BF16),
                        pltpu.VMEM((H_I, tq, kg * LANES), F32),
                        pltpu.VMEM((tq, T), I32),
                        pltpu.VMEM((tq, T), I16),
                        pltpu.VMEM((tq, T), I16),
                        pltpu.VMEM((tq, T), F32),
                        pltpu.VMEM((G_A * tq, T), F32),
                        pltpu.VMEM((G_A * tq, LANES), F32),
                        pltpu.VMEM((G_A * tq, LANES), F32),
                        pltpu.VMEM((G_A * tq, HD), F32)],
        compiler_params=_cparams(("parallel", "arbitrary")),
        name="dsa_prompt",
    )(z3, z3, z3, kn3, z3, kin3, P["q_norm8"], bias_p.reshape(3, N_KV, G_A * tq, LANES), P["gseg8"], P["tri"],
      P["ones"])


def _pad_transpose(x):
    r, c = x.shape
    if c < LANES:
        x = jnp.concatenate([x, jnp.zeros((r, LANES - c), F32)], axis=1)
    x = jnp.concatenate([x, jnp.zeros((LANES - r, LANES), F32)], axis=0)
    return jnp.transpose(x)


def _atts_kernel(pt_ref, ckT_ref, cvT_ref, ckiT_ref, q_ref, qi_ref, kw_ref, kn_ref, vn_ref, kin_ref, qg_ref,
                 bias_ref, gseg_ref, tri_ref, ones_ref,
                 o_ref,
                 kT_s, vT_s, kiT_s, madd_s, sem, *, layer, npages, nbatch, Ts, topk):
    b = pl.program_id(0)
    slot = b % 2
    P = LANES
    L = (npages + 1) * P

    def page_copies(seq, sl, p):
        pg = pt_ref[seq, p]
        col = pl.multiple_of(p * P, P)
        return (pltpu.make_async_copy(ckT_ref.at[layer, pg], kT_s.at[sl, :, :, pl.ds(col, P)], sem.at[sl, 0]),
                pltpu.make_async_copy(cvT_ref.at[layer, pg], vT_s.at[sl, :, :, pl.ds(col, P)], sem.at[sl, 1]),
                pltpu.make_async_copy(ckiT_ref.at[layer, pg], kiT_s.at[sl, :, pl.ds(col, P)], sem.at[sl, 2]))

    def start_pages(seq, sl):
        def body(p, carry):
            for cp in page_copies(seq, sl, p):
                cp.start()
            return carry
        lax.fori_loop(0, npages, body, 0)

    def wait_pages(seq, sl):
        def body(p, carry):
            for cp in page_copies(seq, sl, p):
                cp.wait()
            return carry
        lax.fori_loop(0, npages, body, 0)

    @pl.when(b == 0)
    def _():
        start_pages(0, 0)

    @pl.when(b + 1 < nbatch)
    def _():
        start_pages(b + 1, 1 - slot)

    knT = _pad_transpose(kn_ref[0])
    vnT = _pad_transpose(vn_ref[0])
    for n in range(N_KV):
        kT_s[slot, n, :, npages * P:L] = knT[n * HD:(n + 1) * HD, :]
        vT_s[slot, n, :, npages * P:L] = vnT[n * HD:(n + 1) * HD, :]
    kiT_s[slot, :, npages * P:L] = _pad_transpose(kin_ref[0])[0:D_IDX, :]

    qn = _q_prep(q_ref[0], qg_ref[...], gseg_ref[...])
    qi = (qi_ref[0] * (D_IDX ** -0.5)).astype(BF16)
    qis = jnp.concatenate([qi[:, h * D_IDX:(h + 1) * D_IDX] for h in range(H_I)], axis=0)
    kw = kw_ref[0]

    wait_pages(b, slot)

    d = _dot(qis, kiT_s[slot].astype(BF16))
    s = jnp.zeros((Ts, L), F32)
    for h in range(H_I):
        s = s + jnp.maximum(d[h * Ts:(h + 1) * Ts, :], 0.0) * (kw[:, D_IDX + h:D_IDX + h + 1] * (H_I ** -0.5))
    col = lax.broadcasted_iota(I32, (Ts, L), 1)
    row = lax.broadcasted_iota(I32, (Ts, L), 0) + npages * P
    key = _sort_key(jnp.where(col <= row, s, -jnp.inf))

    def bit_body(it, cur):
        cand = cur + lax.shift_left(jnp.int32(1), jnp.asarray(31 - it, dtype=I32))
        cnt = jnp.sum(jnp.where(key >= cand, 1.0, 0.0), axis=1, keepdims=True)
        return jnp.where(cnt >= float(topk), cand, cur)

    tau = lax.fori_loop(0, 32, bit_body, jnp.full((Ts, 1), INT_MIN, I32))
    need = float(topk) - jnp.sum(jnp.where(key > tau, 1.0, 0.0), axis=1, keepdims=True)
    cge = jnp.sum(jnp.where(key >= tau, 1.0, 0.0), axis=1, keepdims=True)
    ties = jnp.max(cge) > float(topk)

    @pl.when(jnp.logical_not(ties))
    def _():
        madd_s[...] = jnp.where(key >= tau, jnp.where(key > KEY_NEG_INF, 0.0, NEG), NEG)

    @pl.when(ties)
    def _():
        tri = tri_ref[...]
        ones = ones_ref[...]
        eq = jnp.where(key == tau, 1.0, 0.0)
        run = jnp.zeros((Ts, LANES), F32)
        for c in range(L // LANES):
            sl = slice(c * LANES, (c + 1) * LANES)
            eqc = eq[:, sl]
            eqb = eqc.astype(BF16)
            pre = _dot(eqb, tri) + run
            take = jnp.where(key[:, sl] > tau, 1.0, jnp.where(pre <= need, eqc, 0.0))
            take = jnp.where(key[:, sl] > KEY_NEG_INF, take, 0.0)
            madd_s[:, sl] = jnp.where(take > 0.5, 0.0, NEG)
            run = run + _dot(eqb, ones)

    madd = madd_s[...]
    madd4 = jnp.concatenate([madd] * G_A, axis=0)
    outs = []
    for n in range(N_KV):
        qs = jnp.concatenate([qn[:, (n * G_A + g) * HD:(n * G_A + g + 1) * HD] for g in range(G_A)], axis=0)
        sc = _dot(qs, kT_s[slot, n].astype(BF16)) + bias_ref[n] + madd4
        m = jnp.max(sc, axis=1, keepdims=True)
        pe = jnp.exp(sc - m)
        l = jnp.sum(pe, axis=1, keepdims=True)
        o = _dot_nt(pe.astype(BF16), vT_s[slot, n].astype(BF16)) / l
        for g in range(G_A):
            outs.append(o[g * Ts:(g + 1) * Ts, :])
    o_ref[0] = jnp.concatenate(outs, axis=-1)


def _att_sample(layer, page_table, ckT, cvT, ckiT, z3, kn3, kin3, P, bias_s, topk):
    B, Ts, _ = z3.shape
    npages = page_table.shape[1]
    L = (npages + 1) * LANES
    hbm = pl.BlockSpec(memory_space=pl.ANY)
    cst = lambda shp: pl.BlockSpec(shp, lambda b, pt: tuple(0 for _ in shp))
    grid_spec = pltpu.PrefetchScalarGridSpec(
        num_scalar_prefetch=1,
        grid=(B,),
        in_specs=[hbm, hbm, hbm,
                  pl.BlockSpec((1, Ts, C_A), lambda b, pt: (b, 0, ZC_Q)),
                  pl.BlockSpec((1, Ts, H_I * D_IDX), lambda b, pt: (b, 0, ZC_QI)),
                  pl.BlockSpec((1, Ts, LANES), lambda b, pt: (b, 0, ZC_KW)),
                  pl.BlockSpec((1, Ts, N_KV * HD), lambda b, pt: (b, 0, 0)),
                  pl.BlockSpec((1, Ts, N_KV * HD), lambda b, pt: (b, 0, ZC_V)),
                  pl.BlockSpec((1, Ts, D_IDX), lambda b, pt: (b, 0, 0)),
                  cst((1, C_A)), cst((N_KV, G_A * Ts, L)), cst((C_A, C_A)), cst((LANES, LANES)),
                  cst((LANES, LANES))],
        out_specs=pl.BlockSpec((1, Ts, C_A), lambda b, pt: (b, 0, 0)),
        scratch_shapes=[pltpu.VMEM((2, N_KV, HD, L), F32),
                        pltpu.VMEM((2, N_KV, HD, L), F32),
                        pltpu.VMEM((2, D_IDX, L), F32),
                        pltpu.VMEM((Ts, L), F32),
                        pltpu.SemaphoreType.DMA((2, 3))])
    return pl.pallas_call(
        functools.partial(_atts_kernel, layer=layer, npages=npages, nbatch=B, Ts=Ts, topk=topk),
        grid_spec=grid_spec,
        out_shape=jax.ShapeDtypeStruct((B, Ts, C_A), F32),
        compiler_params=_cparams(("arbitrary",)),
        name="dsa_sample",
    )(page_table, ckT, cvT, ckiT, z3, z3, z3, kn3, z3, kin3, P["q_norm8"], bias_s, P["gseg8"], P["tri"], P["ones"])


def _out_proj_kernel(x_ref, orw_ref, ocv_ref, oat_ref, w_ref, y_ref):
    acc = _dot(orw_ref[...].astype(BF16), w_ref[0:C_R, :])
    acc = acc + _dot(ocv_ref[...].astype(BF16), w_ref[C_R:C_R + C_CONV, :])
    acc = acc + _dot(oat_ref[...].astype(BF16), w_ref[C_R + C_CONV:, :])
    y_ref[...] = x_ref[...] + acc


def _out_proj(x2, orw, ocv, oat, w_bf, tm):
    M, D = x2.shape
    row = lambda n: pl.BlockSpec((tm, n), lambda i: (i, 0))
    return pl.pallas_call(
        _out_proj_kernel,
        grid=(M // tm,),
        in_specs=[row(D), row(C_R), row(C_CONV), row(C_A), pl.BlockSpec((D, D), lambda i: (0, 0))],
        out_specs=row(D),
        out_shape=jax.ShapeDtypeStruct((M, D), F32),
        compiler_params=_cparams(("parallel",)),
        name="out_proj",
    )(x2, orw, ocv, oat, w_bf)


def _gelu(x):
    return 0.5 * x * (1.0 + lax.erf(x * (2.0 ** -0.5)))


def _ffn_kernel(x_ref, p_ref, ng_ref, upg_ref, upv_ref, cw_ref, cb_ref, dn_ref, h_ref, h2_ref, pg_ref, pgate_ref,
                pproj_ref,
                xo_ref, tail_ref,
                xn_s, acc_s, hist_s, *, tm, nF, seq_tiles, seq_len):
    i = pl.program_id(0)
    f = pl.program_id(1)

    @pl.when(f == 0)
    def _():
        xn_s[...] = _rms(x_ref[...], ng_ref[...]).astype(BF16)
        acc_s[...] = jnp.zeros(acc_s.shape, F32)

    xn = xn_s[...]
    g = _dot(xn, upg_ref[...])
    val = _dot(xn, upv_ref[...])
    row = lax.broadcasted_iota(I32, g.shape, 0)
    if seq_tiles is not None:
        first = (i % seq_tiles) == 0
        hs = hist_s[f]
        h0 = h_ref[0]
        hm2 = jnp.where(first, h0[0:1, :], hs[SUBLANES - 2:SUBLANES - 1, :])
        hm1 = jnp.where(first, h0[1:2, :], hs[SUBLANES - 1:SUBLANES, :])
        g1 = jnp.where(row == 0, hm1, pltpu.roll(g, 1, 0))
        g2 = jnp.where(row == 0, hm2, jnp.where(row == 1, hm1, pltpu.roll(g, 2, 0)))
        slab = g[tm - SUBLANES:tm, :]
        hist_s[f] = slab
        tail_ref[0] = slab
    else:
        t = row % seq_len
        g1 = jnp.where(t == 0, 0.0, pltpu.roll(g, 1, 0)) + h_ref[...]
        g2 = jnp.where(t < 2, 0.0, pltpu.roll(g, 2, 0)) + h2_ref[...]
        tail_ref[...] = g
    cw = cw_ref[...]
    gate = cw[0:1, :] * g2 + cw[1:2, :] * g1 + cw[2:3, :] * g + cb_ref[...]
    act = (_gelu(gate) * val).astype(BF16)
    acc_s[...] += _dot(act, dn_ref[...])

    @pl.when(f == nF - 1)
    def _():
        x2 = x_ref[...] + acc_s[...]
        xn2 = _rms(x2, pg_ref[...]).astype(BF16)
        gate2 = _sigmoid(_dot(xn2, pgate_ref[...]))
        xo_ref[...] = x2 + _dot(p_ref[...].astype(BF16), pproj_ref[...]) * gate2


def _ffn(x2, p2, ffn0, P, tm, B, T):
    M, D = x2.shape
    tf = 256
    nF = D_FF // tf
    DP = p2.shape[1]
    whole = tm % T == 0
    if whole:
        seq_tiles, seq_len = None, T
        h1 = jnp.pad(ffn0[:, 1:2, :], ((0, 0), (0, T - 1), (0, 0))).reshape(M, D_FF)
        h2 = jnp.pad(ffn0, ((0, 0), (0, T - 2), (0, 0))).reshape(M, D_FF)
        h_specs = [pl.BlockSpec((tm, tf), lambda i, f: (i, f)), pl.BlockSpec((tm, tf), lambda i, f: (i, f))]
        tail_spec = pl.BlockSpec((tm, tf), lambda i, f: (i, f))
        tail_shape = jax.ShapeDtypeStruct((M, D_FF), F32)
    else:
        seq_tiles, seq_len = T // tm, T
        h1, h2 = ffn0, ffn0
        h_specs = [pl.BlockSpec((1, 2, tf), lambda i, f: (i // seq_tiles, 0, f)),
                   pl.BlockSpec((1, 2, tf), lambda i, f: (i // seq_tiles, 0, f))]
        tail_spec = pl.BlockSpec((1, SUBLANES, tf), lambda i, f: (i, 0, f))
        tail_shape = jax.ShapeDtypeStruct((M // tm, SUBLANES, D_FF), F32)
    cst = lambda shp: pl.BlockSpec(shp, lambda i, f: tuple(0 for _ in shp))
    xo, tail = pl.pallas_call(
        functools.partial(_ffn_kernel, tm=tm, nF=nF, seq_tiles=seq_tiles, seq_len=seq_len),
        grid=(M // tm, nF),
        in_specs=[pl.BlockSpec((tm, D), lambda i, f: (i, 0)),
                  pl.BlockSpec((tm, DP), lambda i, f: (i, 0)),
                  cst((1, D)),
                  pl.BlockSpec((D, tf), lambda i, f: (0, f)),
                  pl.BlockSpec((D, tf), lambda i, f: (0, nF + f)),
                  pl.BlockSpec((3, tf), lambda i, f: (0, f)),
                  pl.BlockSpec((1, tf), lambda i, f: (0, f)),
                  pl.BlockSpec((tf, D), lambda i, f: (f, 0))] + h_specs +
                 [cst((1, D)), cst((D, D)), cst((DP, D))],
        out_specs=[pl.BlockSpec((tm, D), lambda i, f: (i, 0)), tail_spec],
        out_shape=[jax.ShapeDtypeStruct((M, D), F32), tail_shape],
        scratch_shapes=[pltpu.VMEM((tm, D), BF16), pltpu.VMEM((tm, D), F32), pltpu.VMEM((nF, SUBLANES, tf), F32)],
        compiler_params=_cparams(("parallel", "arbitrary")),
        name="conv_ffn_ple",
    )(x2, p2, P["norm_ffn"], P["ffn_up"], P["ffn_up"], P["ffn_conv_w"], P["ffn_conv_b"], P["ffn_down"], h1, h2,
      P["ple_norm"], P["ple_gate"], P["ple_proj"])
    if whole:
        ffn_T = tail.reshape(B, T, D_FF)[:, T - 2:, :]
    else:
        ffn_T = tail.reshape(B, seq_tiles, SUBLANES, D_FF)[:, seq_tiles - 1, SUBLANES - 2:, :]
    return xo, ffn_T


def _seg_ones(n, seg):
    i = np.arange(n)
    return jnp.asarray((i[:, None] // seg) == (i[None, :] // seg), dtype=BF16)


def _hl(w):
    hi = w.astype(BF16)
    return hi, (w - hi.astype(F32)).astype(BF16)


def _layer_params(i, W):
    P = {}
    w_in = W["w_in"][i]
    D = w_in.shape[0]
    P["w_in"] = jnp.concatenate([w_in[:, :N_RWKV_COLS], jnp.zeros((D, 1024 - N_RWKV_COLS), F32),
                                 w_in[:, N_RWKV_COLS:], jnp.zeros((D, NZ - 128 - N_IN), F32)], axis=1).astype(BF16)
    row = lambda v: v.reshape(1, -1)
    P["norm_mix"] = row(W["norm_mix"][i])
    P["mu"] = row(W["mu_shift"][i])
    P["w0"] = row(W["w0"][i])
    P["a0"] = row(W["a0"][i])
    z = lambda r: jnp.zeros((r, C_R), F32)
    P["wl_hi"], P["wl_lo"] = _hl(jnp.concatenate([W["w_lora"][i], z(96)], axis=0))
    P["al_hi"], P["al_lo"] = _hl(jnp.concatenate([z(32), W["a_lora"][i], z(64)], axis=0))
    P["gl_hi"], P["gl_lo"] = _hl(jnp.concatenate([z(64), W["g_lora"][i]], axis=0))
    P["k_k"] = row(W["k_k"][i])
    P["k_a"] = row(W["k_a"][i])
    P["r_k"] = row(W["r_k"][i])
    P["lnx_g"] = row(W["lnx_g"][i])
    P["lnx_b"] = row(W["lnx_b"][i])
    P["conv_w"] = W["conv_w"][i]
    P["conv_b"] = row(W["conv_b"][i])
    P["conv_ln_g"] = row(W["conv_ln_g"][i])
    P["conv_ln_b"] = row(W["conv_ln_b"][i])
    P["q_norm8"] = row(jnp.tile(W["q_norm"][i], H_A))
    P["k_norm2"] = row(jnp.tile(W["k_norm"][i], N_KV))
    P["kidx_norm"] = row(W["kidx_norm"][i])
    P["w_out"] = W["w_out"][i].astype(BF16)
    P["norm_ffn"] = row(W["norm_ffn"][i])
    P["ffn_up"] = W["ffn_up"][i].astype(BF16)
    P["ffn_conv_w"] = W["ffn_conv_w"][i]
    P["ffn_conv_b"] = row(W["ffn_conv_b"][i])
    P["ffn_down"] = W["ffn_down"][i].astype(BF16)
    P["ple_norm"] = row(W["ple_norm"][i])
    P["ple_gate"] = W["ple_gate"][i].astype(BF16)
    P["ple_proj"] = W["ple_proj"][i].astype(BF16)
    P["gseg4"] = _seg_ones(C_R, HD)
    P["gseg2"] = _seg_ones(N_KV * HD, HD)
    P["gseg8"] = _seg_ones(C_A, HD)
    P["tri"] = jnp.asarray(np.arange(LANES)[:, None] <= np.arange(LANES)[None, :], dtype=BF16)
    P["ones"] = jnp.ones((LANES, LANES), BF16)
    return P


def _pick_tile(n, pref):
    t = min(n, pref)
    while n % t:
        t //= 2
    return t


def _layer(layer, x, p_i, st, P, att_fn, bb):
    B, T, D = x.shape
    M = B * T
    shift0, wkv0, conv0, ffn0 = st
    x2 = x.reshape(M, D)
    tm = _pick_tile(M, 512)
    z2 = _in_proj(x2, P["norm_mix"], P["w_in"], tm)
    z3 = z2.reshape(B, T, NZ)
    Tc = _pick_tile(T, 256)
    o_rw, wkv_T = _rwkv(z3, shift0.reshape(B, 1, N_RWKV_COLS), wkv0, P, bb, Tc)
    conv0p = jnp.pad(conv0, ((0, 0), (HIST - (CONV_W - 1), 0), (0, 0)))
    o_cv, conv_Tp = _conv(z3, conv0p, P, _pick_tile(T, 512))
    kn2, kin2 = _att_prep(z2, P, tm)
    kn3 = kn2.reshape(B, T, N_KV * HD)
    kin3 = kin2.reshape(B, T, D_IDX)
    o_at = att_fn(layer, z3, kn3, kin3, P)
    xa = _out_proj(x2, o_rw.reshape(M, C_R), o_cv.reshape(M, C_CONV), o_at.reshape(M, C_A), P["w_out"], tm)
    tmf = M if M <= 1024 else _pick_tile(T, 1024)
    xo, ffn_T = _ffn(xa, p_i.reshape(M, -1), ffn0, P, tmf, B, T)
    k_new = kn3.reshape(B, T, N_KV, HD)
    v_new = z3[:, :, ZC_V * LANES:(ZC_V + 1) * LANES].reshape(B, T, N_KV, HD)
    shift_T = z3[:, T - 1, :N_RWKV_COLS]
    conv_T = conv_Tp[:, HIST - (CONV_W - 1):, :]
    return xo.reshape(B, T, D), (k_new, v_new, kin3, wkv_T, shift_T, conv_T, ffn_T)


def kernel(x_prompt, x_sample, cache_k, cache_v, cache_kidx, state_wkv, state_shift, state_conv, state_ffn, page_table, p_prompt, p_sample, norm_mix, w_in, w_out, mu_shift, w0, w_lora, a0, a_lora, g_lora, k_k, k_a, r_k, lnx_g, lnx_b, conv_w, conv_b, conv_ln_g, conv_ln_b, q_norm, k_norm, kidx_norm, rel_bias, norm_ffn, ffn_up, ffn_conv_w, ffn_conv_b, ffn_down, ple_norm, ple_proj, ple_gate):
    W = dict(norm_mix=norm_mix, w_in=w_in, w_out=w_out, mu_shift=mu_shift, w0=w0, w_lora=w_lora, a0=a0,
             a_lora=a_lora, g_lora=g_lora, k_k=k_k, k_a=k_a, r_k=r_k, lnx_g=lnx_g, lnx_b=lnx_b,
             conv_w=conv_w, conv_b=conv_b, conv_ln_g=conv_ln_g, conv_ln_b=conv_ln_b, q_norm=q_norm,
             k_norm=k_norm, kidx_norm=kidx_norm, norm_ffn=norm_ffn, ffn_up=ffn_up,
             ffn_conv_w=ffn_conv_w, ffn_conv_b=ffn_conv_b, ffn_down=ffn_down, ple_norm=ple_norm,
             ple_proj=ple_proj, ple_gate=ple_gate)
    depth = w_in.shape[0]
    params = [_layer_params(i, W) for i in range(depth)]

    B, T, D = x_prompt.shape
    tq = LANES
    topk_p = min(TOPK_MAX, T // 4)
    t_ = np.arange(tq)[:, None]
    s_ = np.arange(LANES)[None, :]
    tiles = np.stack([_rel_bucket_np(np.maximum(t_ - s_, 0)), _rel_bucket_np(tq + t_ - s_),
                      np.full((tq, LANES), N_BUCKETS - 1, np.int32)])
    idx_p = np.broadcast_to(tiles[:, None], (3, H_A, tq, LANES)).reshape(3 * H_A, tq, LANES)
    bias_p = _bias_expand(rel_bias, jnp.asarray(idx_p)).reshape(3, H_A, tq, LANES)

    def att_p(layer, z3, kn3, kin3, P):
        return _att_prompt(z3, kn3, kin3, P, bias_p, topk_p, tq)

    st0 = (jnp.zeros((B, N_RWKV_COLS), F32), jnp.zeros((B, H_R, HD, HD), F32),
           jnp.zeros((B, CONV_W - 1, C_CONV), F32), jnp.zeros((B, 2, D_FF), F32))
    x = x_prompt
    new_p = []
    for i in range(depth):
        x, new = _layer(i, x, p_prompt[i], st0, params[i], att_p, B)
        new_p.append(new)
    y_prompt = x

    Bs, Ts, _ = x_sample.shape
    npages = page_table.shape[1]
    psz = cache_k.shape[2]
    past_len = npages * psz
    topk_s = min(TOPK_MAX, (past_len + Ts) // 4)
    Ls = past_len + psz
    qpos = past_len + np.arange(Ts)[:, None]
    bk = _rel_bucket_np(np.maximum(qpos - np.arange(Ls)[None, :], 0))
    idx_s = np.broadcast_to(bk[None], (H_A, Ts, Ls))
    bias_s = _bias_expand(rel_bias, jnp.asarray(idx_s)).reshape(N_KV, G_A * Ts, Ls)
    ckT = jnp.transpose(cache_k, (0, 1, 3, 4, 2))
    cvT = jnp.transpose(cache_v, (0, 1, 3, 4, 2))
    ckiT = jnp.transpose(cache_kidx, (0, 1, 3, 2))

    def att_s(layer, z3, kn3, kin3, P):
        return _att_sample(layer, page_table, ckT, cvT, ckiT, z3, kn3, kin3, P, bias_s, topk_s)

    x = x_sample
    new_s = []
    for i in range(depth):
        st = (state_shift[i], state_wkv[i], state_conv[i], state_ffn[i])
        x, new = _layer(i, x, p_sample[i], st, params[i], att_s, 4)
        new_s.append(new)
    y_sample = x

    stack = lambda lst: tuple(jnp.stack(t) for t in zip(*lst))
    return (y_prompt, y_sample) + stack(new_p) + stack(new_s)
```

```python
import functools
import math

import numpy as np
import jax
import jax.numpy as jnp
from jax import lax
from jax.experimental import pallas as pl
from jax.experimental.pallas import tpu as pltpu

F32 = jnp.float32
BF16 = jnp.bfloat16
I32 = jnp.int32
I16 = jnp.int16

HD = 64
C_R = 256
H_R = C_R // HD
C_CONV = 256
CONV_W = 31
C_A = 512
H_A = C_A // HD
N_KV = 2
G_A = H_A // N_KV
H_I = 4
D_IDX = 64
TOPK_MAX = 256
N_BUCKETS = 32
MAX_DISTANCE = 128
D_FF = 2816
NORM_EPS = 1e-6
RWKV_LN_EPS = 64e-5
CONV_LN_EPS = 1e-5
N_RWKV_COLS = 896
N_IN = 2500

LANES = 128
SUBLANES = 8
VMEM_LIMIT = 56 * 1024 * 1024

NZ = 2688
ZC_CONV = 1024 // 512
ZC_Q = 1536 // 512
ZC_K = 2048 // 128
ZC_V = 2176 // 128
ZC_QI = 2304 // 256
ZC_KW = 2560 // 128

HIST = 32
NEG = -1e30
INT_MIN = -2 ** 31
I16_MIN = -2 ** 15
KEY_NEG_INF = int(np.array(0xFF800000 ^ 0x7FFFFFFF, dtype=np.uint32).view(np.int32))


def _cparams(sem):
    return pltpu.CompilerParams(dimension_semantics=sem, vmem_limit_bytes=VMEM_LIMIT)


def _split2(x):
    hi = x.astype(BF16)
    lo = (x - hi.astype(F32)).astype(BF16)
    return hi, lo


def _dot(a, b):
    return jnp.dot(a, b, preferred_element_type=F32)


def _dot_nt(a, b):
    return lax.dot_general(a, b, (((1,), (1,)), ((), ())), preferred_element_type=F32)


def _dot_exactw(x, w_bf):
    hi, lo = _split2(x)
    return _dot(hi, w_bf) + _dot(lo, w_bf)


def _dot_hl(x, w_hi, w_lo):
    hi, lo = _split2(x)
    return _dot(hi, w_hi) + (_dot(lo, w_hi) + _dot(hi, w_lo))


def _sigmoid(x):
    return 1.0 / (1.0 + jnp.exp(-x))


def _rms(x, g):
    ms = jnp.mean(x * x, axis=-1, keepdims=True)
    return x * lax.rsqrt(ms + NORM_EPS) * g


def _in_proj_kernel(x_ref, g_ref, w_ref, z_ref, xn_s):
    @pl.when(pl.program_id(1) == 0)
    def _():
        xn_s[...] = _rms(x_ref[...], g_ref[...]).astype(BF16)

    z_ref[...] = _dot(xn_s[...], w_ref[...])


def _in_proj(x2, g, w_bf, tm):
    M, D = x2.shape
    tn = 896
    return pl.pallas_call(
        _in_proj_kernel,
        grid=(M // tm, NZ // tn),
        in_specs=[pl.BlockSpec((tm, D), lambda i, j: (i, 0)),
                  pl.BlockSpec((1, D), lambda i, j: (0, 0)),
                  pl.BlockSpec((D, tn), lambda i, j: (0, j))],
        out_specs=pl.BlockSpec((tm, tn), lambda i, j: (i, j)),
        out_shape=jax.ShapeDtypeStruct((M, NZ), F32),
        scratch_shapes=[pltpu.VMEM((tm, D), BF16)],
        compiler_params=_cparams(("parallel", "arbitrary")),
        name="in_proj",
    )(x2, g, w_bf)


def _rwkv_kernel(z_ref, sh0_ref, wkv0_ref, mu_ref, w0_ref, wl_hi, wl_lo, a0_ref, al_hi, al_lo, gl_hi, gl_lo,
                 kk_ref, ka_ref, rk_ref, lg_ref, lb_ref, gseg_ref,
                 o_ref, st_ref,
                 S_s, carry_s, w_s, k_s, v_s, a_s, b_s, r_s, y_s, bon_s, g_s, *, bb, Tc):
    c = pl.program_id(1)

    @pl.when(c == 0)
    def _():
        S_s[...] = wkv0_ref[...]
        carry_s[...] = sh0_ref[...]

    gseg = gseg_ref[...]
    mu = mu_ref[...]
    rid = lax.broadcasted_iota(I32, (Tc, N_RWKV_COLS), 0)
    for b in range(bb):
        z = z_ref[b]
        prev = jnp.where(rid == 0, carry_s[b], pltpu.roll(z, 1, 0))
        carry_s[b] = z[Tc - 1:Tc, :]
        zs = z + (prev - z) * mu
        r = zs[:, 0:C_R]
        k = zs[:, C_R:2 * C_R]
        v = zs[:, 2 * C_R:3 * C_R]
        t6 = zs[:, 3 * C_R:]
        lw = w0_ref[...] + _dot_hl(jnp.tanh(t6), wl_hi[...], wl_lo[...])
        nl = -lw
        softplus = jnp.maximum(nl, 0.0) + jnp.log(1.0 + jnp.exp(-jnp.abs(nl)))
        decay = jnp.exp(-jnp.exp(-softplus - 0.5))
        a = _sigmoid(a0_ref[...] + _dot_hl(t6, al_hi[...], al_lo[...]))
        g_s[b] = _dot_hl(_sigmoid(t6), gl_hi[...], gl_lo[...])
        kk = k * kk_ref[...]
        nrm = jnp.sqrt(_dot_exactw(kk * kk, gseg))
        kk = kk / jnp.maximum(nrm, 1e-12)
        k2 = k * (1.0 + (a - 1.0) * ka_ref[...])
        bon_s[b] = _dot_exactw(r * k2 * rk_ref[...], gseg) * v
        na = -kk
        kb = kk * a
        for h in range(H_R):
            sl = slice(h * HD, (h + 1) * HD)
            w_s[b, h] = decay[:, sl]
            k_s[b, h] = k2[:, sl]
            v_s[b, h] = v[:, sl]
            a_s[b, h] = na[:, sl]
            b_s[b, h] = kb[:, sl]
            r_s[b, h] = r[:, sl]

    eye = (lax.broadcasted_iota(I32, (HD, HD), 0) == lax.broadcasted_iota(I32, (HD, HD), 1)).astype(F32)
    chains = [(b, h) for b in range(bb) for h in range(H_R)]

    def sub(i, carry):
        t0 = pl.multiple_of(i * SUBLANES, SUBLANES)
        S = [S_s[b, h] for b, h in chains]
        for j in range(SUBLANES):
            t = t0 + j
            for ci, (b, h) in enumerate(chains):
                row = lambda ref: ref[b, h, pl.ds(t, 1), :]
                sa = jnp.sum(S[ci] * row(a_s), axis=1, keepdims=True)
                vc = jnp.sum(eye * row(v_s), axis=1, keepdims=True)
                Sn = S[ci] * row(w_s) + sa * row(b_s) + vc * row(k_s)
                S[ci] = Sn
                y_s[b, h, pl.ds(t, 1), :] = _dot_nt(row(r_s).astype(BF16), Sn.astype(BF16))
        for ci, (b, h) in enumerate(chains):
            S_s[b, h] = S[ci]
        return carry

    lax.fori_loop(0, Tc // SUBLANES, sub, 0)

    for b in range(bb):
        y = jnp.concatenate([y_s[b, h] for h in range(H_R)], axis=-1)
        mean = _dot_exactw(y, gseg) * (1.0 / HD)
        d = y - mean
        var = _dot_exactw(d * d, gseg) * (1.0 / HD)
        yn = d * lax.rsqrt(var + RWKV_LN_EPS) * lg_ref[...] + lb_ref[...]
        o_ref[b] = (yn + bon_s[b]) * g_s[b]
    st_ref[...] = S_s[...]


def _rwkv(z3, shift0, wkv0, P, bb, Tc):
    B, T, _ = z3.shape
    vec = lambda n: pl.BlockSpec((1, n), lambda bi, c: (0, 0))
    mat = lambda r, n: pl.BlockSpec((r, n), lambda bi, c: (0, 0))
    kern = functools.partial(_rwkv_kernel, bb=bb, Tc=Tc)
    hv = lambda: pltpu.VMEM((bb, H_R, Tc, HD), F32)
    return pl.pallas_call(
        kern,
        grid=(B // bb, T // Tc),
        in_specs=[pl.BlockSpec((bb, Tc, N_RWKV_COLS), lambda bi, c: (bi, c, 0)),
                  pl.BlockSpec((bb, 1, N_RWKV_COLS), lambda bi, c: (bi, 0, 0)),
                  pl.BlockSpec((bb, H_R, HD, HD), lambda bi, c: (bi, 0, 0, 0)),
                  vec(N_RWKV_COLS), vec(C_R), mat(LANES, C_R), mat(LANES, C_R), vec(C_R), mat(LANES, C_R),
                  mat(LANES, C_R), mat(LANES, C_R), mat(LANES, C_R),
                  vec(C_R), vec(C_R), vec(C_R), vec(C_R), vec(C_R), mat(C_R, C_R)],
        out_specs=[pl.BlockSpec((bb, Tc, C_R), lambda bi, c: (bi, c, 0)),
                   pl.BlockSpec((bb, H_R, HD, HD), lambda bi, c: (bi, 0, 0, 0))],
        out_shape=[jax.ShapeDtypeStruct((B, T, C_R), F32),
                   jax.ShapeDtypeStruct((B, H_R, HD, HD), F32)],
        scratch_shapes=[pltpu.VMEM((bb, H_R, HD, HD), F32),
                        pltpu.VMEM((bb, 1, N_RWKV_COLS), F32),
                        hv(), hv(), hv(), hv(), hv(), hv(), hv(),
                        pltpu.VMEM((bb, Tc, C_R), F32),
                        pltpu.VMEM((bb, Tc, C_R), F32)],
        compiler_params=_cparams(("parallel", "arbitrary")),
        name="rwkv7",
    )(z3, shift0, wkv0, P["mu"], P["w0"], P["wl_hi"], P["wl_lo"], P["a0"], P["al_hi"], P["al_lo"],
      P["gl_hi"], P["gl_lo"], P["k_k"], P["k_a"], P["r_k"], P["lnx_g"], P["lnx_b"], P["gseg4"])


def _conv_kernel(z_ref, c0_ref, w_ref, b_ref, lg_ref, lb_ref, o_ref, ct_ref, buf, *, Tc):
    @pl.when(pl.program_id(1) == 0)
    def _():
        buf[0:HIST, :] = c0_ref[0]

    z = z_ref[0]
    u = z[:, 0:C_CONV] * _sigmoid(z[:, C_CONV:])
    buf[HIST:HIST + Tc, :] = u
    w = w_ref[...]
    acc = jnp.zeros((Tc, C_CONV), F32) + b_ref[...]
    off = HIST - (CONV_W - 1)
    for j in range(CONV_W):
        acc = acc + w[j:j + 1, :] * buf[off + j:off + j + Tc, :]
    mu = jnp.mean(acc, axis=-1, keepdims=True)
    d = acc - mu
    var = jnp.mean(d * d, axis=-1, keepdims=True)
    cn = d * lax.rsqrt(var + CONV_LN_EPS) * lg_ref[...] + lb_ref[...]
    o_ref[0] = cn * _sigmoid(cn)
    hist = buf[Tc:Tc + HIST, :]
    buf[0:HIST, :] = hist
    ct_ref[0] = hist


def _conv(z3, conv0p, P, Tc):
    B, T, _ = z3.shape
    vec = pl.BlockSpec((1, C_CONV), lambda b, c: (0, 0))
    return pl.pallas_call(
        functools.partial(_conv_kernel, Tc=Tc),
        grid=(B, T // Tc),
        in_specs=[pl.BlockSpec((1, Tc, 2 * C_CONV), lambda b, c: (b, c, ZC_CONV)),
                  pl.BlockSpec((1, HIST, C_CONV), lambda b, c: (b, 0, 0)),
                  pl.BlockSpec((CONV_W, C_CONV), lambda b, c: (0, 0)),
                  vec, vec, vec],
        out_specs=[pl.BlockSpec((1, Tc, C_CONV), lambda b, c: (b, c, 0)),
                   pl.BlockSpec((1, HIST, C_CONV), lambda b, c: (b, 0, 0))],
        out_shape=[jax.ShapeDtypeStruct((B, T, C_CONV), F32),
                   jax.ShapeDtypeStruct((B, HIST, C_CONV), F32)],
        scratch_shapes=[pltpu.VMEM((HIST + Tc, C_CONV), F32)],
        compiler_params=_cparams(("parallel", "arbitrary")),
        name="conformer_conv",
    )(z3, conv0p, P["conv_w"], P["conv_b"], P["conv_ln_g"], P["conv_ln_b"])


def _att_prep_kernel(k_ref, kw_ref, kg_ref, kig_ref, gseg_ref, kn_ref, kin_ref):
    k = k_ref[...]
    ms = _dot_exactw(k * k, gseg_ref[...]) * (1.0 / HD)
    kn_ref[...] = k * lax.rsqrt(ms + NORM_EPS) * kg_ref[...]
    kw = kw_ref[...]
    ki = kw[:, 0:D_IDX]
    msi = jnp.mean(ki * ki, axis=-1, keepdims=True)
    kin_ref[...] = ki * lax.rsqrt(msi + NORM_EPS) * kig_ref[...]


def _att_prep(z2, P, tm):
    M = z2.shape[0]
    return pl.pallas_call(
        _att_prep_kernel,
        grid=(M // tm,),
        in_specs=[pl.BlockSpec((tm, LANES), lambda i: (i, ZC_K)),
                  pl.BlockSpec((tm, LANES), lambda i: (i, ZC_KW)),
                  pl.BlockSpec((1, LANES), lambda i: (0, 0)),
                  pl.BlockSpec((1, D_IDX), lambda i: (0, 0)),
                  pl.BlockSpec((LANES, LANES), lambda i: (0, 0))],
        out_specs=[pl.BlockSpec((tm, LANES), lambda i: (i, 0)),
                   pl.BlockSpec((tm, D_IDX), lambda i: (i, 0))],
        out_shape=[jax.ShapeDtypeStruct((M, N_KV * HD), F32),
                   jax.ShapeDtypeStruct((M, D_IDX), F32)],
        compiler_params=_cparams(("parallel",)),
        name="att_prep",
    )(z2, z2, P["k_norm2"], P["kidx_norm"], P["gseg2"])


def _bias_kernel(rb_ref, idx_ref, o_ref):
    h = pl.program_id(0) % H_A
    idx = idx_ref[0]
    out = jnp.zeros(idx.shape, F32)
    for bk in range(N_BUCKETS):
        out = jnp.where(idx == bk, rb_ref[bk, h], out)
    o_ref[0] = out


def _bias_expand(rel_bias, idx):
    S, R, L = idx.shape
    return pl.pallas_call(
        _bias_kernel,
        grid=(S,),
        in_specs=[pl.BlockSpec(memory_space=pltpu.SMEM),
                  pl.BlockSpec((1, R, L), lambda s: (s, 0, 0))],
        out_specs=pl.BlockSpec((1, R, L), lambda s: (s, 0, 0)),
        out_shape=jax.ShapeDtypeStruct((S, R, L), F32),
        compiler_params=_cparams(("parallel",)),
        name="rel_bias_tiles",
    )(rel_bias, idx)


def _rel_bucket_np(dist):
    max_exact = N_BUCKETS // 2
    d_f = np.maximum(dist, max_exact).astype(np.float32)
    large = max_exact + (np.log(d_f / np.float32(max_exact)) / np.float32(math.log(MAX_DISTANCE / max_exact))
                         * np.float32(N_BUCKETS - max_exact)).astype(np.int32)
    return np.where(dist < max_exact, dist, np.minimum(large, N_BUCKETS - 1)).astype(np.int32)


def _sort_key(s):
    bits = pltpu.bitcast(s + 0.0, I32)
    return jnp.where(bits < 0, bits ^ 0x7FFFFFFF, bits)


def _q_prep(q, qg, gseg8):
    ms = _dot_exactw(q * q, gseg8) * (1.0 / HD)
    return (q * lax.rsqrt(ms + NORM_EPS) * qg * (HD ** -0.5)).astype(BF16)


def _attp_kernel(q_ref, qi_ref, kw_ref, kn_ref, v_ref, kin_ref, qg_ref, bias_ref, gseg_ref, tri_ref, ones_ref,
                 o_ref,
                 kk_s, vv_s, ki_s, qs_s, qi_s, wb_s, key_s, hi_s, lo_s, tau_s, madd_s, s_s, m_s, l_s, acc_s,
                 *, tq, topk, kg, ag):
    j = pl.program_id(1)
    nb = j + 1
    KW = kg * LANES
    AW = ag * LANES
    ngk = (nb + kg - 1) // kg
    nga = (nb + ag - 1) // ag
    R = G_A * tq

    @pl.when(j == 0)
    def _():
        knT = jnp.transpose(kn_ref[0])
        v = v_ref[0]
        for n in range(N_KV):
            kk_s[n] = knT[n * HD:(n + 1) * HD, :].astype(BF16)
            vv_s[n] = v[:, n * HD:(n + 1) * HD].astype(BF16)
        kin = kin_ref[0]
        kinT = jnp.transpose(jnp.concatenate([kin, jnp.zeros(kin.shape, F32)], axis=1))
        ki_s[...] = kinT[0:D_IDX, :].astype(BF16)

    qn = _q_prep(q_ref[0], qg_ref[...], gseg_ref[...])
    for n in range(N_KV):
        qs_s[n] = jnp.concatenate([qn[:, (n * G_A + g) * HD:(n * G_A + g + 1) * HD] for g in range(G_A)], axis=0)
    qi = (qi_ref[0] * (D_IDX ** -0.5)).astype(BF16)
    kw = kw_ref[0]
    for h in range(H_I):
        qi_s[h] = qi[:, h * D_IDX:(h + 1) * D_IDX]
        wb_s[h] = jnp.broadcast_to(kw[:, D_IDX + h:D_IDX + h + 1] * (H_I ** -0.5), (tq, KW))

    row = lax.broadcasted_iota(I32, (tq, KW), 0) + j * tq
    col0 = lax.broadcasted_iota(I32, (tq, KW), 1)

    def idx_body(gi, carry):
        off = pl.multiple_of(gi * KW, KW)
        kc = ki_s[:, pl.ds(off, KW)]
        s = jnp.zeros((tq, KW), F32)
        for h in range(H_I):
            s = s + jnp.maximum(_dot(qi_s[h], kc), 0.0) * wb_s[h]
        s = jnp.where(col0 + off <= row, s, -jnp.inf)
        key = _sort_key(s)
        key_s[:, pl.ds(off, KW)] = key
        hi_s[:, pl.ds(off, KW)] = lax.shift_right_arithmetic(key, 16).astype(I16)
        lo_s[:, pl.ds(off, KW)] = ((key & 0xFFFF) + I16_MIN).astype(I16)
        return carry

    lax.fori_loop(0, ngk, idx_body, 0)

    def key_tiles(gi, ref=key_s):
        off = pl.multiple_of(gi * KW, KW)
        key = ref[:, pl.ds(off, KW)]
        return [key[:, u * LANES:(u + 1) * LANES] for u in range(kg)]

    def lane_total(acc):
        return jnp.broadcast_to(jnp.sum(acc.astype(F32), axis=1, keepdims=True), (tq, LANES))

    one16 = jnp.ones((tq, LANES), I16)
    zero16 = jnp.zeros((tq, LANES), I16)

    def search16(ref, kneed):
        for ng in range(1, key_s.shape[1] // KW + 1):
            @pl.when(ngk == ng)
            def _(ng=ng):
                def bit_body(it, cur):
                    cand = cur + lax.shift_left(jnp.int32(1), jnp.asarray(15 - it, dtype=I32))
                    c16 = cand.astype(I16)
                    accs = [zero16] * kg
                    for gi in range(ng):
                        for u in range(kg):
                            kt = ref[:, gi * KW + u * LANES:gi * KW + (u + 1) * LANES]
                            accs[u] = accs[u] + jnp.where(kt >= c16, one16, zero16)
                    acc = accs[0]
                    for a in accs[1:]:
                        acc = acc + a
                    return jnp.where(lane_total(acc) >= kneed, cand, cur)

                tau_s[...] = lax.fori_loop(0, 16, bit_body, jnp.full((tq, LANES), I16_MIN, I32))

        return tau_s[...]

    tau_hi = search16(hi_s, float(topk))
    th16 = tau_hi.astype(I16)
    lo_pad = jnp.full((tq, LANES), I16_MIN, I16)

    def mid_body(gi, acc):
        off = pl.multiple_of(gi * KW, KW)
        los = key_tiles(gi, lo_s)
        for u, ht in enumerate(key_tiles(gi, hi_s)):
            acc = acc + jnp.where(ht > th16, one16, zero16)
            lo_s[:, pl.ds(off + u * LANES, LANES)] = jnp.where(ht == th16, los[u], lo_pad)
        return acc

    need_lo = float(topk) - lane_total(lax.fori_loop(0, ngk, mid_body, zero16))
    tau_lo = search16(lo_s, need_lo)
    tau = lax.shift_left(tau_hi, 16) + (tau_lo - I16_MIN)

    def cnt2_body(gi, carry):
        cge, cgt = carry
        for kt in key_tiles(gi):
            cge = cge + jnp.where(kt >= tau, 1.0, 0.0)
            cgt = cgt + jnp.where(kt > tau, 1.0, 0.0)
        return cge, cgt

    zero = jnp.zeros((tq, LANES), F32)
    cge, cgt = lax.fori_loop(0, ngk, cnt2_body, (zero, zero))
    need = float(topk) - lane_total(cgt)
    ties = jnp.max(lane_total(cge)) > float(topk)

    @pl.when(jnp.logical_not(ties))
    def _():
        def body(gi, carry):
            off = pl.multiple_of(gi * KW, KW)
            for u, kt in enumerate(key_tiles(gi)):
                take = jnp.where(kt >= tau, jnp.where(kt > KEY_NEG_INF, 0.0, NEG), NEG)
                madd_s[:, pl.ds(off + u * LANES, LANES)] = take
            return carry

        lax.fori_loop(0, ngk, body, 0)

    @pl.when(ties)
    def _():
        tri = tri_ref[...]
        ones = ones_ref[...]

        def body(c, run):
            off = pl.multiple_of(c * LANES, LANES)
            key = key_s[:, pl.ds(off, LANES)]
            eq = jnp.where(key == tau, 1.0, 0.0)
            eqb = eq.astype(BF16)
            pre = _dot(eqb, tri) + run
            take = jnp.where(key > tau, 1.0, jnp.where(pre <= need, eq, 0.0))
            take = jnp.where(key > KEY_NEG_INF, take, 0.0)
            madd_s[:, pl.ds(off, LANES)] = jnp.where(take > 0.5, 0.0, NEG)
            return run + _dot(eqb, ones)

        lax.fori_loop(0, ngk * kg, body, jnp.zeros((tq, LANES), F32))

    outs = []
    for n in range(N_KV):
        qs = qs_s[n]

        m_s[...] = jnp.full((R, LANES), NEG, F32)
        l_s[...] = jnp.zeros((R, LANES), F32)
        acc_s[...] = jnp.zeros((R, HD), F32)

        def pass_a(gi, carry, n=n):
            off = pl.multiple_of(gi * AW, AW)
            kc = kk_s[n, :, pl.ds(off, AW)]
            dds = [jnp.clip(j - (gi * ag + u), 0, 2) for u in range(ag)]
            s_all = _dot(qs_s[n], kc)
            for g in range(G_A):
                rows = pl.ds(g * tq, tq)
                s = s_all[g * tq:(g + 1) * tq, :]
                mx = m_s[rows, :]
                for u in range(ag):
                    cols = pl.ds(off + u * LANES, LANES)
                    su = s[:, u * LANES:(u + 1) * LANES] + bias_ref[dds[u], n, rows, :] + madd_s[:, cols]
                    s_s[rows, cols] = su
                    mx = jnp.maximum(mx, su)
                m_s[rows, :] = mx
            return carry

        lax.fori_loop(0, nga, pass_a, 0)
        mb = jnp.broadcast_to(jnp.max(m_s[...], axis=1, keepdims=True), (R, LANES))
        m_s[...] = mb

        def pass_b(gi, carry, n=n):
            off = pl.multiple_of(gi * AW, AW)
            slabs = []
            for g in range(G_A):
                rows = pl.ds(g * tq, tq)
                mrow = m_s[rows, :]
                lacc = l_s[rows, :]
                ps = []
                for u in range(ag):
                    p = jnp.exp(s_s[rows, pl.ds(off + u * LANES, LANES)] - mrow)
                    lacc = lacc + p
                    ps.append(p.astype(BF16))
                l_s[rows, :] = lacc
                slabs.append(ps[0] if ag == 1 else jnp.concatenate(ps, axis=1))
            acc_s[...] += _dot(jnp.concatenate(slabs, axis=0), vv_s[n, pl.ds(off, AW), :])
            return carry

        lax.fori_loop(0, nga, pass_b, 0)
        o = acc_s[...] / jnp.sum(l_s[...], axis=1, keepdims=True)
        for g in range(G_A):
            outs.append(o[g * tq:(g + 1) * tq, :])
    o_ref[0] = jnp.concatenate(outs, axis=-1)


def _att_prompt(z3, kn3, kin3, P, bias_p, topk, tq):
    B, T, _ = z3.shape
    nblk = T // LANES
    kg = 4 if nblk % 4 == 0 else (2 if nblk % 2 == 0 else 1)
    ag = kg
    cst = lambda shp: pl.BlockSpec(shp, lambda b, j: tuple(0 for _ in shp))
    return pl.pallas_call(
        functools.partial(_attp_kernel, tq=tq, topk=topk, kg=kg, ag=ag),
        grid=(B, T // tq),
        in_specs=[pl.BlockSpec((1, tq, C_A), lambda b, j: (b, j, ZC_Q)),
                  pl.BlockSpec((1, tq, H_I * D_IDX), lambda b, j: (b, j, ZC_QI)),
                  pl.BlockSpec((1, tq, LANES), lambda b, j: (b, j, ZC_KW)),
                  pl.BlockSpec((1, T, N_KV * HD), lambda b, j: (b, 0, 0)),
                  pl.BlockSpec((1, T, N_KV * HD), lambda b, j: (b, 0, ZC_V)),
                  pl.BlockSpec((1, T, D_IDX), lambda b, j: (b, 0, 0)),
                  cst((1, C_A)), cst((3, N_KV, G_A * tq, LANES)), cst((C_A, C_A)), cst((LANES, LANES)),
                  cst((LANES, LANES))],
        out_specs=pl.BlockSpec((1, tq, C_A), lambda b, j: (b, j, 0)),
        out_shape=jax.ShapeDtypeStruct((B, T, C_A), F32),
        scratch_shapes=[pltpu.VMEM((N_KV, HD, T), BF16),
                        pltpu.VMEM((N_KV, T, HD), BF16),
                        pltpu.VMEM((D_IDX, T), BF16),
                        pltpu.VMEM((N_KV, G_A * tq, HD), BF16),
                        pltpu.VMEM((H_I, tq, D_IDX), BF16),
                        pltpu.VMEM((H_I, tq, kg * LANES), F32),
                        pltpu.VMEM((tq, T), I32),
                        pltpu.VMEM((tq, T), I16),
                        pltpu.VMEM((tq, T), I16),
                        pltpu.VMEM((tq, LANES), I32),
                        pltpu.VMEM((tq, T), F32),
                        pltpu.VMEM((G_A * tq, T), F32),
                        pltpu.VMEM((G_A * tq, LANES), F32),
                        pltpu.VMEM((G_A * tq, LANES), F32),
                        pltpu.VMEM((G_A * tq, HD), F32)],
        compiler_params=_cparams(("parallel", "arbitrary")),
        name="dsa_prompt",
    )(z3, z3, z3, kn3, z3, kin3, P["q_norm8"], bias_p.reshape(3, N_KV, G_A * tq, LANES), P["gseg8"], P["tri"],
      P["ones"])


def _pad_transpose(x):
    r, c = x.shape
    if c < LANES:
        x = jnp.concatenate([x, jnp.zeros((r, LANES - c), F32)], axis=1)
    x = jnp.concatenate([x, jnp.zeros((LANES - r, LANES), F32)], axis=0)
    return jnp.transpose(x)


def _atts_kernel(pt_ref, ckT_ref, cvT_ref, ckiT_ref, q_ref, qi_ref, kw_ref, kn_ref, vn_ref, kin_ref, qg_ref,
                 bias_ref, gseg_ref, tri_ref, ones_ref,
                 o_ref,
                 kT_s, vT_s, kiT_s, madd_s, sem, *, layer, npages, nbatch, Ts, topk):
    b = pl.program_id(0)
    slot = b % 2
    P = LANES
    L = (npages + 1) * P

    def page_copies(seq, sl, p):
        pg = pt_ref[seq, p]
        col = pl.multiple_of(p * P, P)
        return (pltpu.make_async_copy(ckT_ref.at[layer, pg], kT_s.at[sl, :, :, pl.ds(col, P)], sem.at[sl, 0]),
                pltpu.make_async_copy(cvT_ref.at[layer, pg], vT_s.at[sl, :, :, pl.ds(col, P)], sem.at[sl, 1]),
                pltpu.make_async_copy(ckiT_ref.at[layer, pg], kiT_s.at[sl, :, pl.ds(col, P)], sem.at[sl, 2]))

    def start_pages(seq, sl):
        def body(p, carry):
            for cp in page_copies(seq, sl, p):
                cp.start()
            return carry
        lax.fori_loop(0, npages, body, 0)

    def wait_pages(seq, sl):
        def body(p, carry):
            for cp in page_copies(seq, sl, p):
                cp.wait()
            return carry
        lax.fori_loop(0, npages, body, 0)

    @pl.when(b == 0)
    def _():
        start_pages(0, 0)

    @pl.when(b + 1 < nbatch)
    def _():
        start_pages(b + 1, 1 - slot)

    knT = _pad_transpose(kn_ref[0])
    vnT = _pad_transpose(vn_ref[0])
    for n in range(N_KV):
        kT_s[slot, n, :, npages * P:L] = knT[n * HD:(n + 1) * HD, :]
        vT_s[slot, n, :, npages * P:L] = vnT[n * HD:(n + 1) * HD, :]
    kiT_s[slot, :, npages * P:L] = _pad_transpose(kin_ref[0])[0:D_IDX, :]

    qn = _q_prep(q_ref[0], qg_ref[...], gseg_ref[...])
    qi = (qi_ref[0] * (D_IDX ** -0.5)).astype(BF16)
    qis = jnp.concatenate([qi[:, h * D_IDX:(h + 1) * D_IDX] for h in range(H_I)], axis=0)
    kw = kw_ref[0]

    wait_pages(b, slot)

    d = _dot(qis, kiT_s[slot].astype(BF16))
    s = jnp.zeros((Ts, L), F32)
    for h in range(H_I):
        s = s + jnp.maximum(d[h * Ts:(h + 1) * Ts, :], 0.0) * (kw[:, D_IDX + h:D_IDX + h + 1] * (H_I ** -0.5))
    col = lax.broadcasted_iota(I32, (Ts, L), 1)
    row = lax.broadcasted_iota(I32, (Ts, L), 0) + npages * P
    key = _sort_key(jnp.where(col <= row, s, -jnp.inf))

    def bit_body(it, cur):
        cand = cur + lax.shift_left(jnp.int32(1), jnp.asarray(31 - it, dtype=I32))
        cnt = jnp.sum(jnp.where(key >= cand, 1.0, 0.0), axis=1, keepdims=True)
        return jnp.where(cnt >= float(topk), cand, cur)

    tau = lax.fori_loop(0, 32, bit_body, jnp.full((Ts, 1), INT_MIN, I32))
    need = float(topk) - jnp.sum(jnp.where(key > tau, 1.0, 0.0), axis=1, keepdims=True)
    cge = jnp.sum(jnp.where(key >= tau, 1.0, 0.0), axis=1, keepdims=True)
    ties = jnp.max(cge) > float(topk)

    @pl.when(jnp.logical_not(ties))
    def _():
        madd_s[...] = jnp.where(key >= tau, jnp.where(key > KEY_NEG_INF, 0.0, NEG), NEG)

    @pl.when(ties)
    def _():
        tri = tri_ref[...]
        ones = ones_ref[...]
        eq = jnp.where(key == tau, 1.0, 0.0)
        run = jnp.zeros((Ts, LANES), F32)
        for c in range(L // LANES):
            sl = slice(c * LANES, (c + 1) * LANES)
            eqc = eq[:, sl]
            eqb = eqc.astype(BF16)
            pre = _dot(eqb, tri) + run
            take = jnp.where(key[:, sl] > tau, 1.0, jnp.where(pre <= need, eqc, 0.0))
            take = jnp.where(key[:, sl] > KEY_NEG_INF, take, 0.0)
            madd_s[:, sl] = jnp.where(take > 0.5, 0.0, NEG)
            run = run + _dot(eqb, ones)

    madd = madd_s[...]
    madd4 = jnp.concatenate([madd] * G_A, axis=0)
    outs = []
    for n in range(N_KV):
        qs = jnp.concatenate([qn[:, (n * G_A + g) * HD:(n * G_A + g + 1) * HD] for g in range(G_A)], axis=0)
        sc = _dot(qs, kT_s[slot, n].astype(BF16)) + bias_ref[n] + madd4
        m = jnp.max(sc, axis=1, keepdims=True)
        pe = jnp.exp(sc - m)
        l = jnp.sum(pe, axis=1, keepdims=True)
        o = _dot_nt(pe.astype(BF16), vT_s[slot, n].astype(BF16)) / l
        for g in range(G_A):
            outs.append(o[g * Ts:(g + 1) * Ts, :])
    o_ref[0] = jnp.concatenate(outs, axis=-1)


def _att_sample(layer, page_table, ckT, cvT, ckiT, z3, kn3, kin3, P, bias_s, topk):
    B, Ts, _ = z3.shape
    npages = page_table.shape[1]
    L = (npages + 1) * LANES
    hbm = pl.BlockSpec(memory_space=pl.ANY)
    cst = lambda shp: pl.BlockSpec(shp, lambda b, pt: tuple(0 for _ in shp))
    grid_spec = pltpu.PrefetchScalarGridSpec(
        num_scalar_prefetch=1,
        grid=(B,),
        in_specs=[hbm, hbm, hbm,
                  pl.BlockSpec((1, Ts, C_A), lambda b, pt: (b, 0, ZC_Q)),
                  pl.BlockSpec((1, Ts, H_I * D_IDX), lambda b, pt: (b, 0, ZC_QI)),
                  pl.BlockSpec((1, Ts, LANES), lambda b, pt: (b, 0, ZC_KW)),
                  pl.BlockSpec((1, Ts, N_KV * HD), lambda b, pt: (b, 0, 0)),
                  pl.BlockSpec((1, Ts, N_KV * HD), lambda b, pt: (b, 0, ZC_V)),
                  pl.BlockSpec((1, Ts, D_IDX), lambda b, pt: (b, 0, 0)),
                  cst((1, C_A)), cst((N_KV, G_A * Ts, L)), cst((C_A, C_A)), cst((LANES, LANES)),
                  cst((LANES, LANES))],
        out_specs=pl.BlockSpec((1, Ts, C_A), lambda b, pt: (b, 0, 0)),
        scratch_shapes=[pltpu.VMEM((2, N_KV, HD, L), F32),
                        pltpu.VMEM((2, N_KV, HD, L), F32),
                        pltpu.VMEM((2, D_IDX, L), F32),
                        pltpu.VMEM((Ts, L), F32),
                        pltpu.SemaphoreType.DMA((2, 3))])
    return pl.pallas_call(
        functools.partial(_atts_kernel, layer=layer, npages=npages, nbatch=B, Ts=Ts, topk=topk),
        grid_spec=grid_spec,
        out_shape=jax.ShapeDtypeStruct((B, Ts, C_A), F32),
        compiler_params=_cparams(("arbitrary",)),
        name="dsa_sample",
    )(page_table, ckT, cvT, ckiT, z3, z3, z3, kn3, z3, kin3, P["q_norm8"], bias_s, P["gseg8"], P["tri"], P["ones"])


def _out_proj_kernel(x_ref, orw_ref, ocv_ref, oat_ref, w_ref, y_ref):
    acc = _dot(orw_ref[...].astype(BF16), w_ref[0:C_R, :])
    acc = acc + _dot(ocv_ref[...].astype(BF16), w_ref[C_R:C_R + C_CONV, :])
    acc = acc + _dot(oat_ref[...].astype(BF16), w_ref[C_R + C_CONV:, :])
    y_ref[...] = x_ref[...] + acc


def _out_proj(x2, orw, ocv, oat, w_bf, tm):
    M, D = x2.shape
    row = lambda n: pl.BlockSpec((tm, n), lambda i: (i, 0))
    return pl.pallas_call(
        _out_proj_kernel,
        grid=(M // tm,),
        in_specs=[row(D), row(C_R), row(C_CONV), row(C_A), pl.BlockSpec((D, D), lambda i: (0, 0))],
        out_specs=row(D),
        out_shape=jax.ShapeDtypeStruct((M, D), F32),
        compiler_params=_cparams(("parallel",)),
        name="out_proj",
    )(x2, orw, ocv, oat, w_bf)


def _gelu(x):
    return 0.5 * x * (1.0 + lax.erf(x * (2.0 ** -0.5)))


def _ffn_kernel(x_ref, p_ref, ng_ref, upg_ref, upv_ref, cw_ref, cb_ref, dn_ref, h_ref, h2_ref, pg_ref, pgate_ref,
                pproj_ref,
                xo_ref, tail_ref,
                xn_s, acc_s, hist_s, *, tm, nF, seq_tiles, seq_len):
    i = pl.program_id(0)
    f = pl.program_id(1)

    @pl.when(f == 0)
    def _():
        xn_s[...] = _rms(x_ref[...], ng_ref[...]).astype(BF16)
        acc_s[...] = jnp.zeros(acc_s.shape, F32)

    xn = xn_s[...]
    g = _dot(xn, upg_ref[...])
    val = _dot(xn, upv_ref[...])
    row = lax.broadcasted_iota(I32, g.shape, 0)
    if seq_tiles is not None:
        first = (i % seq_tiles) == 0
        hs = hist_s[f]
        h0 = h_ref[0]
        hm2 = jnp.where(first, h0[0:1, :], hs[SUBLANES - 2:SUBLANES - 1, :])
        hm1 = jnp.where(first, h0[1:2, :], hs[SUBLANES - 1:SUBLANES, :])
        g1 = jnp.where(row == 0, hm1, pltpu.roll(g, 1, 0))
        g2 = jnp.where(row == 0, hm2, jnp.where(row == 1, hm1, pltpu.roll(g, 2, 0)))
        slab = g[tm - SUBLANES:tm, :]
        hist_s[f] = slab
        tail_ref[0] = slab
    else:
        t = row % seq_len
        g1 = jnp.where(t == 0, 0.0, pltpu.roll(g, 1, 0)) + h_ref[...]
        g2 = jnp.where(t < 2, 0.0, pltpu.roll(g, 2, 0)) + h2_ref[...]
        tail_ref[...] = g
    cw = cw_ref[...]
    gate = cw[0:1, :] * g2 + cw[1:2, :] * g1 + cw[2:3, :] * g + cb_ref[...]
    act = (_gelu(gate) * val).astype(BF16)
    acc_s[...] += _dot(act, dn_ref[...])

    @pl.when(f == nF - 1)
    def _():
        x2 = x_ref[...] + acc_s[...]
        xn2 = _rms(x2, pg_ref[...]).astype(BF16)
        gate2 = _sigmoid(_dot(xn2, pgate_ref[...]))
        xo_ref[...] = x2 + _dot(p_ref[...].astype(BF16), pproj_ref[...]) * gate2


def _ffn(x2, p2, ffn0, P, tm, B, T):
    M, D = x2.shape
    tf = 256
    nF = D_FF // tf
    DP = p2.shape[1]
    whole = tm % T == 0
    if whole:
        seq_tiles, seq_len = None, T
        h1 = jnp.pad(ffn0[:, 1:2, :], ((0, 0), (0, T - 1), (0, 0))).reshape(M, D_FF)
        h2 = jnp.pad(ffn0, ((0, 0), (0, T - 2), (0, 0))).reshape(M, D_FF)
        h_specs = [pl.BlockSpec((tm, tf), lambda i, f: (i, f)), pl.BlockSpec((tm, tf), lambda i, f: (i, f))]
        tail_spec = pl.BlockSpec((tm, tf), lambda i, f: (i, f))
        tail_shape = jax.ShapeDtypeStruct((M, D_FF), F32)
    else:
        seq_tiles, seq_len = T // tm, T
        h1, h2 = ffn0, ffn0
        h_specs = [pl.BlockSpec((1, 2, tf), lambda i, f: (i // seq_tiles, 0, f)),
                   pl.BlockSpec((1, 2, tf), lambda i, f: (i // seq_tiles, 0, f))]
        tail_spec = pl.BlockSpec((1, SUBLANES, tf), lambda i, f: (i, 0, f))
        tail_shape = jax.ShapeDtypeStruct((M // tm, SUBLANES, D_FF), F32)
    cst = lambda shp: pl.BlockSpec(shp, lambda i, f: tuple(0 for _ in shp))
    xo, tail = pl.pallas_call(
        functools.partial(_ffn_kernel, tm=tm, nF=nF, seq_tiles=seq_tiles, seq_len=seq_len),
        grid=(M // tm, nF),
        in_specs=[pl.BlockSpec((tm, D), lambda i, f: (i, 0)),
                  pl.BlockSpec((tm, DP), lambda i, f: (i, 0)),
                  cst((1, D)),
                  pl.BlockSpec((D, tf), lambda i, f: (0, f)),
                  pl.BlockSpec((D, tf), lambda i, f: (0, nF + f)),
                  pl.BlockSpec((3, tf), lambda i, f: (0, f)),
                  pl.BlockSpec((1, tf), lambda i, f: (0, f)),
                  pl.BlockSpec((tf, D), lambda i, f: (f, 0))] + h_specs +
                 [cst((1, D)), cst((D, D)), cst((DP, D))],
        out_specs=[pl.BlockSpec((tm, D), lambda i, f: (i, 0)), tail_spec],
        out_shape=[jax.ShapeDtypeStruct((M, D), F32), tail_shape],
        scratch_shapes=[pltpu.VMEM((tm, D), BF16), pltpu.VMEM((tm, D), F32), pltpu.VMEM((nF, SUBLANES, tf), F32)],
        compiler_params=_cparams(("parallel", "arbitrary")),
        name="conv_ffn_ple",
    )(x2, p2, P["norm_ffn"], P["ffn_up"], P["ffn_up"], P["ffn_conv_w"], P["ffn_conv_b"], P["ffn_down"], h1, h2,
      P["ple_norm"], P["ple_gate"], P["ple_proj"])
    if whole:
        ffn_T = tail.reshape(B, T, D_FF)[:, T - 2:, :]
    else:
        ffn_T = tail.reshape(B, seq_tiles, SUBLANES, D_FF)[:, seq_tiles - 1, SUBLANES - 2:, :]
    return xo, ffn_T


def _seg_ones(n, seg):
    i = np.arange(n)
    return jnp.asarray((i[:, None] // seg) == (i[None, :] // seg), dtype=BF16)


def _hl(w):
    hi = w.astype(BF16)
    return hi, (w - hi.astype(F32)).astype(BF16)


def _layer_params(i, W):
    P = {}
    w_in = W["w_in"][i]
    D = w_in.shape[0]
    P["w_in"] = jnp.concatenate([w_in[:, :N_RWKV_COLS], jnp.zeros((D, 1024 - N_RWKV_COLS), F32),
                                 w_in[:, N_RWKV_COLS:], jnp.zeros((D, NZ - 128 - N_IN), F32)], axis=1).astype(BF16)
    row = lambda v: v.reshape(1, -1)
    P["norm_mix"] = row(W["norm_mix"][i])
    P["mu"] = row(W["mu_shift"][i])
    P["w0"] = row(W["w0"][i])
    P["a0"] = row(W["a0"][i])
    z = lambda r: jnp.zeros((r, C_R), F32)
    P["wl_hi"], P["wl_lo"] = _hl(jnp.concatenate([W["w_lora"][i], z(96)], axis=0))
    P["al_hi"], P["al_lo"] = _hl(jnp.concatenate([z(32), W["a_lora"][i], z(64)], axis=0))
    P["gl_hi"], P["gl_lo"] = _hl(jnp.concatenate([z(64), W["g_lora"][i]], axis=0))
    P["k_k"] = row(W["k_k"][i])
    P["k_a"] = row(W["k_a"][i])
    P["r_k"] = row(W["r_k"][i])
    P["lnx_g"] = row(W["lnx_g"][i])
    P["lnx_b"] = row(W["lnx_b"][i])
    P["conv_w"] = W["conv_w"][i]
    P["conv_b"] = row(W["conv_b"][i])
    P["conv_ln_g"] = row(W["conv_ln_g"][i])
    P["conv_ln_b"] = row(W["conv_ln_b"][i])
    P["q_norm8"] = row(jnp.tile(W["q_norm"][i], H_A))
    P["k_norm2"] = row(jnp.tile(W["k_norm"][i], N_KV))
    P["kidx_norm"] = row(W["kidx_norm"][i])
    P["w_out"] = W["w_out"][i].astype(BF16)
    P["norm_ffn"] = row(W["norm_ffn"][i])
    P["ffn_up"] = W["ffn_up"][i].astype(BF16)
    P["ffn_conv_w"] = W["ffn_conv_w"][i]
    P["ffn_conv_b"] = row(W["ffn_conv_b"][i])
    P["ffn_down"] = W["ffn_down"][i].astype(BF16)
    P["ple_norm"] = row(W["ple_norm"][i])
    P["ple_gate"] = W["ple_gate"][i].astype(BF16)
    P["ple_proj"] = W["ple_proj"][i].astype(BF16)
    P["gseg4"] = _seg_ones(C_R, HD)
    P["gseg2"] = _seg_ones(N_KV * HD, HD)
    P["gseg8"] = _seg_ones(C_A, HD)
    P["tri"] = jnp.asarray(np.arange(LANES)[:, None] <= np.arange(LANES)[None, :], dtype=BF16)
    P["ones"] = jnp.ones((LANES, LANES), BF16)
    return P


def _pick_tile(n, pref):
    t = min(n, pref)
    while n % t:
        t //= 2
    return t


def _layer(layer, x, p_i, st, P, att_fn, bb):
    B, T, D = x.shape
    M = B * T
    shift0, wkv0, conv0, ffn0 = st
    x2 = x.reshape(M, D)
    tm = _pick_tile(M, 512)
    z2 = _in_proj(x2, P["norm_mix"], P["w_in"], tm)
    z3 = z2.reshape(B, T, NZ)
    Tc = _pick_tile(T, 256)
    o_rw, wkv_T = _rwkv(z3, shift0.reshape(B, 1, N_RWKV_COLS), wkv0, P, bb, Tc)
    conv0p = jnp.pad(conv0, ((0, 0), (HIST - (CONV_W - 1), 0), (0, 0)))
    o_cv, conv_Tp = _conv(z3, conv0p, P, _pick_tile(T, 512))
    kn2, kin2 = _att_prep(z2, P, tm)
    kn3 = kn2.reshape(B, T, N_KV * HD)
    kin3 = kin2.reshape(B, T, D_IDX)
    o_at = att_fn(layer, z3, kn3, kin3, P)
    xa = _out_proj(x2, o_rw.reshape(M, C_R), o_cv.reshape(M, C_CONV), o_at.reshape(M, C_A), P["w_out"], tm)
    tmf = M if M <= 1024 else _pick_tile(T, 1024)
    xo, ffn_T = _ffn(xa, p_i.reshape(M, -1), ffn0, P, tmf, B, T)
    k_new = kn3.reshape(B, T, N_KV, HD)
    v_new = z3[:, :, ZC_V * LANES:(ZC_V + 1) * LANES].reshape(B, T, N_KV, HD)
    shift_T = z3[:, T - 1, :N_RWKV_COLS]
    conv_T = conv_Tp[:, HIST - (CONV_W - 1):, :]
    return xo.reshape(B, T, D), (k_new, v_new, kin3, wkv_T, shift_T, conv_T, ffn_T)


def kernel(x_prompt, x_sample, cache_k, cache_v, cache_kidx, state_wkv, state_shift, state_conv, state_ffn, page_table, p_prompt, p_sample, norm_mix, w_in, w_out, mu_shift, w0, w_lora, a0, a_lora, g_lora, k_k, k_a, r_k, lnx_g, lnx_b, conv_w, conv_b, conv_ln_g, conv_ln_b, q_norm, k_norm, kidx_norm, rel_bias, norm_ffn, ffn_up, ffn_conv_w, ffn_conv_b, ffn_down, ple_norm, ple_proj, ple_gate):
    W = dict(norm_mix=norm_mix, w_in=w_in, w_out=w_out, mu_shift=mu_shift, w0=w0, w_lora=w_lora, a0=a0,
             a_lora=a_lora, g_lora=g_lora, k_k=k_k, k_a=k_a, r_k=r_k, lnx_g=lnx_g, lnx_b=lnx_b,
             conv_w=conv_w, conv_b=conv_b, conv_ln_g=conv_ln_g, conv_ln_b=conv_ln_b, q_norm=q_norm,
             k_norm=k_norm, kidx_norm=kidx_norm, norm_ffn=norm_ffn, ffn_up=ffn_up,
             ffn_conv_w=ffn_conv_w, ffn_conv_b=ffn_conv_b, ffn_down=ffn_down, ple_norm=ple_norm,
             ple_proj=ple_proj, ple_gate=ple_gate)
    depth = w_in.shape[0]
    params = [_layer_params(i, W) for i in range(depth)]

    B, T, D = x_prompt.shape
    tq = LANES
    topk_p = min(TOPK_MAX, T // 4)
    t_ = np.arange(tq)[:, None]
    s_ = np.arange(LANES)[None, :]
    tiles = np.stack([_rel_bucket_np(np.maximum(t_ - s_, 0)), _rel_bucket_np(tq + t_ - s_),
                      np.full((tq, LANES), N_BUCKETS - 1, np.int32)])
    idx_p = np.broadcast_to(tiles[:, None], (3, H_A, tq, LANES)).reshape(3 * H_A, tq, LANES)
    bias_p = _bias_expand(rel_bias, jnp.asarray(idx_p)).reshape(3, H_A, tq, LANES)

    def att_p(layer, z3, kn3, kin3, P):
        return _att_prompt(z3, kn3, kin3, P, bias_p, topk_p, tq)

    st0 = (jnp.zeros((B, N_RWKV_COLS), F32), jnp.zeros((B, H_R, HD, HD), F32),
           jnp.zeros((B, CONV_W - 1, C_CONV), F32), jnp.zeros((B, 2, D_FF), F32))
    x = x_prompt
    new_p = []
    for i in range(depth):
        x, new = _layer(i, x, p_prompt[i], st0, params[i], att_p, B)
        new_p.append(new)
    y_prompt = x

    Bs, Ts, _ = x_sample.shape
    npages = page_table.shape[1]
    psz = cache_k.shape[2]
    past_len = npages * psz
    topk_s = min(TOPK_MAX, (past_len + Ts) // 4)
    Ls = past_len + psz
    qpos = past_len + np.arange(Ts)[:, None]
    bk = _rel_bucket_np(np.maximum(qpos - np.arange(Ls)[None, :], 0))
    idx_s = np.broadcast_to(bk[None], (H_A, Ts, Ls))
    bias_s = _bias_expand(rel_bias, jnp.asarray(idx_s)).reshape(N_KV, G_A * Ts, Ls)
    ckT = jnp.transpose(cache_k, (0, 1, 3, 4, 2))
    cvT = jnp.transpose(cache_v, (0, 1, 3, 4, 2))
    ckiT = jnp.transpose(cache_kidx, (0, 1, 3, 2))

    def att_s(layer, z3, kn3, kin3, P):
        return _att_sample(layer, page_table, ckT, cvT, ckiT, z3, kn3, kin3, P, bias_s, topk_s)

    x = x_sample
    new_s = []
    for i in range(depth):
        st = (state_shift[i], state_wkv[i], state_conv[i], state_ffn[i])
        x, new = _layer(i, x, p_sample[i], st, params[i], att_s, 4)
        new_s.append(new)
    y_sample = x

    stack = lambda lst: tuple(jnp.stack(t) for t in zip(*lst))
    return (y_prompt, y_sample) + stack(new_p) + stack(new_s)
```

```python
import functools
import math

import numpy as np
import jax
import jax.numpy as jnp
from jax import lax
from jax.experimental import pallas as pl
from jax.experimental.pallas import tpu as pltpu

F32 = jnp.float32
BF16 = jnp.bfloat16
I32 = jnp.int32
I16 = jnp.int16

HD = 64
C_R = 256
H_R = C_R // HD
C_CONV = 256
CONV_W = 31
C_A = 512
H_A = C_A // HD
N_KV = 2
G_A = H_A // N_KV
H_I = 4
D_IDX = 64
TOPK_MAX = 256
N_BUCKETS = 32
MAX_DISTANCE = 128
D_FF = 2816
NORM_EPS = 1e-6
RWKV_LN_EPS = 64e-5
CONV_LN_EPS = 1e-5
N_RWKV_COLS = 896
N_IN = 2500

LANES = 128
SUBLANES = 8
VMEM_LIMIT = 56 * 1024 * 1024

NZ = 2688
ZC_CONV = 1024 // 512
ZC_Q = 1536 // 512
ZC_K = 2048 // 128
ZC_V = 2176 // 128
ZC_QI = 2304 // 256
ZC_KW = 2560 // 128

HIST = 32
ATT_ROWS = 512
ATT_TQ = 256
NEG = -1e30
INT_MIN = -2 ** 31
I16_MIN = -2 ** 15
KEY_NEG_INF = int(np.array(0xFF800000 ^ 0x7FFFFFFF, dtype=np.uint32).view(np.int32))


def _cparams(sem):
    return pltpu.CompilerParams(dimension_semantics=sem, vmem_limit_bytes=VMEM_LIMIT)


def _split2(x):
    hi = x.astype(BF16)
    lo = (x - hi.astype(F32)).astype(BF16)
    return hi, lo


def _dot(a, b):
    return jnp.dot(a, b, preferred_element_type=F32)


def _dot_nt(a, b):
    return lax.dot_general(a, b, (((1,), (1,)), ((), ())), preferred_element_type=F32)


def _dot_exactw(x, w_bf):
    hi, lo = _split2(x)
    return _dot(hi, w_bf) + _dot(lo, w_bf)


def _dot_hl(x, w_hi, w_lo):
    hi, lo = _split2(x)
    return _dot(hi, w_hi) + (_dot(lo, w_hi) + _dot(hi, w_lo))


def _sigmoid(x):
    return 1.0 / (1.0 + jnp.exp(-x))


def _rms(x, g):
    ms = jnp.mean(x * x, axis=-1, keepdims=True)
    return x * lax.rsqrt(ms + NORM_EPS) * g


def _in_proj_kernel(x_ref, g_ref, w_ref, z_ref, xn_s):
    @pl.when(pl.program_id(1) == 0)
    def _():
        xn_s[...] = _rms(x_ref[...], g_ref[...]).astype(BF16)

    z_ref[...] = _dot(xn_s[...], w_ref[...])


def _in_proj(x2, g, w_bf, tm):
    M, D = x2.shape
    tn = NZ
    return pl.pallas_call(
        _in_proj_kernel,
        grid=(M // tm, NZ // tn),
        in_specs=[pl.BlockSpec((tm, D), lambda i, j: (i, 0)),
                  pl.BlockSpec((1, D), lambda i, j: (0, 0)),
                  pl.BlockSpec((D, tn), lambda i, j: (0, j))],
        out_specs=pl.BlockSpec((tm, tn), lambda i, j: (i, j)),
        out_shape=jax.ShapeDtypeStruct((M, NZ), F32),
        scratch_shapes=[pltpu.VMEM((tm, D), BF16)],
        compiler_params=_cparams(("parallel", "arbitrary")),
        name="in_proj",
    )(x2, g, w_bf)


def _rwkv_kernel(z_ref, sh0_ref, wkv0_ref, mu_ref, w0_ref, wl_hi, wl_lo, a0_ref, al_hi, al_lo, gl_hi, gl_lo,
                 kk_ref, ka_ref, rk_ref, lg_ref, lb_ref, gseg_ref,
                 o_ref, st_ref,
                 S_s, carry_s, w_s, k_s, v_s, a_s, b_s, r_s, y_s, bon_s, g_s, *, bb, Tc):
    c = pl.program_id(1)

    @pl.when(c == 0)
    def _():
        S_s[...] = wkv0_ref[...]
        carry_s[...] = sh0_ref[...]

    gseg = gseg_ref[...]
    mu = mu_ref[...]
    rid = lax.broadcasted_iota(I32, (Tc, N_RWKV_COLS), 0)
    for b in range(bb):
        z = z_ref[b]
        prev = jnp.where(rid == 0, carry_s[b], pltpu.roll(z, 1, 0))
        carry_s[b] = z[Tc - 1:Tc, :]
        zs = z + (prev - z) * mu
        r = zs[:, 0:C_R]
        k = zs[:, C_R:2 * C_R]
        v = zs[:, 2 * C_R:3 * C_R]
        t6 = zs[:, 3 * C_R:]
        lw = w0_ref[...] + _dot_hl(jnp.tanh(t6), wl_hi[...], wl_lo[...])
        nl = -lw
        softplus = jnp.maximum(nl, 0.0) + jnp.log(1.0 + jnp.exp(-jnp.abs(nl)))
        decay = jnp.exp(-jnp.exp(-softplus - 0.5))
        a = _sigmoid(a0_ref[...] + _dot_hl(t6, al_hi[...], al_lo[...]))
        g_s[b] = _dot_hl(_sigmoid(t6), gl_hi[...], gl_lo[...])
        kk = k * kk_ref[...]
        nrm = jnp.sqrt(_dot_exactw(kk * kk, gseg))
        kk = kk / jnp.maximum(nrm, 1e-12)
        k2 = k * (1.0 + (a - 1.0) * ka_ref[...])
        bon_s[b] = _dot_exactw(r * k2 * rk_ref[...], gseg) * v
        na = -kk
        kb = kk * a
        for h in range(H_R):
            sl = slice(h * HD, (h + 1) * HD)
            w_s[b, h] = decay[:, sl]
            k_s[b, h] = k2[:, sl]
            v_s[b, h] = v[:, sl]
            a_s[b, h] = na[:, sl]
            b_s[b, h] = kb[:, sl]
            r_s[b, h] = r[:, sl]

    eye = (lax.broadcasted_iota(I32, (HD, HD), 0) == lax.broadcasted_iota(I32, (HD, HD), 1)).astype(F32)
    chains = [(b, h) for b in range(bb) for h in range(H_R)]

    def sub(i, carry):
        t0 = pl.multiple_of(i * SUBLANES, SUBLANES)
        S = [S_s[b, h] for b, h in chains]
        for j in range(SUBLANES):
            t = t0 + j
            for ci, (b, h) in enumerate(chains):
                row = lambda ref: ref[b, h, pl.ds(t, 1), :]
                sa = jnp.sum(S[ci] * row(a_s), axis=1, keepdims=True)
                vc = jnp.sum(eye * row(v_s), axis=1, keepdims=True)
                Sn = S[ci] * row(w_s) + sa * row(b_s) + vc * row(k_s)
                S[ci] = Sn
                y_s[b, h, pl.ds(t, 1), :] = _dot_nt(row(r_s).astype(BF16), Sn.astype(BF16))
        for ci, (b, h) in enumerate(chains):
            S_s[b, h] = S[ci]
        return carry

    lax.fori_loop(0, Tc // SUBLANES, sub, 0)

    for b in range(bb):
        y = jnp.concatenate([y_s[b, h] for h in range(H_R)], axis=-1)
        mean = _dot_exactw(y, gseg) * (1.0 / HD)
        d = y - mean
        var = _dot_exactw(d * d, gseg) * (1.0 / HD)
        yn = d * lax.rsqrt(var + RWKV_LN_EPS) * lg_ref[...] + lb_ref[...]
        o_ref[b] = (yn + bon_s[b]) * g_s[b]
    st_ref[...] = S_s[...]


def _rwkv(z3, shift0, wkv0, P, bb, Tc):
    B, T, _ = z3.shape
    vec = lambda n: pl.BlockSpec((1, n), lambda bi, c: (0, 0))
    mat = lambda r, n: pl.BlockSpec((r, n), lambda bi, c: (0, 0))
    kern = functools.partial(_rwkv_kernel, bb=bb, Tc=Tc)
    hv = lambda: pltpu.VMEM((bb, H_R, Tc, HD), F32)
    return pl.pallas_call(
        kern,
        grid=(B // bb, T // Tc),
        in_specs=[pl.BlockSpec((bb, Tc, N_RWKV_COLS), lambda bi, c: (bi, c, 0)),
                  pl.BlockSpec((bb, 1, N_RWKV_COLS), lambda bi, c: (bi, 0, 0)),
                  pl.BlockSpec((bb, H_R, HD, HD), lambda bi, c: (bi, 0, 0, 0)),
                  vec(N_RWKV_COLS), vec(C_R), mat(LANES, C_R), mat(LANES, C_R), vec(C_R), mat(LANES, C_R),
                  mat(LANES, C_R), mat(LANES, C_R), mat(LANES, C_R),
                  vec(C_R), vec(C_R), vec(C_R), vec(C_R), vec(C_R), mat(C_R, C_R)],
        out_specs=[pl.BlockSpec((bb, Tc, C_R), lambda bi, c: (bi, c, 0)),
                   pl.BlockSpec((bb, H_R, HD, HD), lambda bi, c: (bi, 0, 0, 0))],
        out_shape=[jax.ShapeDtypeStruct((B, T, C_R), F32),
                   jax.ShapeDtypeStruct((B, H_R, HD, HD), F32)],
        scratch_shapes=[pltpu.VMEM((bb, H_R, HD, HD), F32),
                        pltpu.VMEM((bb, 1, N_RWKV_COLS), F32),
                        hv(), hv(), hv(), hv(), hv(), hv(), hv(),
                        pltpu.VMEM((bb, Tc, C_R), F32),
                        pltpu.VMEM((bb, Tc, C_R), F32)],
        compiler_params=_cparams(("parallel", "arbitrary")),
        name="rwkv7",
    )(z3, shift0, wkv0, P["mu"], P["w0"], P["wl_hi"], P["wl_lo"], P["a0"], P["al_hi"], P["al_lo"],
      P["gl_hi"], P["gl_lo"], P["k_k"], P["k_a"], P["r_k"], P["lnx_g"], P["lnx_b"], P["gseg4"])


def _conv_kernel(z_ref, c0_ref, w_ref, b_ref, lg_ref, lb_ref, o_ref, ct_ref, buf, *, Tc):
    @pl.when(pl.program_id(1) == 0)
    def _():
        buf[0:HIST, :] = c0_ref[0]

    z = z_ref[0]
    u = z[:, 0:C_CONV] * _sigmoid(z[:, C_CONV:])
    buf[HIST:HIST + Tc, :] = u
    w = w_ref[...]
    acc = jnp.zeros((Tc, C_CONV), F32) + b_ref[...]
    off = HIST - (CONV_W - 1)
    for j in range(CONV_W):
        acc = acc + w[j:j + 1, :] * buf[off + j:off + j + Tc, :]
    mu = jnp.mean(acc, axis=-1, keepdims=True)
    d = acc - mu
    var = jnp.mean(d * d, axis=-1, keepdims=True)
    cn = d * lax.rsqrt(var + CONV_LN_EPS) * lg_ref[...] + lb_ref[...]
    o_ref[0] = cn * _sigmoid(cn)
    hist = buf[Tc:Tc + HIST, :]
    buf[0:HIST, :] = hist
    ct_ref[0] = hist


def _conv(z3, conv0p, P, Tc):
    B, T, _ = z3.shape
    vec = pl.BlockSpec((1, C_CONV), lambda b, c: (0, 0))
    return pl.pallas_call(
        functools.partial(_conv_kernel, Tc=Tc),
        grid=(B, T // Tc),
        in_specs=[pl.BlockSpec((1, Tc, 2 * C_CONV), lambda b, c: (b, c, ZC_CONV)),
                  pl.BlockSpec((1, HIST, C_CONV), lambda b, c: (b, 0, 0)),
                  pl.BlockSpec((CONV_W, C_CONV), lambda b, c: (0, 0)),
                  vec, vec, vec],
        out_specs=[pl.BlockSpec((1, Tc, C_CONV), lambda b, c: (b, c, 0)),
                   pl.BlockSpec((1, HIST, C_CONV), lambda b, c: (b, 0, 0))],
        out_shape=[jax.ShapeDtypeStruct((B, T, C_CONV), F32),
                   jax.ShapeDtypeStruct((B, HIST, C_CONV), F32)],
        scratch_shapes=[pltpu.VMEM((HIST + Tc, C_CONV), F32)],
        compiler_params=_cparams(("parallel", "arbitrary")),
        name="conformer_conv",
    )(z3, conv0p, P["conv_w"], P["conv_b"], P["conv_ln_g"], P["conv_ln_b"])


def _att_prep_kernel(k_ref, kw_ref, kg_ref, kig_ref, gseg_ref, kn_ref, kin_ref):
    k = k_ref[...]
    ms = _dot_exactw(k * k, gseg_ref[...]) * (1.0 / HD)
    kn_ref[...] = k * lax.rsqrt(ms + NORM_EPS) * kg_ref[...]
    kw = kw_ref[...]
    ki = kw[:, 0:D_IDX]
    msi = jnp.mean(ki * ki, axis=-1, keepdims=True)
    kin_ref[...] = ki * lax.rsqrt(msi + NORM_EPS) * kig_ref[...]


def _att_prep(z2, P, tm):
    M = z2.shape[0]
    return pl.pallas_call(
        _att_prep_kernel,
        grid=(M // tm,),
        in_specs=[pl.BlockSpec((tm, LANES), lambda i: (i, ZC_K)),
                  pl.BlockSpec((tm, LANES), lambda i: (i, ZC_KW)),
                  pl.BlockSpec((1, LANES), lambda i: (0, 0)),
                  pl.BlockSpec((1, D_IDX), lambda i: (0, 0)),
                  pl.BlockSpec((LANES, LANES), lambda i: (0, 0))],
        out_specs=[pl.BlockSpec((tm, LANES), lambda i: (i, 0)),
                   pl.BlockSpec((tm, D_IDX), lambda i: (i, 0))],
        out_shape=[jax.ShapeDtypeStruct((M, N_KV * HD), F32),
                   jax.ShapeDtypeStruct((M, D_IDX), F32)],
        compiler_params=_cparams(("parallel",)),
        name="att_prep",
    )(z2, z2, P["k_norm2"], P["kidx_norm"], P["gseg2"])


def _bias_kernel(rb_ref, idx_ref, o_ref):
    h = pl.program_id(0) % H_A
    idx = idx_ref[0]
    out = jnp.zeros(idx.shape, F32)
    for bk in range(N_BUCKETS):
        out = jnp.where(idx == bk, rb_ref[bk, h], out)
    o_ref[0] = out


def _bias_expand(rel_bias, idx):
    S, R, L = idx.shape
    return pl.pallas_call(
        _bias_kernel,
        grid=(S,),
        in_specs=[pl.BlockSpec(memory_space=pltpu.SMEM),
                  pl.BlockSpec((1, R, L), lambda s: (s, 0, 0))],
        out_specs=pl.BlockSpec((1, R, L), lambda s: (s, 0, 0)),
        out_shape=jax.ShapeDtypeStruct((S, R, L), F32),
        compiler_params=_cparams(("parallel",)),
        name="rel_bias_tiles",
    )(rel_bias, idx)


def _rel_bucket_np(dist):
    max_exact = N_BUCKETS // 2
    d_f = np.maximum(dist, max_exact).astype(np.float32)
    large = max_exact + (np.log(d_f / np.float32(max_exact)) / np.float32(math.log(MAX_DISTANCE / max_exact))
                         * np.float32(N_BUCKETS - max_exact)).astype(np.int32)
    return np.where(dist < max_exact, dist, np.minimum(large, N_BUCKETS - 1)).astype(np.int32)


def _sort_key(s):
    bits = pltpu.bitcast(s + 0.0, I32)
    return jnp.where(bits < 0, bits ^ 0x7FFFFFFF, bits)


def _q_prep(q, qg, gseg8):
    ms = _dot_exactw(q * q, gseg8) * (1.0 / HD)
    return (q * lax.rsqrt(ms + NORM_EPS) * qg * (HD ** -0.5)).astype(BF16)


def _attp_kernel(q_ref, qi_ref, kw_ref, kn_ref, v_ref, kin_ref, qg_ref, bias_ref, gseg_ref, tri_ref, ones_ref,
                 o_ref,
                 kk_s, vv_s, ki_s, qs_s, qi_s, wb_s, key_s, hi_s, lo_s, tau_s, madd_s, s_s, m_s, l_s, acc_s,
                 *, tq, topk, kg, ag):
    j = pl.program_id(1)
    nh = tq // LANES
    hp = m_s.shape[0] // tq
    nb = (j + 1) * nh
    KW = kg * LANES
    AW = ag * LANES
    ngk = (nb + kg - 1) // kg
    nga = (nb + ag - 1) // ag
    R = hp * tq

    @pl.when(j == 0)
    def _():
        knT = jnp.transpose(kn_ref[0])
        v = v_ref[0]
        for n in range(N_KV):
            kk_s[n] = knT[n * HD:(n + 1) * HD, :].astype(BF16)
            vv_s[n] = v[:, n * HD:(n + 1) * HD].astype(BF16)
        kin = kin_ref[0]
        kinT = jnp.transpose(jnp.concatenate([kin, jnp.zeros(kin.shape, F32)], axis=1))
        ki_s[...] = kinT[0:D_IDX, :].astype(BF16)

    qn = _q_prep(q_ref[0], qg_ref[...], gseg_ref[...])
    for n in range(N_KV):
        qs_s[n] = jnp.concatenate([qn[:, (n * G_A + g) * HD:(n * G_A + g + 1) * HD] for g in range(G_A)], axis=0)
    qi = (qi_ref[0] * (D_IDX ** -0.5)).astype(BF16)
    kw = kw_ref[0]
    for h in range(H_I):
        qi_s[h] = qi[:, h * D_IDX:(h + 1) * D_IDX]
        wb_s[h] = jnp.broadcast_to(kw[:, D_IDX + h:D_IDX + h + 1] * (H_I ** -0.5), (tq, KW))

    row = lax.broadcasted_iota(I32, (tq, KW), 0) + j * tq
    col0 = lax.broadcasted_iota(I32, (tq, KW), 1)

    def idx_body(gi, carry):
        off = pl.multiple_of(gi * KW, KW)
        kc = ki_s[:, pl.ds(off, KW)]
        s = jnp.zeros((tq, KW), F32)
        for h in range(H_I):
            s = s + jnp.maximum(_dot(qi_s[h], kc), 0.0) * wb_s[h]
        s = jnp.where(col0 + off <= row, s, -jnp.inf)
        key = _sort_key(s)
        key_s[:, pl.ds(off, KW)] = key
        hi_s[:, pl.ds(off, KW)] = lax.shift_right_arithmetic(key, 16).astype(I16)
        lo_s[:, pl.ds(off, KW)] = ((key & 0xFFFF) + I16_MIN).astype(I16)
        return carry

    lax.fori_loop(0, ngk, idx_body, 0)

    def key_tiles(gi, ref=key_s):
        off = pl.multiple_of(gi * KW, KW)
        key = ref[:, pl.ds(off, KW)]
        return [key[:, u * LANES:(u + 1) * LANES] for u in range(kg)]

    def lane_total(acc):
        return jnp.broadcast_to(jnp.sum(acc.astype(F32), axis=1, keepdims=True), (tq, LANES))

    one16 = jnp.ones((tq, LANES), I16)
    zero16 = jnp.zeros((tq, LANES), I16)

    def search16(ref, kneed):
        for ng in range(1, key_s.shape[1] // KW + 1):
            @pl.when(ngk == ng)
            def _(ng=ng):
                def bit_body(it, cur):
                    cand = cur + lax.shift_left(jnp.int32(1), jnp.asarray(15 - it, dtype=I32))
                    c16 = cand.astype(I16)
                    accs = [zero16] * kg
                    for gi in range(ng):
                        for u in range(kg):
                            kt = ref[:, gi * KW + u * LANES:gi * KW + (u + 1) * LANES]
                            accs[u] = accs[u] + jnp.where(kt >= c16, one16, zero16)
                    acc = accs[0]
                    for a in accs[1:]:
                        acc = acc + a
                    return jnp.where(lane_total(acc) >= kneed, cand, cur)

                tau_s[...] = lax.fori_loop(0, 16, bit_body, jnp.full((tq, LANES), I16_MIN, I32))

        return tau_s[...]

    tau_hi = search16(hi_s, float(topk))
    th16 = tau_hi.astype(I16)
    lo_pad = jnp.full((tq, LANES), I16_MIN, I16)

    def mid_body(gi, acc):
        off = pl.multiple_of(gi * KW, KW)
        los = key_tiles(gi, lo_s)
        for u, ht in enumerate(key_tiles(gi, hi_s)):
            acc = acc + jnp.where(ht > th16, one16, zero16)
            lo_s[:, pl.ds(off + u * LANES, LANES)] = jnp.where(ht == th16, los[u], lo_pad)
        return acc

    need_lo = float(topk) - lane_total(lax.fori_loop(0, ngk, mid_body, zero16))
    tau_lo = search16(lo_s, need_lo)
    tau = lax.shift_left(tau_hi, 16) + (tau_lo - I16_MIN)

    def cnt2_body(gi, carry):
        cge, cgt = carry
        for kt in key_tiles(gi):
            cge = cge + jnp.where(kt >= tau, 1.0, 0.0)
            cgt = cgt + jnp.where(kt > tau, 1.0, 0.0)
        return cge, cgt

    zero = jnp.zeros((tq, LANES), F32)
    cge, cgt = lax.fori_loop(0, ngk, cnt2_body, (zero, zero))
    need = float(topk) - lane_total(cgt)
    ties = jnp.max(lane_total(cge)) > float(topk)

    @pl.when(jnp.logical_not(ties))
    def _():
        def body(gi, carry):
            off = pl.multiple_of(gi * KW, KW)
            for u, kt in enumerate(key_tiles(gi)):
                take = jnp.where(kt >= tau, jnp.where(kt > KEY_NEG_INF, 0.0, NEG), NEG)
                madd_s[:, pl.ds(off + u * LANES, LANES)] = take
            return carry

        lax.fori_loop(0, ngk, body, 0)

    @pl.when(ties)
    def _():
        tri = tri_ref[...]
        ones = ones_ref[...]

        def body(c, run):
            off = pl.multiple_of(c * LANES, LANES)
            key = key_s[:, pl.ds(off, LANES)]
            eq = jnp.where(key == tau, 1.0, 0.0)
            eqb = eq.astype(BF16)
            pre = _dot(eqb, tri) + run
            take = jnp.where(key > tau, 1.0, jnp.where(pre <= need, eq, 0.0))
            take = jnp.where(key > KEY_NEG_INF, take, 0.0)
            madd_s[:, pl.ds(off, LANES)] = jnp.where(take > 0.5, 0.0, NEG)
            return run + _dot(eqb, ones)

        lax.fori_loop(0, ngk * kg, body, jnp.zeros((tq, LANES), F32))

    slab_list = [(gl, hf) for gl in range(hp) for hf in range(nh)]
    outs = []
    for n in range(N_KV):
      for gp in range(G_A // hp):
        m_s[...] = jnp.full((R, LANES), NEG, F32)
        l_s[...] = jnp.zeros((R, LANES), F32)
        acc_s[...] = jnp.zeros((R, HD), F32)

        def pass_a(gi, carry, n=n, gp=gp):
            off = pl.multiple_of(gi * AW, AW)
            kc = kk_s[n, :, pl.ds(off, AW)]
            s_all = _dot(qs_s[n, gp * R:(gp + 1) * R, :], kc)
            for gl, hf in slab_list:
                r0 = gl * tq + hf * LANES
                rows = pl.ds(r0, LANES)
                head = n * G_A + gp * hp + gl
                mx = m_s[rows, :]
                for u in range(ag):
                    cols = pl.ds(off + u * LANES, LANES)
                    dd = jnp.clip(j * nh + hf - (gi * ag + u), 0, 2)
                    su = (s_all[r0:r0 + LANES, u * LANES:(u + 1) * LANES] + bias_ref[dd, head]
                          + madd_s[hf * LANES:(hf + 1) * LANES, cols])
                    s_s[rows, cols] = su
                    mx = jnp.maximum(mx, su)
                m_s[rows, :] = mx
            return carry

        lax.fori_loop(0, nga, pass_a, 0)
        mb = jnp.broadcast_to(jnp.max(m_s[...], axis=1, keepdims=True), (R, LANES))
        m_s[...] = mb

        def pass_b(gi, carry, n=n):
            off = pl.multiple_of(gi * AW, AW)
            slabs = []
            for gl, hf in slab_list:
                rows = pl.ds(gl * tq + hf * LANES, LANES)
                mrow = m_s[rows, :]
                lacc = l_s[rows, :]
                ps = []
                for u in range(ag):
                    p = jnp.exp(s_s[rows, pl.ds(off + u * LANES, LANES)] - mrow)
                    lacc = lacc + p
                    ps.append(p.astype(BF16))
                l_s[rows, :] = lacc
                slabs.append(ps[0] if ag == 1 else jnp.concatenate(ps, axis=1))
            acc_s[...] += _dot(jnp.concatenate(slabs, axis=0), vv_s[n, pl.ds(off, AW), :])
            return carry

        lax.fori_loop(0, nga, pass_b, 0)
        o = acc_s[...] / jnp.sum(l_s[...], axis=1, keepdims=True)
        for gl in range(hp):
            outs.append(o[gl * tq:(gl + 1) * tq, :])
    o_ref[0] = jnp.concatenate(outs, axis=-1)


def _att_prompt(z3, kn3, kin3, P, bias_p, topk, tq):
    B, T, _ = z3.shape
    nblk = T // LANES
    kg = 4 if nblk % 4 == 0 else (2 if nblk % 2 == 0 else 1)
    ag = kg
    rp = min(G_A * tq, ATT_ROWS)
    cst = lambda shp: pl.BlockSpec(shp, lambda b, j: tuple(0 for _ in shp))
    return pl.pallas_call(
        functools.partial(_attp_kernel, tq=tq, topk=topk, kg=kg, ag=ag),
        grid=(B, T // tq),
        in_specs=[pl.BlockSpec((1, tq, C_A), lambda b, j: (b, j, ZC_Q)),
                  pl.BlockSpec((1, tq, H_I * D_IDX), lambda b, j: (b, j, ZC_QI)),
                  pl.BlockSpec((1, tq, LANES), lambda b, j: (b, j, ZC_KW)),
                  pl.BlockSpec((1, T, N_KV * HD), lambda b, j: (b, 0, 0)),
                  pl.BlockSpec((1, T, N_KV * HD), lambda b, j: (b, 0, ZC_V)),
                  pl.BlockSpec((1, T, D_IDX), lambda b, j: (b, 0, 0)),
                  cst((1, C_A)), cst((3, H_A, LANES, LANES)), cst((C_A, C_A)), cst((LANES, LANES)),
                  cst((LANES, LANES))],
        out_specs=pl.BlockSpec((1, tq, C_A), lambda b, j: (b, j, 0)),
        out_shape=jax.ShapeDtypeStruct((B, T, C_A), F32),
        scratch_shapes=[pltpu.VMEM((N_KV, HD, T), BF16),
                        pltpu.VMEM((N_KV, T, HD), BF16),
                        pltpu.VMEM((D_IDX, T), BF16),
                        pltpu.VMEM((N_KV, G_A * tq, HD), BF16),
                        pltpu.VMEM((H_I, tq, D_IDX), BF16),
                        pltpu.VMEM((H_I, tq, kg * LANES), F32),
                        pltpu.VMEM((tq, T), I32),
                        pltpu.VMEM((tq, T), I16),
                        pltpu.VMEM((tq, T), I16),
                        pltpu.VMEM((tq, LANES), I32),
                        pltpu.VMEM((tq, T), F32),
                        pltpu.VMEM((rp, T), F32),
                        pltpu.VMEM((rp, LANES), F32),
                        pltpu.VMEM((rp, LANES), F32),
                        pltpu.VMEM((rp, HD), F32)],
        compiler_params=_cparams(("parallel", "arbitrary")),
        name="dsa_prompt",
    )(z3, z3, z3, kn3, z3, kin3, P["q_norm8"], bias_p, P["gseg8"], P["tri"], P["ones"])


def _pad_transpose(x):
    r, c = x.shape
    if c < LANES:
        x = jnp.concatenate([x, jnp.zeros((r, LANES - c), F32)], axis=1)
    x = jnp.concatenate([x, jnp.zeros((LANES - r, LANES), F32)], axis=0)
    return jnp.transpose(x)


def _atts_kernel(pt_ref, ckT_ref, cvT_ref, ckiT_ref, q_ref, qi_ref, kw_ref, kn_ref, vn_ref, kin_ref, qg_ref,
                 bias_ref, gseg_ref, tri_ref, ones_ref,
                 o_ref,
                 kT_s, vT_s, kiT_s, madd_s, sem, *, layer, npages, nbatch, Ts, topk):
    b = pl.program_id(0)
    slot = b % 2
    P = LANES
    L = (npages + 1) * P

    def page_copies(seq, sl, p):
        pg = pt_ref[seq, p]
        col = pl.multiple_of(p * P, P)
        return (pltpu.make_async_copy(ckT_ref.at[layer, pg], kT_s.at[sl, :, :, pl.ds(col, P)], sem.at[sl, 0]),
                pltpu.make_async_copy(cvT_ref.at[layer, pg], vT_s.at[sl, :, :, pl.ds(col, P)], sem.at[sl, 1]),
                pltpu.make_async_copy(ckiT_ref.at[layer, pg], kiT_s.at[sl, :, pl.ds(col, P)], sem.at[sl, 2]))

    def start_pages(seq, sl):
        def body(p, carry):
            for cp in page_copies(seq, sl, p):
                cp.start()
            return carry
        lax.fori_loop(0, npages, body, 0)

    def wait_pages(seq, sl):
        def body(p, carry):
            for cp in page_copies(seq, sl, p):
                cp.wait()
            return carry
        lax.fori_loop(0, npages, body, 0)

    @pl.when(b == 0)
    def _():
        start_pages(0, 0)

    @pl.when(b + 1 < nbatch)
    def _():
        start_pages(b + 1, 1 - slot)

    knT = _pad_transpose(kn_ref[0])
    vnT = _pad_transpose(vn_ref[0])
    for n in range(N_KV):
        kT_s[slot, n, :, npages * P:L] = knT[n * HD:(n + 1) * HD, :]
        vT_s[slot, n, :, npages * P:L] = vnT[n * HD:(n + 1) * HD, :]
    kiT_s[slot, :, npages * P:L] = _pad_transpose(kin_ref[0])[0:D_IDX, :]

    qn = _q_prep(q_ref[0], qg_ref[...], gseg_ref[...])
    qi = (qi_ref[0] * (D_IDX ** -0.5)).astype(BF16)
    qis = jnp.concatenate([qi[:, h * D_IDX:(h + 1) * D_IDX] for h in range(H_I)], axis=0)
    kw = kw_ref[0]

    wait_pages(b, slot)

    d = _dot(qis, kiT_s[slot].astype(BF16))
    s = jnp.zeros((Ts, L), F32)
    for h in range(H_I):
        s = s + jnp.maximum(d[h * Ts:(h + 1) * Ts, :], 0.0) * (kw[:, D_IDX + h:D_IDX + h + 1] * (H_I ** -0.5))
    col = lax.broadcasted_iota(I32, (Ts, L), 1)
    row = lax.broadcasted_iota(I32, (Ts, L), 0) + npages * P
    key = _sort_key(jnp.where(col <= row, s, -jnp.inf))

    def bit_body(it, cur):
        cand = cur + lax.shift_left(jnp.int32(1), jnp.asarray(31 - it, dtype=I32))
        cnt = jnp.sum(jnp.where(key >= cand, 1.0, 0.0), axis=1, keepdims=True)
        return jnp.where(cnt >= float(topk), cand, cur)

    tau = lax.fori_loop(0, 32, bit_body, jnp.full((Ts, 1), INT_MIN, I32))
    need = float(topk) - jnp.sum(jnp.where(key > tau, 1.0, 0.0), axis=1, keepdims=True)
    cge = jnp.sum(jnp.where(key >= tau, 1.0, 0.0), axis=1, keepdims=True)
    ties = jnp.max(cge) > float(topk)

    @pl.when(jnp.logical_not(ties))
    def _():
        madd_s[...] = jnp.where(key >= tau, jnp.where(key > KEY_NEG_INF, 0.0, NEG), NEG)

    @pl.when(ties)
    def _():
        tri = tri_ref[...]
        ones = ones_ref[...]
        eq = jnp.where(key == tau, 1.0, 0.0)
        run = jnp.zeros((Ts, LANES), F32)
        for c in range(L // LANES):
            sl = slice(c * LANES, (c + 1) * LANES)
            eqc = eq[:, sl]
            eqb = eqc.astype(BF16)
            pre = _dot(eqb, tri) + run
            take = jnp.where(key[:, sl] > tau, 1.0, jnp.where(pre <= need, eqc, 0.0))
            take = jnp.where(key[:, sl] > KEY_NEG_INF, take, 0.0)
            madd_s[:, sl] = jnp.where(take > 0.5, 0.0, NEG)
            run = run + _dot(eqb, ones)

    madd = madd_s[...]
    madd4 = jnp.concatenate([madd] * G_A, axis=0)
    outs = []
    for n in range(N_KV):
        qs = jnp.concatenate([qn[:, (n * G_A + g) * HD:(n * G_A + g + 1) * HD] for g in range(G_A)], axis=0)
        sc = _dot(qs, kT_s[slot, n].astype(BF16)) + bias_ref[n] + madd4
        m = jnp.max(sc, axis=1, keepdims=True)
        pe = jnp.exp(sc - m)
        l = jnp.sum(pe, axis=1, keepdims=True)
        o = _dot_nt(pe.astype(BF16), vT_s[slot, n].astype(BF16)) / l
        for g in range(G_A):
            outs.append(o[g * Ts:(g + 1) * Ts, :])
    o_ref[0] = jnp.concatenate(outs, axis=-1)


def _att_sample(layer, page_table, ckT, cvT, ckiT, z3, kn3, kin3, P, bias_s, topk):
    B, Ts, _ = z3.shape
    npages = page_table.shape[1]
    L = (npages + 1) * LANES
    hbm = pl.BlockSpec(memory_space=pl.ANY)
    cst = lambda shp: pl.BlockSpec(shp, lambda b, pt: tuple(0 for _ in shp))
    grid_spec = pltpu.PrefetchScalarGridSpec(
        num_scalar_prefetch=1,
        grid=(B,),
        in_specs=[hbm, hbm, hbm,
                  pl.BlockSpec((1, Ts, C_A), lambda b, pt: (b, 0, ZC_Q)),
                  pl.BlockSpec((1, Ts, H_I * D_IDX), lambda b, pt: (b, 0, ZC_QI)),
                  pl.BlockSpec((1, Ts, LANES), lambda b, pt: (b, 0, ZC_KW)),
                  pl.BlockSpec((1, Ts, N_KV * HD), lambda b, pt: (b, 0, 0)),
                  pl.BlockSpec((1, Ts, N_KV * HD), lambda b, pt: (b, 0, ZC_V)),
                  pl.BlockSpec((1, Ts, D_IDX), lambda b, pt: (b, 0, 0)),
                  cst((1, C_A)), cst((N_KV, G_A * Ts, L)), cst((C_A, C_A)), cst((LANES, LANES)),
                  cst((LANES, LANES))],
        out_specs=pl.BlockSpec((1, Ts, C_A), lambda b, pt: (b, 0, 0)),
        scratch_shapes=[pltpu.VMEM((2, N_KV, HD, L), F32),
                        pltpu.VMEM((2, N_KV, HD, L), F32),
                        pltpu.VMEM((2, D_IDX, L), F32),
                        pltpu.VMEM((Ts, L), F32),
                        pltpu.SemaphoreType.DMA((2, 3))])
    return pl.pallas_call(
        functools.partial(_atts_kernel, layer=layer, npages=npages, nbatch=B, Ts=Ts, topk=topk),
        grid_spec=grid_spec,
        out_shape=jax.ShapeDtypeStruct((B, Ts, C_A), F32),
        compiler_params=_cparams(("arbitrary",)),
        name="dsa_sample",
    )(page_table, ckT, cvT, ckiT, z3, z3, z3, kn3, z3, kin3, P["q_norm8"], bias_s, P["gseg8"], P["tri"], P["ones"])


def _out_proj_kernel(x_ref, orw_ref, ocv_ref, oat_ref, w_ref, y_ref):
    acc = _dot(orw_ref[...].astype(BF16), w_ref[0:C_R, :])
    acc = acc + _dot(ocv_ref[...].astype(BF16), w_ref[C_R:C_R + C_CONV, :])
    acc = acc + _dot(oat_ref[...].astype(BF16), w_ref[C_R + C_CONV:, :])
    y_ref[...] = x_ref[...] + acc


def _out_proj(x2, orw, ocv, oat, w_bf, tm):
    M, D = x2.shape
    row = lambda n: pl.BlockSpec((tm, n), lambda i: (i, 0))
    return pl.pallas_call(
        _out_proj_kernel,
        grid=(M // tm,),
        in_specs=[row(D), row(C_R), row(C_CONV), row(C_A), pl.BlockSpec((D, D), lambda i: (0, 0))],
        out_specs=row(D),
        out_shape=jax.ShapeDtypeStruct((M, D), F32),
        compiler_params=_cparams(("parallel",)),
        name="out_proj",
    )(x2, orw, ocv, oat, w_bf)


def _gelu(x):
    return 0.5 * x * (1.0 + lax.erf(x * (2.0 ** -0.5)))


def _ffn_kernel(x_ref, p_ref, ng_ref, upg_ref, upv_ref, cw_ref, cb_ref, dn_ref, h_ref, h2_ref, pg_ref, pgate_ref,
                pproj_ref,
                xo_ref, tail_ref,
                xn_s, acc_s, hist_s, *, tm, nF, seq_tiles, seq_len):
    i = pl.program_id(0)
    f = pl.program_id(1)

    @pl.when(f == 0)
    def _():
        xn_s[...] = _rms(x_ref[...], ng_ref[...]).astype(BF16)
        acc_s[...] = jnp.zeros(acc_s.shape, F32)

    xn = xn_s[...]
    g = _dot(xn, upg_ref[...])
    val = _dot(xn, upv_ref[...])
    row = lax.broadcasted_iota(I32, g.shape, 0)
    if seq_tiles is not None:
        first = (i % seq_tiles) == 0
        hs = hist_s[f]
        h0 = h_ref[0]
        hm2 = jnp.where(first, h0[0:1, :], hs[SUBLANES - 2:SUBLANES - 1, :])
        hm1 = jnp.where(first, h0[1:2, :], hs[SUBLANES - 1:SUBLANES, :])
        g1 = jnp.where(row == 0, hm1, pltpu.roll(g, 1, 0))
        g2 = jnp.where(row == 0, hm2, jnp.where(row == 1, hm1, pltpu.roll(g, 2, 0)))
        slab = g[tm - SUBLANES:tm, :]
        hist_s[f] = slab
        tail_ref[0] = slab
    else:
        t = row % seq_len
        g1 = jnp.where(t == 0, 0.0, pltpu.roll(g, 1, 0)) + h_ref[...]
        g2 = jnp.where(t < 2, 0.0, pltpu.roll(g, 2, 0)) + h2_ref[...]
        tail_ref[...] = g
    cw = cw_ref[...]
    gate = cw[0:1, :] * g2 + cw[1:2, :] * g1 + cw[2:3, :] * g + cb_ref[...]
    act = (_gelu(gate) * val).astype(BF16)
    acc_s[...] += _dot(act, dn_ref[...])

    @pl.when(f == nF - 1)
    def _():
        x2 = x_ref[...] + acc_s[...]
        xn2 = _rms(x2, pg_ref[...]).astype(BF16)
        gate2 = _sigmoid(_dot(xn2, pgate_ref[...]))
        xo_ref[...] = x2 + _dot(p_ref[...].astype(BF16), pproj_ref[...]) * gate2


def _ffn(x2, p2, ffn0, P, tm, B, T):
    M, D = x2.shape
    tf = 256
    nF = D_FF // tf
    DP = p2.shape[1]
    whole = tm % T == 0
    if whole:
        seq_tiles, seq_len = None, T
        h1 = jnp.pad(ffn0[:, 1:2, :], ((0, 0), (0, T - 1), (0, 0))).reshape(M, D_FF)
        h2 = jnp.pad(ffn0, ((0, 0), (0, T - 2), (0, 0))).reshape(M, D_FF)
        h_specs = [pl.BlockSpec((tm, tf), lambda i, f: (i, f)), pl.BlockSpec((tm, tf), lambda i, f: (i, f))]
        tail_spec = pl.BlockSpec((tm, tf), lambda i, f: (i, f))
        tail_shape = jax.ShapeDtypeStruct((M, D_FF), F32)
    else:
        seq_tiles, seq_len = T // tm, T
        h1, h2 = ffn0, ffn0
        h_specs = [pl.BlockSpec((1, 2, tf), lambda i, f: (i // seq_tiles, 0, f)),
                   pl.BlockSpec((1, 2, tf), lambda i, f: (i // seq_tiles, 0, f))]
        tail_spec = pl.BlockSpec((1, SUBLANES, tf), lambda i, f: (i, 0, f))
        tail_shape = jax.ShapeDtypeStruct((M // tm, SUBLANES, D_FF), F32)
    cst = lambda shp: pl.BlockSpec(shp, lambda i, f: tuple(0 for _ in shp))
    xo, tail = pl.pallas_call(
        functools.partial(_ffn_kernel, tm=tm, nF=nF, seq_tiles=seq_tiles, seq_len=seq_len),
        grid=(M // tm, nF),
        in_specs=[pl.BlockSpec((tm, D), lambda i, f: (i, 0)),
                  pl.BlockSpec((tm, DP), lambda i, f: (i, 0)),
                  cst((1, D)),
                  pl.BlockSpec((D, tf), lambda i, f: (0, f)),
                  pl.BlockSpec((D, tf), lambda i, f: (0, nF + f)),
                  pl.BlockSpec((3, tf), lambda i, f: (0, f)),
                  pl.BlockSpec((1, tf), lambda i, f: (0, f)),
                  pl.BlockSpec((tf, D), lambda i, f: (f, 0))] + h_specs +
                 [cst((1, D)), cst((D, D)), cst((DP, D))],
        out_specs=[pl.BlockSpec((tm, D), lambda i, f: (i, 0)), tail_spec],
        out_shape=[jax.ShapeDtypeStruct((M, D), F32), tail_shape],
        scratch_shapes=[pltpu.VMEM((tm, D), BF16), pltpu.VMEM((tm, D), F32), pltpu.VMEM((nF, SUBLANES, tf), F32)],
        compiler_params=_cparams(("parallel", "arbitrary")),
        name="conv_ffn_ple",
    )(x2, p2, P["norm_ffn"], P["ffn_up"], P["ffn_up"], P["ffn_conv_w"], P["ffn_conv_b"], P["ffn_down"], h1, h2,
      P["ple_norm"], P["ple_gate"], P["ple_proj"])
    if whole:
        ffn_T = tail.reshape(B, T, D_FF)[:, T - 2:, :]
    else:
        ffn_T = tail.reshape(B, seq_tiles, SUBLANES, D_FF)[:, seq_tiles - 1, SUBLANES - 2:, :]
    return xo, ffn_T


def _seg_ones(n, seg):
    i = np.arange(n)
    return jnp.asarray((i[:, None] // seg) == (i[None, :] // seg), dtype=BF16)


def _hl(w):
    hi = w.astype(BF16)
    return hi, (w - hi.astype(F32)).astype(BF16)


def _layer_params(i, W):
    P = {}
    w_in = W["w_in"][i]
    D = w_in.shape[0]
    P["w_in"] = jnp.concatenate([w_in[:, :N_RWKV_COLS], jnp.zeros((D, 1024 - N_RWKV_COLS), F32),
                                 w_in[:, N_RWKV_COLS:], jnp.zeros((D, NZ - 128 - N_IN), F32)], axis=1).astype(BF16)
    row = lambda v: v.reshape(1, -1)
    P["norm_mix"] = row(W["norm_mix"][i])
    P["mu"] = row(W["mu_shift"][i])
    P["w0"] = row(W["w0"][i])
    P["a0"] = row(W["a0"][i])
    z = lambda r: jnp.zeros((r, C_R), F32)
    P["wl_hi"], P["wl_lo"] = _hl(jnp.concatenate([W["w_lora"][i], z(96)], axis=0))
    P["al_hi"], P["al_lo"] = _hl(jnp.concatenate([z(32), W["a_lora"][i], z(64)], axis=0))
    P["gl_hi"], P["gl_lo"] = _hl(jnp.concatenate([z(64), W["g_lora"][i]], axis=0))
    P["k_k"] = row(W["k_k"][i])
    P["k_a"] = row(W["k_a"][i])
    P["r_k"] = row(W["r_k"][i])
    P["lnx_g"] = row(W["lnx_g"][i])
    P["lnx_b"] = row(W["lnx_b"][i])
    P["conv_w"] = W["conv_w"][i]
    P["conv_b"] = row(W["conv_b"][i])
    P["conv_ln_g"] = row(W["conv_ln_g"][i])
    P["conv_ln_b"] = row(W["conv_ln_b"][i])
    P["q_norm8"] = row(jnp.tile(W["q_norm"][i], H_A))
    P["k_norm2"] = row(jnp.tile(W["k_norm"][i], N_KV))
    P["kidx_norm"] = row(W["kidx_norm"][i])
    P["w_out"] = W["w_out"][i].astype(BF16)
    P["norm_ffn"] = row(W["norm_ffn"][i])
    P["ffn_up"] = W["ffn_up"][i].astype(BF16)
    P["ffn_conv_w"] = W["ffn_conv_w"][i]
    P["ffn_conv_b"] = row(W["ffn_conv_b"][i])
    P["ffn_down"] = W["ffn_down"][i].astype(BF16)
    P["ple_norm"] = row(W["ple_norm"][i])
    P["ple_gate"] = W["ple_gate"][i].astype(BF16)
    P["ple_proj"] = W["ple_proj"][i].astype(BF16)
    P["gseg4"] = _seg_ones(C_R, HD)
    P["gseg2"] = _seg_ones(N_KV * HD, HD)
    P["gseg8"] = _seg_ones(C_A, HD)
    P["tri"] = jnp.asarray(np.arange(LANES)[:, None] <= np.arange(LANES)[None, :], dtype=BF16)
    P["ones"] = jnp.ones((LANES, LANES), BF16)
    return P


def _pick_tile(n, pref):
    t = min(n, pref)
    while n % t:
        t //= 2
    return t


def _layer(layer, x, p_i, st, P, att_fn, bb):
    B, T, D = x.shape
    M = B * T
    shift0, wkv0, conv0, ffn0 = st
    x2 = x.reshape(M, D)
    tm = _pick_tile(M, 512)
    z2 = _in_proj(x2, P["norm_mix"], P["w_in"], tm)
    z3 = z2.reshape(B, T, NZ)
    Tc = _pick_tile(T, 256)
    o_rw, wkv_T = _rwkv(z3, shift0.reshape(B, 1, N_RWKV_COLS), wkv0, P, bb, Tc)
    conv0p = jnp.pad(conv0, ((0, 0), (HIST - (CONV_W - 1), 0), (0, 0)))
    o_cv, conv_Tp = _conv(z3, conv0p, P, _pick_tile(T, 512))
    kn2, kin2 = _att_prep(z2, P, tm)
    kn3 = kn2.reshape(B, T, N_KV * HD)
    kin3 = kin2.reshape(B, T, D_IDX)
    o_at = att_fn(layer, z3, kn3, kin3, P)
    xa = _out_proj(x2, o_rw.reshape(M, C_R), o_cv.reshape(M, C_CONV), o_at.reshape(M, C_A), P["w_out"], tm)
    tmf = M if M <= 1024 else _pick_tile(T, 1024)
    xo, ffn_T = _ffn(xa, p_i.reshape(M, -1), ffn0, P, tmf, B, T)
    k_new = kn3.reshape(B, T, N_KV, HD)
    v_new = z3[:, :, ZC_V * LANES:(ZC_V + 1) * LANES].reshape(B, T, N_KV, HD)
    shift_T = z3[:, T - 1, :N_RWKV_COLS]
    conv_T = conv_Tp[:, HIST - (CONV_W - 1):, :]
    return xo.reshape(B, T, D), (k_new, v_new, kin3, wkv_T, shift_T, conv_T, ffn_T)


def kernel(x_prompt, x_sample, cache_k, cache_v, cache_kidx, state_wkv, state_shift, state_conv, state_ffn, page_table, p_prompt, p_sample, norm_mix, w_in, w_out, mu_shift, w0, w_lora, a0, a_lora, g_lora, k_k, k_a, r_k, lnx_g, lnx_b, conv_w, conv_b, conv_ln_g, conv_ln_b, q_norm, k_norm, kidx_norm, rel_bias, norm_ffn, ffn_up, ffn_conv_w, ffn_conv_b, ffn_down, ple_norm, ple_proj, ple_gate):
    W = dict(norm_mix=norm_mix, w_in=w_in, w_out=w_out, mu_shift=mu_shift, w0=w0, w_lora=w_lora, a0=a0,
             a_lora=a_lora, g_lora=g_lora, k_k=k_k, k_a=k_a, r_k=r_k, lnx_g=lnx_g, lnx_b=lnx_b,
             conv_w=conv_w, conv_b=conv_b, conv_ln_g=conv_ln_g, conv_ln_b=conv_ln_b, q_norm=q_norm,
             k_norm=k_norm, kidx_norm=kidx_norm, norm_ffn=norm_ffn, ffn_up=ffn_up,
             ffn_conv_w=ffn_conv_w, ffn_conv_b=ffn_conv_b, ffn_down=ffn_down, ple_norm=ple_norm,
             ple_proj=ple_proj, ple_gate=ple_gate)
    depth = w_in.shape[0]
    params = [_layer_params(i, W) for i in range(depth)]

    B, T, D = x_prompt.shape
    tq = ATT_TQ if T % ATT_TQ == 0 else LANES
    topk_p = min(TOPK_MAX, T // 4)
    t_ = np.arange(LANES)[:, None]
    s_ = np.arange(LANES)[None, :]
    tiles = np.stack([_rel_bucket_np(np.maximum(t_ - s_, 0)), _rel_bucket_np(LANES + t_ - s_),
                      np.full((LANES, LANES), N_BUCKETS - 1, np.int32)])
    idx_p = np.broadcast_to(tiles[:, None], (3, H_A, LANES, LANES)).reshape(3 * H_A, LANES, LANES)
    bias_p = _bias_expand(rel_bias, jnp.asarray(idx_p)).reshape(3, H_A, LANES, LANES)

    def att_p(layer, z3, kn3, kin3, P):
        return _att_prompt(z3, kn3, kin3, P, bias_p, topk_p, tq)

    st0 = (jnp.zeros((B, N_RWKV_COLS), F32), jnp.zeros((B, H_R, HD, HD), F32),
           jnp.zeros((B, CONV_W - 1, C_CONV), F32), jnp.zeros((B, 2, D_FF), F32))
    x = x_prompt
    new_p = []
    for i in range(depth):
        x, new = _layer(i, x, p_prompt[i], st0, params[i], att_p, B)
        new_p.append(new)
    y_prompt = x

    Bs, Ts, _ = x_sample.shape
    npages = page_table.shape[1]
    psz = cache_k.shape[2]
    past_len = npages * psz
    topk_s = min(TOPK_MAX, (past_len + Ts) // 4)
    Ls = past_len + psz
    qpos = past_len + np.arange(Ts)[:, None]
    bk = _rel_bucket_np(np.maximum(qpos - np.arange(Ls)[None, :], 0))
    idx_s = np.broadcast_to(bk[None], (H_A, Ts, Ls))
    bias_s = _bias_expand(rel_bias, jnp.asarray(idx_s)).reshape(N_KV, G_A * Ts, Ls)
    ckT = jnp.transpose(cache_k, (0, 1, 3, 4, 2))
    cvT = jnp.transpose(cache_v, (0, 1, 3, 4, 2))
    ckiT = jnp.transpose(cache_kidx, (0, 1, 3, 2))

    def att_s(layer, z3, kn3, kin3, P):
        return _att_sample(layer, page_table, ckT, cvT, ckiT, z3, kn3, kin3, P, bias_s, topk_s)

    x = x_sample
    new_s = []
    for i in range(depth):
        st = (state_shift[i], state_wkv[i], state_conv[i], state_ffn[i])
        x, new = _layer(i, x, p_sample[i], st, params[i], att_s, 4)
        new_s.append(new)
    y_sample = x

    stack = lambda lst: tuple(jnp.stack(t) for t in zip(*lst))
    return (y_prompt, y_sample) + stack(new_p) + stack(new_s)
```

```python
import functools
import math

import numpy as np
import jax
import jax.numpy as jnp
from jax import lax
from jax.experimental import pallas as pl
from jax.experimental.pallas import tpu as pltpu

F32 = jnp.float32
BF16 = jnp.bfloat16
I32 = jnp.int32
I16 = jnp.int16

HD = 64
C_R = 256
H_R = C_R // HD
C_CONV = 256
CONV_W = 31
C_A = 512
H_A = C_A // HD
N_KV = 2
G_A = H_A // N_KV
H_I = 4
D_IDX = 64
TOPK_MAX = 256
N_BUCKETS = 32
MAX_DISTANCE = 128
D_FF = 2816
NORM_EPS = 1e-6
RWKV_LN_EPS = 64e-5
CONV_LN_EPS = 1e-5
N_RWKV_COLS = 896
N_IN = 2500

LANES = 128
SUBLANES = 8
VMEM_LIMIT = 56 * 1024 * 1024

NZ = 2688
ZC_CONV = 1024 // 512
ZC_Q = 1536 // 512
ZC_K = 2048 // 128
ZC_V = 2176 // 128
ZC_QI = 2304 // 256
ZC_KW = 2560 // 128

HIST = 32
ATT_ROWS = 512
ATT_TQ = 256
NEG = -1e30
INT_MIN = -2 ** 31
I16_MIN = -2 ** 15
KEY_NEG_INF = int(np.array(0xFF800000 ^ 0x7FFFFFFF, dtype=np.uint32).view(np.int32))


def _cparams(sem):
    return pltpu.CompilerParams(dimension_semantics=sem, vmem_limit_bytes=VMEM_LIMIT)


def _split2(x):
    hi = x.astype(BF16)
    lo = (x - hi.astype(F32)).astype(BF16)
    return hi, lo


def _dot(a, b):
    return jnp.dot(a, b, preferred_element_type=F32)


def _dot_nt(a, b):
    return lax.dot_general(a, b, (((1,), (1,)), ((), ())), preferred_element_type=F32)


def _dot_exactw(x, w_bf):
    hi, lo = _split2(x)
    return _dot(hi, w_bf) + _dot(lo, w_bf)


def _dot_hl(x, w_hi, w_lo):
    hi, lo = _split2(x)
    return _dot(hi, w_hi) + (_dot(lo, w_hi) + _dot(hi, w_lo))


def _sigmoid(x):
    return 1.0 / (1.0 + jnp.exp(-x))


def _rms(x, g):
    ms = jnp.mean(x * x, axis=-1, keepdims=True)
    return x * lax.rsqrt(ms + NORM_EPS) * g


def _in_proj_kernel(x_ref, g_ref, w_ref, z_ref, xn_s):
    @pl.when(pl.program_id(1) == 0)
    def _():
        xn_s[...] = _rms(x_ref[...], g_ref[...]).astype(BF16)

    z_ref[...] = _dot(xn_s[...], w_ref[...])


def _in_proj(x2, g, w_bf, tm):
    M, D = x2.shape
    tn = NZ
    return pl.pallas_call(
        _in_proj_kernel,
        grid=(M // tm, NZ // tn),
        in_specs=[pl.BlockSpec((tm, D), lambda i, j: (i, 0)),
                  pl.BlockSpec((1, D), lambda i, j: (0, 0)),
                  pl.BlockSpec((D, tn), lambda i, j: (0, j))],
        out_specs=pl.BlockSpec((tm, tn), lambda i, j: (i, j)),
        out_shape=jax.ShapeDtypeStruct((M, NZ), F32),
        scratch_shapes=[pltpu.VMEM((tm, D), BF16)],
        compiler_params=_cparams(("parallel", "arbitrary")),
        name="in_proj",
    )(x2, g, w_bf)


def _rwkv_kernel(z_ref, sh0_ref, wkv0_ref, mu_ref, w0_ref, wl_hi, wl_lo, a0_ref, al_hi, al_lo, gl_hi, gl_lo,
                 kk_ref, ka_ref, rk_ref, lg_ref, lb_ref, gseg_ref,
                 o_ref, st_ref,
                 S_s, carry_s, w_s, k_s, v_s, a_s, b_s, r_s, y_s, bon_s, g_s, *, bb, Tc):
    c = pl.program_id(1)
    NP = H_R // 2

    @pl.when(c == 0)
    def _():
        for b in range(bb):
            for p in range(NP):
                S_s[b, p] = jnp.concatenate([wkv0_ref[b, 2 * p], wkv0_ref[b, 2 * p + 1]], axis=1)
        carry_s[...] = sh0_ref[...]

    gseg = gseg_ref[...]
    mu = mu_ref[...]
    rid = lax.broadcasted_iota(I32, (Tc, N_RWKV_COLS), 0)
    for b in range(bb):
        z = z_ref[b]
        prev = jnp.where(rid == 0, carry_s[b], pltpu.roll(z, 1, 0))
        carry_s[b] = z[Tc - 1:Tc, :]
        zs = z + (prev - z) * mu
        r = zs[:, 0:C_R]
        k = zs[:, C_R:2 * C_R]
        v = zs[:, 2 * C_R:3 * C_R]
        t6 = zs[:, 3 * C_R:]
        lw = w0_ref[...] + _dot_hl(jnp.tanh(t6), wl_hi[...], wl_lo[...])
        nl = -lw
        softplus = jnp.maximum(nl, 0.0) + jnp.log(1.0 + jnp.exp(-jnp.abs(nl)))
        decay = jnp.exp(-jnp.exp(-softplus - 0.5))
        a = _sigmoid(a0_ref[...] + _dot_hl(t6, al_hi[...], al_lo[...]))
        g_s[b] = _dot_hl(_sigmoid(t6), gl_hi[...], gl_lo[...])
        kk = k * kk_ref[...]
        nrm = jnp.sqrt(_dot_exactw(kk * kk, gseg))
        kk = kk / jnp.maximum(nrm, 1e-12)
        k2 = k * (1.0 + (a - 1.0) * ka_ref[...])
        bon_s[b] = _dot_exactw(r * k2 * rk_ref[...], gseg) * v
        na = -kk
        kb = kk * a
        for p in range(NP):
            sl = slice(p * 2 * HD, (p + 1) * 2 * HD)
            w_s[b, p] = decay[:, sl]
            k_s[b, p] = k2[:, sl]
            v_s[b, p] = v[:, sl]
            a_s[b, p] = na[:, sl]
            b_s[b, p] = kb[:, sl]
            r_s[b, p] = r[:, sl]

    W2 = 2 * HD
    i0 = lax.broadcasted_iota(I32, (HD, W2), 0)
    i1 = lax.broadcasted_iota(I32, (HD, W2), 1)
    eye_a = (i1 == i0).astype(F32)
    eye_b = (i1 == i0 + HD).astype(F32)
    lane_lo = i1 < HD
    j0 = lax.broadcasted_iota(I32, (W2, W2), 0)
    j1 = lax.broadcasted_iota(I32, (W2, W2), 1)
    ones_bd = ((j0 < HD) == (j1 < HD)).astype(F32).astype(BF16)
    s0 = lax.broadcasted_iota(I32, (SUBLANES, W2), 0)
    s1 = lax.broadcasted_iota(I32, (SUBLANES, W2), 1)
    rsel = (((s0 == 0) & (s1 < HD)) | ((s0 == 1) & (s1 >= HD))).astype(F32)
    chains = [(b, p) for b in range(bb) for p in range(NP)]

    def sub(i, carry):
        t0 = pl.multiple_of(i * SUBLANES, SUBLANES)
        S = [S_s[b, p] for b, p in chains]

        def emit_y(t, Sb):
            for ci, (b, p) in enumerate(chains):
                rm = (r_s[b, p, pl.ds(t, 1), :] * rsel).astype(BF16)
                ym = _dot_nt(rm, Sb[ci])
                y_s[b, 2 * p, pl.ds(t, 1), :] = ym[0:1, :]
                y_s[b, 2 * p + 1, pl.ds(t, 1), :] = ym[1:2, :]

        pending = None
        for j in range(SUBLANES):
            t = t0 + j
            row = lambda ref, ci: ref[chains[ci][0], chains[ci][1], pl.ds(t, 1), :]
            n = len(chains)
            sa = [_dot((S[ci] * row(a_s, ci)).astype(BF16), ones_bd) for ci in range(n)]
            if pending is not None:
                emit_y(*pending)
            for ci in range(n):
                v = row(v_s, ci)
                vc = jnp.where(lane_lo, jnp.sum(eye_a * v, axis=1, keepdims=True),
                               jnp.sum(eye_b * v, axis=1, keepdims=True))
                S[ci] = S[ci] * row(w_s, ci) + sa[ci] * row(b_s, ci) + vc * row(k_s, ci)
            pending = (t, [s.astype(BF16) for s in S])
        emit_y(*pending)
        for ci, (b, p) in enumerate(chains):
            S_s[b, p] = S[ci]
        return carry

    lax.fori_loop(0, Tc // SUBLANES, sub, 0)

    for b in range(bb):
        y = jnp.concatenate([y_s[b, h] for h in range(H_R)], axis=-1)
        mean = _dot_exactw(y, gseg) * (1.0 / HD)
        d = y - mean
        var = _dot_exactw(d * d, gseg) * (1.0 / HD)
        yn = d * lax.rsqrt(var + RWKV_LN_EPS) * lg_ref[...] + lb_ref[...]
        o_ref[b] = (yn + bon_s[b]) * g_s[b]

    @pl.when(c == pl.num_programs(1) - 1)
    def _():
        for b in range(bb):
            for p in range(NP):
                s2 = S_s[b, p]
                st_ref[b, 2 * p] = s2[:, 0:HD]
                st_ref[b, 2 * p + 1] = s2[:, HD:]


def _rwkv(z3, shift0, wkv0, P, bb, Tc):
    B, T, _ = z3.shape
    vec = lambda n: pl.BlockSpec((1, n), lambda bi, c: (0, 0))
    mat = lambda r, n: pl.BlockSpec((r, n), lambda bi, c: (0, 0))
    kern = functools.partial(_rwkv_kernel, bb=bb, Tc=Tc)
    hv = lambda: pltpu.VMEM((bb, H_R // 2, Tc, 2 * HD), F32)
    return pl.pallas_call(
        kern,
        grid=(B // bb, T // Tc),
        in_specs=[pl.BlockSpec((bb, Tc, N_RWKV_COLS), lambda bi, c: (bi, c, 0)),
                  pl.BlockSpec((bb, 1, N_RWKV_COLS), lambda bi, c: (bi, 0, 0)),
                  pl.BlockSpec((bb, H_R, HD, HD), lambda bi, c: (bi, 0, 0, 0)),
                  vec(N_RWKV_COLS), vec(C_R), mat(LANES, C_R), mat(LANES, C_R), vec(C_R), mat(LANES, C_R),
                  mat(LANES, C_R), mat(LANES, C_R), mat(LANES, C_R),
                  vec(C_R), vec(C_R), vec(C_R), vec(C_R), vec(C_R), mat(C_R, C_R)],
        out_specs=[pl.BlockSpec((bb, Tc, C_R), lambda bi, c: (bi, c, 0)),
                   pl.BlockSpec((bb, H_R, HD, HD), lambda bi, c: (bi, 0, 0, 0))],
        out_shape=[jax.ShapeDtypeStruct((B, T, C_R), F32),
                   jax.ShapeDtypeStruct((B, H_R, HD, HD), F32)],
        scratch_shapes=[pltpu.VMEM((bb, H_R // 2, HD, 2 * HD), F32),
                        pltpu.VMEM((bb, 1, N_RWKV_COLS), F32),
                        hv(), hv(), hv(), hv(), hv(), hv(),
                        pltpu.VMEM((bb, H_R, Tc, HD), F32),
                        pltpu.VMEM((bb, Tc, C_R), F32),
                        pltpu.VMEM((bb, Tc, C_R), F32)],
        compiler_params=_cparams(("parallel", "arbitrary")),
        name="rwkv7",
    )(z3, shift0, wkv0, P["mu"], P["w0"], P["wl_hi"], P["wl_lo"], P["a0"], P["al_hi"], P["al_lo"],
      P["gl_hi"], P["gl_lo"], P["k_k"], P["k_a"], P["r_k"], P["lnx_g"], P["lnx_b"], P["gseg4"])


def _conv_kernel(z_ref, c0_ref, w_ref, b_ref, lg_ref, lb_ref, o_ref, ct_ref, buf, *, Tc):
    @pl.when(pl.program_id(1) == 0)
    def _():
        buf[0:HIST, :] = c0_ref[0]

    z = z_ref[0]
    u = z[:, 0:C_CONV] * _sigmoid(z[:, C_CONV:])
    buf[HIST:HIST + Tc, :] = u
    w = w_ref[...]
    acc = jnp.zeros((Tc, C_CONV), F32) + b_ref[...]
    off = HIST - (CONV_W - 1)
    for j in range(CONV_W):
        acc = acc + w[j:j + 1, :] * buf[off + j:off + j + Tc, :]
    mu = jnp.mean(acc, axis=-1, keepdims=True)
    d = acc - mu
    var = jnp.mean(d * d, axis=-1, keepdims=True)
    cn = d * lax.rsqrt(var + CONV_LN_EPS) * lg_ref[...] + lb_ref[...]
    o_ref[0] = cn * _sigmoid(cn)
    hist = buf[Tc:Tc + HIST, :]
    buf[0:HIST, :] = hist
    ct_ref[0] = hist


def _conv(z3, conv0p, P, Tc):
    B, T, _ = z3.shape
    vec = pl.BlockSpec((1, C_CONV), lambda b, c: (0, 0))
    return pl.pallas_call(
        functools.partial(_conv_kernel, Tc=Tc),
        grid=(B, T // Tc),
        in_specs=[pl.BlockSpec((1, Tc, 2 * C_CONV), lambda b, c: (b, c, ZC_CONV)),
                  pl.BlockSpec((1, HIST, C_CONV), lambda b, c: (b, 0, 0)),
                  pl.BlockSpec((CONV_W, C_CONV), lambda b, c: (0, 0)),
                  vec, vec, vec],
        out_specs=[pl.BlockSpec((1, Tc, C_CONV), lambda b, c: (b, c, 0)),
                   pl.BlockSpec((1, HIST, C_CONV), lambda b, c: (b, 0, 0))],
        out_shape=[jax.ShapeDtypeStruct((B, T, C_CONV), F32),
                   jax.ShapeDtypeStruct((B, HIST, C_CONV), F32)],
        scratch_shapes=[pltpu.VMEM((HIST + Tc, C_CONV), F32)],
        compiler_params=_cparams(("parallel", "arbitrary")),
        name="conformer_conv",
    )(z3, conv0p, P["conv_w"], P["conv_b"], P["conv_ln_g"], P["conv_ln_b"])


def _att_prep_kernel(k_ref, kw_ref, kg_ref, kig_ref, gseg_ref, kn_ref, kin_ref):
    k = k_ref[...]
    ms = _dot_exactw(k * k, gseg_ref[...]) * (1.0 / HD)
    kn_ref[...] = k * lax.rsqrt(ms + NORM_EPS) * kg_ref[...]
    kw = kw_ref[...]
    ki = kw[:, 0:D_IDX]
    msi = jnp.mean(ki * ki, axis=-1, keepdims=True)
    kin_ref[...] = ki * lax.rsqrt(msi + NORM_EPS) * kig_ref[...]


def _att_prep(z2, P, tm):
    M = z2.shape[0]
    return pl.pallas_call(
        _att_prep_kernel,
        grid=(M // tm,),
        in_specs=[pl.BlockSpec((tm, LANES), lambda i: (i, ZC_K)),
                  pl.BlockSpec((tm, LANES), lambda i: (i, ZC_KW)),
                  pl.BlockSpec((1, LANES), lambda i: (0, 0)),
                  pl.BlockSpec((1, D_IDX), lambda i: (0, 0)),
                  pl.BlockSpec((LANES, LANES), lambda i: (0, 0))],
        out_specs=[pl.BlockSpec((tm, LANES), lambda i: (i, 0)),
                   pl.BlockSpec((tm, D_IDX), lambda i: (i, 0))],
        out_shape=[jax.ShapeDtypeStruct((M, N_KV * HD), F32),
                   jax.ShapeDtypeStruct((M, D_IDX), F32)],
        compiler_params=_cparams(("parallel",)),
        name="att_prep",
    )(z2, z2, P["k_norm2"], P["kidx_norm"], P["gseg2"])


def _bias_kernel(rb_ref, idx_ref, o_ref):
    h = pl.program_id(0) % H_A
    idx = idx_ref[0]
    out = jnp.zeros(idx.shape, F32)
    for bk in range(N_BUCKETS):
        out = jnp.where(idx == bk, rb_ref[bk, h], out)
    o_ref[0] = out


def _bias_expand(rel_bias, idx):
    S, R, L = idx.shape
    return pl.pallas_call(
        _bias_kernel,
        grid=(S,),
        in_specs=[pl.BlockSpec(memory_space=pltpu.SMEM),
                  pl.BlockSpec((1, R, L), lambda s: (s, 0, 0))],
        out_specs=pl.BlockSpec((1, R, L), lambda s: (s, 0, 0)),
        out_shape=jax.ShapeDtypeStruct((S, R, L), F32),
        compiler_params=_cparams(("parallel",)),
        name="rel_bias_tiles",
    )(rel_bias, idx)


def _rel_bucket_np(dist):
    max_exact = N_BUCKETS // 2
    d_f = np.maximum(dist, max_exact).astype(np.float32)
    large = max_exact + (np.log(d_f / np.float32(max_exact)) / np.float32(math.log(MAX_DISTANCE / max_exact))
                         * np.float32(N_BUCKETS - max_exact)).astype(np.int32)
    return np.where(dist < max_exact, dist, np.minimum(large, N_BUCKETS - 1)).astype(np.int32)


def _sort_key(s):
    bits = pltpu.bitcast(s + 0.0, I32)
    return jnp.where(bits < 0, bits ^ 0x7FFFFFFF, bits)


def _q_prep(q, qg, gseg8):
    ms = _dot_exactw(q * q, gseg8) * (1.0 / HD)
    return (q * lax.rsqrt(ms + NORM_EPS) * qg * (HD ** -0.5)).astype(BF16)


def _attp_kernel(q_ref, qi_ref, kw_ref, kn_ref, v_ref, kin_ref, qg_ref, bias_ref, gseg_ref, tri_ref, ones_ref,
                 o_ref,
                 kk_s, vv_s, ki_s, qs_s, qi_s, wb_s, key_s, hi_s, lo_s, tau_s, madd_s, s_s, m_s, l_s, acc_s,
                 *, tq, topk, kg, ag):
    j = pl.program_id(1)
    nh = tq // LANES
    hp = m_s.shape[0] // tq
    nb = (j + 1) * nh
    KW = kg * LANES
    AW = ag * LANES
    ngk = (nb + kg - 1) // kg
    nga = (nb + ag - 1) // ag
    R = hp * tq

    @pl.when(j == 0)
    def _():
        knT = jnp.transpose(kn_ref[0])
        v = v_ref[0]
        for n in range(N_KV):
            kk_s[n] = knT[n * HD:(n + 1) * HD, :].astype(BF16)
            vv_s[n] = v[:, n * HD:(n + 1) * HD].astype(BF16)
        kin = kin_ref[0]
        kinT = jnp.transpose(jnp.concatenate([kin, jnp.zeros(kin.shape, F32)], axis=1))
        ki_s[...] = kinT[0:D_IDX, :].astype(BF16)

    qn = _q_prep(q_ref[0], qg_ref[...], gseg_ref[...])
    for n in range(N_KV):
        qs_s[n] = jnp.concatenate([qn[:, (n * G_A + g) * HD:(n * G_A + g + 1) * HD] for g in range(G_A)], axis=0)
    qi = (qi_ref[0] * (D_IDX ** -0.5)).astype(BF16)
    kw = kw_ref[0]
    for h in range(H_I):
        qi_s[h] = qi[:, h * D_IDX:(h + 1) * D_IDX]
        wb_s[h] = jnp.broadcast_to(kw[:, D_IDX + h:D_IDX + h + 1] * (H_I ** -0.5), (tq, KW))

    row = lax.broadcasted_iota(I32, (tq, KW), 0) + j * tq
    col0 = lax.broadcasted_iota(I32, (tq, KW), 1)

    def idx_body(gi, carry):
        off = pl.multiple_of(gi * KW, KW)
        kc = ki_s[:, pl.ds(off, KW)]
        s = jnp.zeros((tq, KW), F32)
        for h in range(H_I):
            s = s + jnp.maximum(_dot(qi_s[h], kc), 0.0) * wb_s[h]
        s = jnp.where(col0 + off <= row, s, -jnp.inf)
        key = _sort_key(s)
        key_s[:, pl.ds(off, KW)] = key
        hi_s[:, pl.ds(off, KW)] = lax.shift_right_arithmetic(key, 16).astype(I16)
        lo_s[:, pl.ds(off, KW)] = ((key & 0xFFFF) + I16_MIN).astype(I16)
        return carry

    lax.fori_loop(0, ngk, idx_body, 0)

    def key_tiles(gi, ref=key_s):
        off = pl.multiple_of(gi * KW, KW)
        key = ref[:, pl.ds(off, KW)]
        return [key[:, u * LANES:(u + 1) * LANES] for u in range(kg)]

    def lane_total(acc):
        return jnp.broadcast_to(jnp.sum(acc.astype(F32), axis=1, keepdims=True), (tq, LANES))

    one16 = jnp.ones((tq, LANES), I16)
    zero16 = jnp.zeros((tq, LANES), I16)

    def search16(ref, kneed):
        for ng in range(1, key_s.shape[1] // KW + 1):
            @pl.when(ngk == ng)
            def _(ng=ng):
                def bit_body(it, cur):
                    cand = cur + lax.shift_left(jnp.int32(1), jnp.asarray(15 - it, dtype=I32))
                    c16 = cand.astype(I16)
                    accs = [zero16] * kg
                    for gi in range(ng):
                        for u in range(kg):
                            kt = ref[:, gi * KW + u * LANES:gi * KW + (u + 1) * LANES]
                            accs[u] = accs[u] + jnp.where(kt >= c16, one16, zero16)
                    acc = accs[0]
                    for a in accs[1:]:
                        acc = acc + a
                    return jnp.where(lane_total(acc) >= kneed, cand, cur)

                tau_s[...] = lax.fori_loop(0, 16, bit_body, jnp.full((tq, LANES), I16_MIN, I32))

        return tau_s[...]

    tau_hi = search16(hi_s, float(topk))
    th16 = tau_hi.astype(I16)
    lo_pad = jnp.full((tq, LANES), I16_MIN, I16)

    def mid_body(gi, acc):
        off = pl.multiple_of(gi * KW, KW)
        los = key_tiles(gi, lo_s)
        for u, ht in enumerate(key_tiles(gi, hi_s)):
            acc = acc + jnp.where(ht > th16, one16, zero16)
            lo_s[:, pl.ds(off + u * LANES, LANES)] = jnp.where(ht == th16, los[u], lo_pad)
        return acc

    need_lo = float(topk) - lane_total(lax.fori_loop(0, ngk, mid_body, zero16))
    tau_lo = search16(lo_s, need_lo)
    tau = lax.shift_left(tau_hi, 16) + (tau_lo - I16_MIN)

    def cnt2_body(gi, carry):
        cge, cgt = carry
        for kt in key_tiles(gi):
            cge = cge + jnp.where(kt >= tau, 1.0, 0.0)
            cgt = cgt + jnp.where(kt > tau, 1.0, 0.0)
        return cge, cgt

    zero = jnp.zeros((tq, LANES), F32)
    cge, cgt = lax.fori_loop(0, ngk, cnt2_body, (zero, zero))
    need = float(topk) - lane_total(cgt)
    ties = jnp.max(lane_total(cge)) > float(topk)

    @pl.when(jnp.logical_not(ties))
    def _():
        def body(gi, carry):
            off = pl.multiple_of(gi * KW, KW)
            for u, kt in enumerate(key_tiles(gi)):
                take = jnp.where(kt >= tau, jnp.where(kt > KEY_NEG_INF, 0.0, NEG), NEG)
                madd_s[:, pl.ds(off + u * LANES, LANES)] = take
            return carry

        lax.fori_loop(0, ngk, body, 0)

    @pl.when(ties)
    def _():
        tri = tri_ref[...]
        ones = ones_ref[...]

        def body(c, run):
            off = pl.multiple_of(c * LANES, LANES)
            key = key_s[:, pl.ds(off, LANES)]
            eq = jnp.where(key == tau, 1.0, 0.0)
            eqb = eq.astype(BF16)
            pre = _dot(eqb, tri) + run
            take = jnp.where(key > tau, 1.0, jnp.where(pre <= need, eq, 0.0))
            take = jnp.where(key > KEY_NEG_INF, take, 0.0)
            madd_s[:, pl.ds(off, LANES)] = jnp.where(take > 0.5, 0.0, NEG)
            return run + _dot(eqb, ones)

        lax.fori_loop(0, ngk * kg, body, jnp.zeros((tq, LANES), F32))

    slab_list = [(gl, hf) for gl in range(hp) for hf in range(nh)]
    outs = []
    for n in range(N_KV):
      for gp in range(G_A // hp):
        m_s[...] = jnp.full((R, LANES), NEG, F32)
        l_s[...] = jnp.zeros((R, LANES), F32)
        acc_s[...] = jnp.zeros((R, HD), F32)

        def pass_a(gi, carry, n=n, gp=gp):
            off = pl.multiple_of(gi * AW, AW)
            kc = kk_s[n, :, pl.ds(off, AW)]
            s_all = _dot(qs_s[n, gp * R:(gp + 1) * R, :], kc)
            for gl, hf in slab_list:
                r0 = gl * tq + hf * LANES
                rows = pl.ds(r0, LANES)
                head = n * G_A + gp * hp + gl
                mx = m_s[rows, :]
                for u in range(ag):
                    cols = pl.ds(off + u * LANES, LANES)
                    dd = jnp.clip(j * nh + hf - (gi * ag + u), 0, 2)
                    su = (s_all[r0:r0 + LANES, u * LANES:(u + 1) * LANES] + bias_ref[dd, head]
                          + madd_s[hf * LANES:(hf + 1) * LANES, cols])
                    s_s[rows, cols] = su
                    mx = jnp.maximum(mx, su)
                m_s[rows, :] = mx
            return carry

        lax.fori_loop(0, nga, pass_a, 0)
        mb = jnp.broadcast_to(jnp.max(m_s[...], axis=1, keepdims=True), (R, LANES))
        m_s[...] = mb

        def pass_b(gi, carry, n=n):
            off = pl.multiple_of(gi * AW, AW)
            slabs = []
            for gl, hf in slab_list:
                rows = pl.ds(gl * tq + hf * LANES, LANES)
                mrow = m_s[rows, :]
                lacc = l_s[rows, :]
                ps = []
                for u in range(ag):
                    p = jnp.exp(s_s[rows, pl.ds(off + u * LANES, LANES)] - mrow)
                    lacc = lacc + p
                    ps.append(p.astype(BF16))
                l_s[rows, :] = lacc
                slabs.append(ps[0] if ag == 1 else jnp.concatenate(ps, axis=1))
            acc_s[...] += _dot(jnp.concatenate(slabs, axis=0), vv_s[n, pl.ds(off, AW), :])
            return carry

        lax.fori_loop(0, nga, pass_b, 0)
        o = acc_s[...] / jnp.sum(l_s[...], axis=1, keepdims=True)
        for gl in range(hp):
            outs.append(o[gl * tq:(gl + 1) * tq, :])
    o_ref[0] = jnp.concatenate(outs, axis=-1)


def _att_prompt(z3, kn3, kin3, P, bias_p, topk, tq):
    B, T, _ = z3.shape
    nblk = T // LANES
    kg = 4 if nblk % 4 == 0 else (2 if nblk % 2 == 0 else 1)
    ag = kg
    rp = min(G_A * tq, ATT_ROWS)
    cst = lambda shp: pl.BlockSpec(shp, lambda b, j: tuple(0 for _ in shp))
    return pl.pallas_call(
        functools.partial(_attp_kernel, tq=tq, topk=topk, kg=kg, ag=ag),
        grid=(B, T // tq),
        in_specs=[pl.BlockSpec((1, tq, C_A), lambda b, j: (b, j, ZC_Q)),
                  pl.BlockSpec((1, tq, H_I * D_IDX), lambda b, j: (b, j, ZC_QI)),
                  pl.BlockSpec((1, tq, LANES), lambda b, j: (b, j, ZC_KW)),
                  pl.BlockSpec((1, T, N_KV * HD), lambda b, j: (b, 0, 0)),
                  pl.BlockSpec((1, T, N_KV * HD), lambda b, j: (b, 0, ZC_V)),
                  pl.BlockSpec((1, T, D_IDX), lambda b, j: (b, 0, 0)),
                  cst((1, C_A)), cst((3, H_A, LANES, LANES)), cst((C_A, C_A)), cst((LANES, LANES)),
                  cst((LANES, LANES))],
        out_specs=pl.BlockSpec((1, tq, C_A), lambda b, j: (b, j, 0)),
        out_shape=jax.ShapeDtypeStruct((B, T, C_A), F32),
        scratch_shapes=[pltpu.VMEM((N_KV, HD, T), BF16),
                        pltpu.VMEM((N_KV, T, HD), BF16),
                        pltpu.VMEM((D_IDX, T), BF16),
                        pltpu.VMEM((N_KV, G_A * tq, HD), BF16),
                        pltpu.VMEM((H_I, tq, D_IDX), BF16),
                        pltpu.VMEM((H_I, tq, kg * LANES), F32),
                        pltpu.VMEM((tq, T), I32),
                        pltpu.VMEM((tq, T), I16),
                        pltpu.VMEM((tq, T), I16),
                        pltpu.VMEM((tq, LANES), I32),
                        pltpu.VMEM((tq, T), F32),
                        pltpu.VMEM((rp, T), F32),
                        pltpu.VMEM((rp, LANES), F32),
                        pltpu.VMEM((rp, LANES), F32),
                        pltpu.VMEM((rp, HD), F32)],
        compiler_params=_cparams(("parallel", "arbitrary")),
        name="dsa_prompt",
    )(z3, z3, z3, kn3, z3, kin3, P["q_norm8"], bias_p, P["gseg8"], P["tri"], P["ones"])


def _pad_transpose(x):
    r, c = x.shape
    if c < LANES:
        x = jnp.concatenate([x, jnp.zeros((r, LANES - c), F32)], axis=1)
    x = jnp.concatenate([x, jnp.zeros((LANES - r, LANES), F32)], axis=0)
    return jnp.transpose(x)


def _atts_kernel(pt_ref, ckT_ref, cvT_ref, ckiT_ref, q_ref, qi_ref, kw_ref, kn_ref, vn_ref, kin_ref, qg_ref,
                 bias_ref, gseg_ref, tri_ref, ones_ref,
                 o_ref,
                 kT_s, vT_s, kiT_s, madd_s, sem, *, layer, npages, nbatch, Ts, topk):
    b = pl.program_id(0)
    slot = b % 2
    P = LANES
    L = (npages + 1) * P

    def page_copies(seq, sl, p):
        pg = pt_ref[seq, p]
        col = pl.multiple_of(p * P, P)
        return (pltpu.make_async_copy(ckT_ref.at[layer, pg], kT_s.at[sl, :, :, pl.ds(col, P)], sem.at[sl, 0]),
                pltpu.make_async_copy(cvT_ref.at[layer, pg], vT_s.at[sl, :, :, pl.ds(col, P)], sem.at[sl, 1]),
                pltpu.make_async_copy(ckiT_ref.at[layer, pg], kiT_s.at[sl, :, pl.ds(col, P)], sem.at[sl, 2]))

    def start_pages(seq, sl):
        def body(p, carry):
            for cp in page_copies(seq, sl, p):
                cp.start()
            return carry
        lax.fori_loop(0, npages, body, 0)

    def wait_pages(seq, sl):
        def body(p, carry):
            for cp in page_copies(seq, sl, p):
                cp.wait()
            return carry
        lax.fori_loop(0, npages, body, 0)

    @pl.when(b == 0)
    def _():
        start_pages(0, 0)

    @pl.when(b + 1 < nbatch)
    def _():
        start_pages(b + 1, 1 - slot)

    knT = _pad_transpose(kn_ref[0])
    vnT = _pad_transpose(vn_ref[0])
    for n in range(N_KV):
        kT_s[slot, n, :, npages * P:L] = knT[n * HD:(n + 1) * HD, :]
        vT_s[slot, n, :, npages * P:L] = vnT[n * HD:(n + 1) * HD, :]
    kiT_s[slot, :, npages * P:L] = _pad_transpose(kin_ref[0])[0:D_IDX, :]

    qn = _q_prep(q_ref[0], qg_ref[...], gseg_ref[...])
    qi = (qi_ref[0] * (D_IDX ** -0.5)).astype(BF16)
    qis = jnp.concatenate([qi[:, h * D_IDX:(h + 1) * D_IDX] for h in range(H_I)], axis=0)
    kw = kw_ref[0]

    wait_pages(b, slot)

    d = _dot(qis, kiT_s[slot].astype(BF16))
    s = jnp.zeros((Ts, L), F32)
    for h in range(H_I):
        s = s + jnp.maximum(d[h * Ts:(h + 1) * Ts, :], 0.0) * (kw[:, D_IDX + h:D_IDX + h + 1] * (H_I ** -0.5))
    col = lax.broadcasted_iota(I32, (Ts, L), 1)
    row = lax.broadcasted_iota(I32, (Ts, L), 0) + npages * P
    key = _sort_key(jnp.where(col <= row, s, -jnp.inf))

    def bit_body(it, cur):
        step = lax.shift_left(jnp.int32(1), jnp.asarray(30 - 2 * it, dtype=I32))
        for mult in (1, 2, 3):
            cand = cur + mult * step
            cnt = jnp.sum(jnp.where(key >= cand, 1.0, 0.0), axis=1, keepdims=True)
            new = jnp.where(cnt >= float(topk), cand, cur) if mult == 1 else jnp.where(cnt >= float(topk), cand, new)
        return new

    tau = lax.fori_loop(0, 16, bit_body, jnp.full((Ts, 1), INT_MIN, I32))
    need = float(topk) - jnp.sum(jnp.where(key > tau, 1.0, 0.0), axis=1, keepdims=True)
    cge = jnp.sum(jnp.where(key >= tau, 1.0, 0.0), axis=1, keepdims=True)
    ties = jnp.max(cge) > float(topk)

    @pl.when(jnp.logical_not(ties))
    def _():
        madd_s[...] = jnp.where(key >= tau, jnp.where(key > KEY_NEG_INF, 0.0, NEG), NEG)

    @pl.when(ties)
    def _():
        tri = tri_ref[...]
        ones = ones_ref[...]
        eq = jnp.where(key == tau, 1.0, 0.0)
        run = jnp.zeros((Ts, LANES), F32)
        for c in range(L // LANES):
            sl = slice(c * LANES, (c + 1) * LANES)
            eqc = eq[:, sl]
            eqb = eqc.astype(BF16)
            pre = _dot(eqb, tri) + run
            take = jnp.where(key[:, sl] > tau, 1.0, jnp.where(pre <= need, eqc, 0.0))
            take = jnp.where(key[:, sl] > KEY_NEG_INF, take, 0.0)
            madd_s[:, sl] = jnp.where(take > 0.5, 0.0, NEG)
            run = run + _dot(eqb, ones)

    madd = madd_s[...]
    madd4 = jnp.concatenate([madd] * G_A, axis=0)
    outs = []
    for n in range(N_KV):
        qs = jnp.concatenate([qn[:, (n * G_A + g) * HD:(n * G_A + g + 1) * HD] for g in range(G_A)], axis=0)
        sc = _dot(qs, kT_s[slot, n].astype(BF16)) + bias_ref[n] + madd4
        m = jnp.max(sc, axis=1, keepdims=True)
        pe = jnp.exp(sc - m)
        l = jnp.sum(pe, axis=1, keepdims=True)
        o = _dot_nt(pe.astype(BF16), vT_s[slot, n].astype(BF16)) / l
        for g in range(G_A):
            outs.append(o[g * Ts:(g + 1) * Ts, :])
    o_ref[0] = jnp.concatenate(outs, axis=-1)


def _att_sample(layer, page_table, ckT, cvT, ckiT, z3, kn3, kin3, P, bias_s, topk):
    B, Ts, _ = z3.shape
    npages = page_table.shape[1]
    L = (npages + 1) * LANES
    hbm = pl.BlockSpec(memory_space=pl.ANY)
    cst = lambda shp: pl.BlockSpec(shp, lambda b, pt: tuple(0 for _ in shp))
    grid_spec = pltpu.PrefetchScalarGridSpec(
        num_scalar_prefetch=1,
        grid=(B,),
        in_specs=[hbm, hbm, hbm,
                  pl.BlockSpec((1, Ts, C_A), lambda b, pt: (b, 0, ZC_Q)),
                  pl.BlockSpec((1, Ts, H_I * D_IDX), lambda b, pt: (b, 0, ZC_QI)),
                  pl.BlockSpec((1, Ts, LANES), lambda b, pt: (b, 0, ZC_KW)),
                  pl.BlockSpec((1, Ts, N_KV * HD), lambda b, pt: (b, 0, 0)),
                  pl.BlockSpec((1, Ts, N_KV * HD), lambda b, pt: (b, 0, ZC_V)),
                  pl.BlockSpec((1, Ts, D_IDX), lambda b, pt: (b, 0, 0)),
                  cst((1, C_A)), cst((N_KV, G_A * Ts, L)), cst((C_A, C_A)), cst((LANES, LANES)),
                  cst((LANES, LANES))],
        out_specs=pl.BlockSpec((1, Ts, C_A), lambda b, pt: (b, 0, 0)),
        scratch_shapes=[pltpu.VMEM((2, N_KV, HD, L), F32),
                        pltpu.VMEM((2, N_KV, HD, L), F32),
                        pltpu.VMEM((2, D_IDX, L), F32),
                        pltpu.VMEM((Ts, L), F32),
                        pltpu.SemaphoreType.DMA((2, 3))])
    return pl.pallas_call(
        functools.partial(_atts_kernel, layer=layer, npages=npages, nbatch=B, Ts=Ts, topk=topk),
        grid_spec=grid_spec,
        out_shape=jax.ShapeDtypeStruct((B, Ts, C_A), F32),
        compiler_params=_cparams(("arbitrary",)),
        name="dsa_sample",
    )(page_table, ckT, cvT, ckiT, z3, z3, z3, kn3, z3, kin3, P["q_norm8"], bias_s, P["gseg8"], P["tri"], P["ones"])


def _out_proj_kernel(x_ref, orw_ref, ocv_ref, oat_ref, w_ref, y_ref):
    acc = _dot(orw_ref[...].astype(BF16), w_ref[0:C_R, :])
    acc = acc + _dot(ocv_ref[...].astype(BF16), w_ref[C_R:C_R + C_CONV, :])
    acc = acc + _dot(oat_ref[...].astype(BF16), w_ref[C_R + C_CONV:, :])
    y_ref[...] = x_ref[...] + acc


def _out_proj(x2, orw, ocv, oat, w_bf, tm):
    M, D = x2.shape
    row = lambda n: pl.BlockSpec((tm, n), lambda i: (i, 0))
    return pl.pallas_call(
        _out_proj_kernel,
        grid=(M // tm,),
        in_specs=[row(D), row(C_R), row(C_CONV), row(C_A), pl.BlockSpec((D, D), lambda i: (0, 0))],
        out_specs=row(D),
        out_shape=jax.ShapeDtypeStruct((M, D), F32),
        compiler_params=_cparams(("parallel",)),
        name="out_proj",
    )(x2, orw, ocv, oat, w_bf)


def _gelu(x):
    return 0.5 * x * (1.0 + lax.erf(x * (2.0 ** -0.5)))


def _ffn_kernel(x_ref, p_ref, ng_ref, upg_ref, upv_ref, cw_ref, cb_ref, dn_ref, h_ref, h2_ref, pg_ref, pgate_ref,
                pproj_ref,
                xo_ref, tail_ref,
                xn_s, acc_s, hist_s, *, tm, nF, seq_tiles, seq_len):
    i = pl.program_id(0)
    f = pl.program_id(1)

    @pl.when(f == 0)
    def _():
        xn_s[...] = _rms(x_ref[...], ng_ref[...]).astype(BF16)
        acc_s[...] = jnp.zeros(acc_s.shape, F32)

    xn = xn_s[...]
    g = _dot(xn, upg_ref[...])
    val = _dot(xn, upv_ref[...])
    row = lax.broadcasted_iota(I32, g.shape, 0)
    if seq_tiles is not None:
        first = (i % seq_tiles) == 0
        hs = hist_s[f]
        h0 = h_ref[0]
        hm2 = jnp.where(first, h0[0:1, :], hs[SUBLANES - 2:SUBLANES - 1, :])
        hm1 = jnp.where(first, h0[1:2, :], hs[SUBLANES - 1:SUBLANES, :])
        g1 = jnp.where(row == 0, hm1, pltpu.roll(g, 1, 0))
        g2 = jnp.where(row == 0, hm2, jnp.where(row == 1, hm1, pltpu.roll(g, 2, 0)))
        slab = g[tm - SUBLANES:tm, :]
        hist_s[f] = slab
        tail_ref[0] = slab
    else:
        t = row % seq_len
        g1 = jnp.where(t == 0, 0.0, pltpu.roll(g, 1, 0)) + h_ref[...]
        g2 = jnp.where(t < 2, 0.0, pltpu.roll(g, 2, 0)) + h2_ref[...]
        tail_ref[...] = g
    cw = cw_ref[...]
    gate = cw[0:1, :] * g2 + cw[1:2, :] * g1 + cw[2:3, :] * g + cb_ref[...]
    act = (_gelu(gate) * val).astype(BF16)
    acc_s[...] += _dot(act, dn_ref[...])

    @pl.when(f == nF - 1)
    def _():
        x2 = x_ref[...] + acc_s[...]
        xn2 = _rms(x2, pg_ref[...]).astype(BF16)
        gate2 = _sigmoid(_dot(xn2, pgate_ref[...]))
        xo_ref[...] = x2 + _dot(p_ref[...].astype(BF16), pproj_ref[...]) * gate2


def _ffn(x2, p2, ffn0, P, tm, B, T):
    M, D = x2.shape
    tf = 256
    nF = D_FF // tf
    DP = p2.shape[1]
    whole = tm % T == 0
    if whole:
        seq_tiles, seq_len = None, T
        h1 = jnp.pad(ffn0[:, 1:2, :], ((0, 0), (0, T - 1), (0, 0))).reshape(M, D_FF)
        h2 = jnp.pad(ffn0, ((0, 0), (0, T - 2), (0, 0))).reshape(M, D_FF)
        h_specs = [pl.BlockSpec((tm, tf), lambda i, f: (i, f)), pl.BlockSpec((tm, tf), lambda i, f: (i, f))]
        tail_spec = pl.BlockSpec((tm, tf), lambda i, f: (i, f))
        tail_shape = jax.ShapeDtypeStruct((M, D_FF), F32)
    else:
        seq_tiles, seq_len = T // tm, T
        h1, h2 = ffn0, ffn0
        h_specs = [pl.BlockSpec((1, 2, tf), lambda i, f: (i // seq_tiles, 0, f)),
                   pl.BlockSpec((1, 2, tf), lambda i, f: (i // seq_tiles, 0, f))]
        tail_spec = pl.BlockSpec((1, SUBLANES, tf), lambda i, f: (i, 0, f))
        tail_shape = jax.ShapeDtypeStruct((M // tm, SUBLANES, D_FF), F32)
    cst = lambda shp: pl.BlockSpec(shp, lambda i, f: tuple(0 for _ in shp))
    xo, tail = pl.pallas_call(
        functools.partial(_ffn_kernel, tm=tm, nF=nF, seq_tiles=seq_tiles, seq_len=seq_len),
        grid=(M // tm, nF),
        in_specs=[pl.BlockSpec((tm, D), lambda i, f: (i, 0)),
                  pl.BlockSpec((tm, DP), lambda i, f: (i, 0)),
                  cst((1, D)),
                  pl.BlockSpec((D, tf), lambda i, f: (0, f)),
                  pl.BlockSpec((D, tf), lambda i, f: (0, nF + f)),
                  pl.BlockSpec((3, tf), lambda i, f: (0, f)),
                  pl.BlockSpec((1, tf), lambda i, f: (0, f)),
                  pl.BlockSpec((tf, D), lambda i, f: (f, 0))] + h_specs +
                 [cst((1, D)), cst((D, D)), cst((DP, D))],
        out_specs=[pl.BlockSpec((tm, D), lambda i, f: (i, 0)), tail_spec],
        out_shape=[jax.ShapeDtypeStruct((M, D), F32), tail_shape],
        scratch_shapes=[pltpu.VMEM((tm, D), BF16), pltpu.VMEM((tm, D), F32), pltpu.VMEM((nF, SUBLANES, tf), F32)],
        compiler_params=_cparams(("parallel", "arbitrary")),
        name="conv_ffn_ple",
    )(x2, p2, P["norm_ffn"], P["ffn_up"], P["ffn_up"], P["ffn_conv_w"], P["ffn_conv_b"], P["ffn_down"], h1, h2,
      P["ple_norm"], P["ple_gate"], P["ple_proj"])
    if whole:
        ffn_T = tail.reshape(B, T, D_FF)[:, T - 2:, :]
    else:
        ffn_T = tail.reshape(B, seq_tiles, SUBLANES, D_FF)[:, seq_tiles - 1, SUBLANES - 2:, :]
    return xo, ffn_T


def _seg_ones(n, seg):
    i = np.arange(n)
    return jnp.asarray((i[:, None] // seg) == (i[None, :] // seg), dtype=BF16)


def _hl(w):
    hi = w.astype(BF16)
    return hi, (w - hi.astype(F32)).astype(BF16)


def _layer_params(i, W):
    P = {}
    w_in = W["w_in"][i]
    D = w_in.shape[0]
    P["w_in"] = jnp.concatenate([w_in[:, :N_RWKV_COLS], jnp.zeros((D, 1024 - N_RWKV_COLS), F32),
                                 w_in[:, N_RWKV_COLS:], jnp.zeros((D, NZ - 128 - N_IN), F32)], axis=1).astype(BF16)
    row = lambda v: v.reshape(1, -1)
    P["norm_mix"] = row(W["norm_mix"][i])
    P["mu"] = row(W["mu_shift"][i])
    P["w0"] = row(W["w0"][i])
    P["a0"] = row(W["a0"][i])
    z = lambda r: jnp.zeros((r, C_R), F32)
    P["wl_hi"], P["wl_lo"] = _hl(jnp.concatenate([W["w_lora"][i], z(96)], axis=0))
    P["al_hi"], P["al_lo"] = _hl(jnp.concatenate([z(32), W["a_lora"][i], z(64)], axis=0))
    P["gl_hi"], P["gl_lo"] = _hl(jnp.concatenate([z(64), W["g_lora"][i]], axis=0))
    P["k_k"] = row(W["k_k"][i])
    P["k_a"] = row(W["k_a"][i])
    P["r_k"] = row(W["r_k"][i])
    P["lnx_g"] = row(W["lnx_g"][i])
    P["lnx_b"] = row(W["lnx_b"][i])
    P["conv_w"] = W["conv_w"][i]
    P["conv_b"] = row(W["conv_b"][i])
    P["conv_ln_g"] = row(W["conv_ln_g"][i])
    P["conv_ln_b"] = row(W["conv_ln_b"][i])
    P["q_norm8"] = row(jnp.tile(W["q_norm"][i], H_A))
    P["k_norm2"] = row(jnp.tile(W["k_norm"][i], N_KV))
    P["kidx_norm"] = row(W["kidx_norm"][i])
    P["w_out"] = W["w_out"][i].astype(BF16)
    P["norm_ffn"] = row(W["norm_ffn"][i])
    P["ffn_up"] = W["ffn_up"][i].astype(BF16)
    P["ffn_conv_w"] = W["ffn_conv_w"][i]
    P["ffn_conv_b"] = row(W["ffn_conv_b"][i])
    P["ffn_down"] = W["ffn_down"][i].astype(BF16)
    P["ple_norm"] = row(W["ple_norm"][i])
    P["ple_gate"] = W["ple_gate"][i].astype(BF16)
    P["ple_proj"] = W["ple_proj"][i].astype(BF16)
    P["gseg4"] = _seg_ones(C_R, HD)
    P["gseg2"] = _seg_ones(N_KV * HD, HD)
    P["gseg8"] = _seg_ones(C_A, HD)
    P["tri"] = jnp.asarray(np.arange(LANES)[:, None] <= np.arange(LANES)[None, :], dtype=BF16)
    P["ones"] = jnp.ones((LANES, LANES), BF16)
    return P


def _pick_tile(n, pref):
    t = min(n, pref)
    while n % t:
        t //= 2
    return t


def _layer(layer, x, p_i, st, P, att_fn, bb):
    B, T, D = x.shape
    M = B * T
    shift0, wkv0, conv0, ffn0 = st
    x2 = x.reshape(M, D)
    tm = _pick_tile(M, 512)
    z2 = _in_proj(x2, P["norm_mix"], P["w_in"], tm)
    z3 = z2.reshape(B, T, NZ)
    Tc = _pick_tile(T, 256)
    o_rw, wkv_T = _rwkv(z3, shift0.reshape(B, 1, N_RWKV_COLS), wkv0, P, bb, Tc)
    conv0p = jnp.pad(conv0, ((0, 0), (HIST - (CONV_W - 1), 0), (0, 0)))
    o_cv, conv_Tp = _conv(z3, conv0p, P, _pick_tile(T, 512))
    kn2, kin2 = _att_prep(z2, P, tm)
    kn3 = kn2.reshape(B, T, N_KV * HD)
    kin3 = kin2.reshape(B, T, D_IDX)
    o_at = att_fn(layer, z3, kn3, kin3, P)
    xa = _out_proj(x2, o_rw.reshape(M, C_R), o_cv.reshape(M, C_CONV), o_at.reshape(M, C_A), P["w_out"], tm)
    tmf = M if M <= 1024 else _pick_tile(T, 1024)
    xo, ffn_T = _ffn(xa, p_i.reshape(M, -1), ffn0, P, tmf, B, T)
    k_new = kn3.reshape(B, T, N_KV, HD)
    v_new = z3[:, :, ZC_V * LANES:(ZC_V + 1) * LANES].reshape(B, T, N_KV, HD)
    shift_T = z3[:, T - 1, :N_RWKV_COLS]
    conv_T = conv_Tp[:, HIST - (CONV_W - 1):, :]
    return xo.reshape(B, T, D), (k_new, v_new, kin3, wkv_T, shift_T, conv_T, ffn_T)


def kernel(x_prompt, x_sample, cache_k, cache_v, cache_kidx, state_wkv, state_shift, state_conv, state_ffn, page_table, p_prompt, p_sample, norm_mix, w_in, w_out, mu_shift, w0, w_lora, a0, a_lora, g_lora, k_k, k_a, r_k, lnx_g, lnx_b, conv_w, conv_b, conv_ln_g, conv_ln_b, q_norm, k_norm, kidx_norm, rel_bias, norm_ffn, ffn_up, ffn_conv_w, ffn_conv_b, ffn_down, ple_norm, ple_proj, ple_gate):
    W = dict(norm_mix=norm_mix, w_in=w_in, w_out=w_out, mu_shift=mu_shift, w0=w0, w_lora=w_lora, a0=a0,
             a_lora=a_lora, g_lora=g_lora, k_k=k_k, k_a=k_a, r_k=r_k, lnx_g=lnx_g, lnx_b=lnx_b,
             conv_w=conv_w, conv_b=conv_b, conv_ln_g=conv_ln_g, conv_ln_b=conv_ln_b, q_norm=q_norm,
             k_norm=k_norm, kidx_norm=kidx_norm, norm_ffn=norm_ffn, ffn_up=ffn_up,
             ffn_conv_w=ffn_conv_w, ffn_conv_b=ffn_conv_b, ffn_down=ffn_down, ple_norm=ple_norm,
             ple_proj=ple_proj, ple_gate=ple_gate)
    depth = w_in.shape[0]
    params = [_layer_params(i, W) for i in range(depth)]

    B, T, D = x_prompt.shape
    tq = ATT_TQ if T % ATT_TQ == 0 else LANES
    topk_p = min(TOPK_MAX, T // 4)
    t_ = np.arange(LANES)[:, None]
    s_ = np.arange(LANES)[None, :]
    tiles = np.stack([_rel_bucket_np(np.maximum(t_ - s_, 0)), _rel_bucket_np(LANES + t_ - s_),
                      np.full((LANES, LANES), N_BUCKETS - 1, np.int32)])
    idx_p = np.broadcast_to(tiles[:, None], (3, H_A, LANES, LANES)).reshape(3 * H_A, LANES, LANES)
    bias_p = _bias_expand(rel_bias, jnp.asarray(idx_p)).reshape(3, H_A, LANES, LANES)

    def att_p(layer, z3, kn3, kin3, P):
        return _att_prompt(z3, kn3, kin3, P, bias_p, topk_p, tq)

    st0 = (jnp.zeros((B, N_RWKV_COLS), F32), jnp.zeros((B, H_R, HD, HD), F32),
           jnp.zeros((B, CONV_W - 1, C_CONV), F32), jnp.zeros((B, 2, D_FF), F32))
    x = x_prompt
    new_p = []
    for i in range(depth):
        x, new = _layer(i, x, p_prompt[i], st0, params[i], att_p, B)
        new_p.append(new)
    y_prompt = x

    Bs, Ts, _ = x_sample.shape
    npages = page_table.shape[1]
    psz = cache_k.shape[2]
    past_len = npages * psz
    topk_s = min(TOPK_MAX, (past_len + Ts) // 4)
    Ls = past_len + psz
    qpos = past_len + np.arange(Ts)[:, None]
    bk = _rel_bucket_np(np.maximum(qpos - np.arange(Ls)[None, :], 0))
    idx_s = np.broadcast_to(bk[None], (H_A, Ts, Ls))
    bias_s = _bias_expand(rel_bias, jnp.asarray(idx_s)).reshape(N_KV, G_A * Ts, Ls)
    ckT = jnp.transpose(cache_k, (0, 1, 3, 4, 2))
    cvT = jnp.transpose(cache_v, (0, 1, 3, 4, 2))
    ckiT = jnp.transpose(cache_kidx, (0, 1, 3, 2))

    def att_s(layer, z3, kn3, kin3, P):
        return _att_sample(layer, page_table, ckT, cvT, ckiT, z3, kn3, kin3, P, bias_s, topk_s)

    x = x_sample
    new_s = []
    for i in range(depth):
        st = (state_shift[i], state_wkv[i], state_conv[i], state_ffn[i])
        x, new = _layer(i, x, p_sample[i], st, params[i], att_s, 4)
        new_s.append(new)
    y_sample = x

    stack = lambda lst: tuple(jnp.stack(t) for t in zip(*lst))
    return (y_prompt, y_sample) + stack(new_p) + stack(new_s)
```

```python
import functools
import math

import numpy as np
import jax
import jax.numpy as jnp
from jax import lax
from jax.experimental import pallas as pl
from jax.experimental.pallas import tpu as pltpu

F32 = jnp.float32
BF16 = jnp.bfloat16
I32 = jnp.int32
I16 = jnp.int16

HD = 64
C_R = 256
H_R = C_R // HD
C_CONV = 256
CONV_W = 31
C_A = 512
H_A = C_A // HD
N_KV = 2
G_A = H_A // N_KV
H_I = 4
D_IDX = 64
TOPK_MAX = 256
N_BUCKETS = 32
MAX_DISTANCE = 128
D_FF = 2816
NORM_EPS = 1e-6
RWKV_LN_EPS = 64e-5
CONV_LN_EPS = 1e-5
N_RWKV_COLS = 896
N_IN = 2500

LANES = 128
SUBLANES = 8
VMEM_LIMIT = 56 * 1024 * 1024

NZ = 2688
ZC_CONV = 1024 // 512
ZC_Q = 1536 // 512
ZC_K = 2048 // 128
ZC_V = 2176 // 128
ZC_QI = 2304 // 256
ZC_KW = 2560 // 128

HIST = 32
ATT_ROWS = 1024
ATT_TQ = 256
NEG = -1e30
INT_MIN = -2 ** 31
I16_MIN = -2 ** 15
KEY_NEG_INF = int(np.array(0xFF800000 ^ 0x7FFFFFFF, dtype=np.uint32).view(np.int32))


def _cparams(sem):
    return pltpu.CompilerParams(dimension_semantics=sem, vmem_limit_bytes=VMEM_LIMIT)


def _split2(x):
    hi = x.astype(BF16)
    lo = (x - hi.astype(F32)).astype(BF16)
    return hi, lo


def _dot(a, b):
    return jnp.dot(a, b, preferred_element_type=F32)


def _dot_nt(a, b):
    return lax.dot_general(a, b, (((1,), (1,)), ((), ())), preferred_element_type=F32)


def _dot_exactw(x, w_bf):
    hi, lo = _split2(x)
    return _dot(hi, w_bf) + _dot(lo, w_bf)


def _dot_hl(x, w_hi, w_lo):
    hi, lo = _split2(x)
    return _dot(hi, w_hi) + (_dot(lo, w_hi) + _dot(hi, w_lo))


def _sigmoid(x):
    return 1.0 / (1.0 + jnp.exp(-x))


def _rms(x, g):
    ms = jnp.mean(x * x, axis=-1, keepdims=True)
    return x * lax.rsqrt(ms + NORM_EPS) * g


def _in_proj_kernel(x_ref, g_ref, w_ref, z_ref, xn_s):
    @pl.when(pl.program_id(1) == 0)
    def _():
        xn_s[...] = _rms(x_ref[...], g_ref[...]).astype(BF16)

    z_ref[...] = _dot(xn_s[...], w_ref[...])


def _in_proj(x2, g, w_bf, tm):
    M, D = x2.shape
    tn = NZ
    return pl.pallas_call(
        _in_proj_kernel,
        grid=(M // tm, NZ // tn),
        in_specs=[pl.BlockSpec((tm, D), lambda i, j: (i, 0)),
                  pl.BlockSpec((1, D), lambda i, j: (0, 0)),
                  pl.BlockSpec((D, tn), lambda i, j: (0, j))],
        out_specs=pl.BlockSpec((tm, tn), lambda i, j: (i, j)),
        out_shape=jax.ShapeDtypeStruct((M, NZ), F32),
        scratch_shapes=[pltpu.VMEM((tm, D), BF16)],
        compiler_params=_cparams(("parallel", "arbitrary")),
        name="in_proj",
    )(x2, g, w_bf)


def _rwkv_kernel(z_ref, sh0_ref, wkv0_ref, mu_ref, w0_ref, wl_hi, wl_lo, a0_ref, al_hi, al_lo, gl_hi, gl_lo,
                 kk_ref, ka_ref, rk_ref, lg_ref, lb_ref, gseg_ref,
                 o_ref, st_ref,
                 S_s, carry_s, w_s, k_s, v_s, a_s, b_s, r_s, y_s, bon_s, g_s, *, bb, Tc):
    c = pl.program_id(1)
    NP = H_R // 2

    @pl.when(c == 0)
    def _():
        for b in range(bb):
            for p in range(NP):
                S_s[b, p] = jnp.concatenate([wkv0_ref[b, 2 * p], wkv0_ref[b, 2 * p + 1]], axis=1)
        carry_s[...] = sh0_ref[...]

    gseg = gseg_ref[...]
    mu = mu_ref[...]
    rid = lax.broadcasted_iota(I32, (Tc, N_RWKV_COLS), 0)
    for b in range(bb):
        z = z_ref[b]
        prev = jnp.where(rid == 0, carry_s[b], pltpu.roll(z, 1, 0))
        carry_s[b] = z[Tc - 1:Tc, :]
        zs = z + (prev - z) * mu
        r = zs[:, 0:C_R]
        k = zs[:, C_R:2 * C_R]
        v = zs[:, 2 * C_R:3 * C_R]
        t6 = zs[:, 3 * C_R:]
        lw = w0_ref[...] + _dot_hl(jnp.tanh(t6), wl_hi[...], wl_lo[...])
        nl = -lw
        softplus = jnp.maximum(nl, 0.0) + jnp.log(1.0 + jnp.exp(-jnp.abs(nl)))
        decay = jnp.exp(-jnp.exp(-softplus - 0.5))
        a = _sigmoid(a0_ref[...] + _dot_hl(t6, al_hi[...], al_lo[...]))
        g_s[b] = _dot_hl(_sigmoid(t6), gl_hi[...], gl_lo[...])
        kk = k * kk_ref[...]
        nrm = jnp.sqrt(_dot_exactw(kk * kk, gseg))
        kk = kk / jnp.maximum(nrm, 1e-12)
        k2 = k * (1.0 + (a - 1.0) * ka_ref[...])
        bon_s[b] = _dot_exactw(r * k2 * rk_ref[...], gseg) * v
        na = -kk
        kb = kk * a
        for p in range(NP):
            sl = slice(p * 2 * HD, (p + 1) * 2 * HD)
            w_s[b, p] = decay[:, sl]
            k_s[b, p] = k2[:, sl]
            v_s[b, p] = v[:, sl]
            a_s[b, p] = na[:, sl]
            b_s[b, p] = kb[:, sl]
            r_s[b, p] = r[:, sl]

    W2 = 2 * HD
    i0 = lax.broadcasted_iota(I32, (HD, W2), 0)
    i1 = lax.broadcasted_iota(I32, (HD, W2), 1)
    eye_a = (i1 == i0).astype(F32)
    eye_b = (i1 == i0 + HD).astype(F32)
    lane_lo = i1 < HD
    j0 = lax.broadcasted_iota(I32, (W2, W2), 0)
    j1 = lax.broadcasted_iota(I32, (W2, W2), 1)
    ones_bd = ((j0 < HD) == (j1 < HD)).astype(F32).astype(BF16)
    s0 = lax.broadcasted_iota(I32, (SUBLANES, W2), 0)
    s1 = lax.broadcasted_iota(I32, (SUBLANES, W2), 1)
    rsel = (((s0 == 0) & (s1 < HD)) | ((s0 == 1) & (s1 >= HD))).astype(F32)
    chains = [(b, p) for b in range(bb) for p in range(NP)]

    def sub(i, carry):
        t0 = pl.multiple_of(i * SUBLANES, SUBLANES)
        S = [S_s[b, p] for b, p in chains]

        def emit_y(t, Sb):
            for ci, (b, p) in enumerate(chains):
                rm = (r_s[b, p, pl.ds(t, 1), :] * rsel).astype(BF16)
                ym = _dot_nt(rm, Sb[ci])
                y_s[b, 2 * p, pl.ds(t, 1), :] = ym[0:1, :]
                y_s[b, 2 * p + 1, pl.ds(t, 1), :] = ym[1:2, :]

        pending = None
        for j in range(SUBLANES):
            t = t0 + j
            row = lambda ref, ci: ref[chains[ci][0], chains[ci][1], pl.ds(t, 1), :]
            n = len(chains)
            sa = [_dot((S[ci] * row(a_s, ci)).astype(BF16), ones_bd) for ci in range(n)]
            if pending is not None:
                emit_y(*pending)
            for ci in range(n):
                v = row(v_s, ci)
                vc = jnp.where(lane_lo, jnp.sum(eye_a * v, axis=1, keepdims=True),
                               jnp.sum(eye_b * v, axis=1, keepdims=True))
                S[ci] = S[ci] * row(w_s, ci) + sa[ci] * row(b_s, ci) + vc * row(k_s, ci)
            pending = (t, [s.astype(BF16) for s in S])
        emit_y(*pending)
        for ci, (b, p) in enumerate(chains):
            S_s[b, p] = S[ci]
        return carry

    lax.fori_loop(0, Tc // SUBLANES, sub, 0)

    for b in range(bb):
        y = jnp.concatenate([y_s[b, h] for h in range(H_R)], axis=-1)
        mean = _dot_exactw(y, gseg) * (1.0 / HD)
        d = y - mean
        var = _dot_exactw(d * d, gseg) * (1.0 / HD)
        yn = d * lax.rsqrt(var + RWKV_LN_EPS) * lg_ref[...] + lb_ref[...]
        o_ref[b] = (yn + bon_s[b]) * g_s[b]

    @pl.when(c == pl.num_programs(1) - 1)
    def _():
        for b in range(bb):
            for p in range(NP):
                s2 = S_s[b, p]
                st_ref[b, 2 * p] = s2[:, 0:HD]
                st_ref[b, 2 * p + 1] = s2[:, HD:]


def _rwkv(z3, shift0, wkv0, P, bb, Tc):
    B, T, _ = z3.shape
    vec = lambda n: pl.BlockSpec((1, n), lambda bi, c: (0, 0))
    mat = lambda r, n: pl.BlockSpec((r, n), lambda bi, c: (0, 0))
    kern = functools.partial(_rwkv_kernel, bb=bb, Tc=Tc)
    hv = lambda: pltpu.VMEM((bb, H_R // 2, Tc, 2 * HD), F32)
    return pl.pallas_call(
        kern,
        grid=(B // bb, T // Tc),
        in_specs=[pl.BlockSpec((bb, Tc, N_RWKV_COLS), lambda bi, c: (bi, c, 0)),
                  pl.BlockSpec((bb, 1, N_RWKV_COLS), lambda bi, c: (bi, 0, 0)),
                  pl.BlockSpec((bb, H_R, HD, HD), lambda bi, c: (bi, 0, 0, 0)),
                  vec(N_RWKV_COLS), vec(C_R), mat(LANES, C_R), mat(LANES, C_R), vec(C_R), mat(LANES, C_R),
                  mat(LANES, C_R), mat(LANES, C_R), mat(LANES, C_R),
                  vec(C_R), vec(C_R), vec(C_R), vec(C_R), vec(C_R), mat(C_R, C_R)],
        out_specs=[pl.BlockSpec((bb, Tc, C_R), lambda bi, c: (bi, c, 0)),
                   pl.BlockSpec((bb, H_R, HD, HD), lambda bi, c: (bi, 0, 0, 0))],
        out_shape=[jax.ShapeDtypeStruct((B, T, C_R), F32),
                   jax.ShapeDtypeStruct((B, H_R, HD, HD), F32)],
        scratch_shapes=[pltpu.VMEM((bb, H_R // 2, HD, 2 * HD), F32),
                        pltpu.VMEM((bb, 1, N_RWKV_COLS), F32),
                        hv(), hv(), hv(), hv(), hv(), hv(),
                        pltpu.VMEM((bb, H_R, Tc, HD), F32),
                        pltpu.VMEM((bb, Tc, C_R), F32),
                        pltpu.VMEM((bb, Tc, C_R), F32)],
        compiler_params=_cparams(("parallel", "arbitrary")),
        name="rwkv7",
    )(z3, shift0, wkv0, P["mu"], P["w0"], P["wl_hi"], P["wl_lo"], P["a0"], P["al_hi"], P["al_lo"],
      P["gl_hi"], P["gl_lo"], P["k_k"], P["k_a"], P["r_k"], P["lnx_g"], P["lnx_b"], P["gseg4"])


def _conv_kernel(z_ref, c0_ref, w_ref, b_ref, lg_ref, lb_ref, o_ref, ct_ref, buf, *, Tc):
    @pl.when(pl.program_id(1) == 0)
    def _():
        buf[0:HIST, :] = c0_ref[0]

    z = z_ref[0]
    u = z[:, 0:C_CONV] * _sigmoid(z[:, C_CONV:])
    buf[HIST:HIST + Tc, :] = u
    w = w_ref[...]
    acc = jnp.zeros((Tc, C_CONV), F32) + b_ref[...]
    off = HIST - (CONV_W - 1)
    for j in range(CONV_W):
        acc = acc + w[j:j + 1, :] * buf[off + j:off + j + Tc, :]
    mu = jnp.mean(acc, axis=-1, keepdims=True)
    d = acc - mu
    var = jnp.mean(d * d, axis=-1, keepdims=True)
    cn = d * lax.rsqrt(var + CONV_LN_EPS) * lg_ref[...] + lb_ref[...]
    o_ref[0] = cn * _sigmoid(cn)
    hist = buf[Tc:Tc + HIST, :]
    buf[0:HIST, :] = hist
    ct_ref[0] = hist


def _conv(z3, conv0p, P, Tc):
    B, T, _ = z3.shape
    vec = pl.BlockSpec((1, C_CONV), lambda b, c: (0, 0))
    return pl.pallas_call(
        functools.partial(_conv_kernel, Tc=Tc),
        grid=(B, T // Tc),
        in_specs=[pl.BlockSpec((1, Tc, 2 * C_CONV), lambda b, c: (b, c, ZC_CONV)),
                  pl.BlockSpec((1, HIST, C_CONV), lambda b, c: (b, 0, 0)),
                  pl.BlockSpec((CONV_W, C_CONV), lambda b, c: (0, 0)),
                  vec, vec, vec],
        out_specs=[pl.BlockSpec((1, Tc, C_CONV), lambda b, c: (b, c, 0)),
                   pl.BlockSpec((1, HIST, C_CONV), lambda b, c: (b, 0, 0))],
        out_shape=[jax.ShapeDtypeStruct((B, T, C_CONV), F32),
                   jax.ShapeDtypeStruct((B, HIST, C_CONV), F32)],
        scratch_shapes=[pltpu.VMEM((HIST + Tc, C_CONV), F32)],
        compiler_params=_cparams(("parallel", "arbitrary")),
        name="conformer_conv",
    )(z3, conv0p, P["conv_w"], P["conv_b"], P["conv_ln_g"], P["conv_ln_b"])


def _att_prep_kernel(k_ref, kw_ref, kg_ref, kig_ref, gseg_ref, kn_ref, kin_ref):
    k = k_ref[...]
    ms = _dot_exactw(k * k, gseg_ref[...]) * (1.0 / HD)
    kn_ref[...] = k * lax.rsqrt(ms + NORM_EPS) * kg_ref[...]
    kw = kw_ref[...]
    ki = kw[:, 0:D_IDX]
    msi = jnp.mean(ki * ki, axis=-1, keepdims=True)
    kin_ref[...] = ki * lax.rsqrt(msi + NORM_EPS) * kig_ref[...]


def _att_prep(z2, P, tm):
    M = z2.shape[0]
    return pl.pallas_call(
        _att_prep_kernel,
        grid=(M // tm,),
        in_specs=[pl.BlockSpec((tm, LANES), lambda i: (i, ZC_K)),
                  pl.BlockSpec((tm, LANES), lambda i: (i, ZC_KW)),
                  pl.BlockSpec((1, LANES), lambda i: (0, 0)),
                  pl.BlockSpec((1, D_IDX), lambda i: (0, 0)),
                  pl.BlockSpec((LANES, LANES), lambda i: (0, 0))],
        out_specs=[pl.BlockSpec((tm, LANES), lambda i: (i, 0)),
                   pl.BlockSpec((tm, D_IDX), lambda i: (i, 0))],
        out_shape=[jax.ShapeDtypeStruct((M, N_KV * HD), F32),
                   jax.ShapeDtypeStruct((M, D_IDX), F32)],
        compiler_params=_cparams(("parallel",)),
        name="att_prep",
    )(z2, z2, P["k_norm2"], P["kidx_norm"], P["gseg2"])


def _bias_kernel(rb_ref, idx_ref, o_ref):
    h = pl.program_id(0) % H_A
    idx = idx_ref[0]
    out = jnp.zeros(idx.shape, F32)
    for bk in range(N_BUCKETS):
        out = jnp.where(idx == bk, rb_ref[bk, h], out)
    o_ref[0] = out


def _bias_expand(rel_bias, idx):
    S, R, L = idx.shape
    return pl.pallas_call(
        _bias_kernel,
        grid=(S,),
        in_specs=[pl.BlockSpec(memory_space=pltpu.SMEM),
                  pl.BlockSpec((1, R, L), lambda s: (s, 0, 0))],
        out_specs=pl.BlockSpec((1, R, L), lambda s: (s, 0, 0)),
        out_shape=jax.ShapeDtypeStruct((S, R, L), F32),
        compiler_params=_cparams(("parallel",)),
        name="rel_bias_tiles",
    )(rel_bias, idx)


def _rel_bucket_np(dist):
    max_exact = N_BUCKETS // 2
    d_f = np.maximum(dist, max_exact).astype(np.float32)
    large = max_exact + (np.log(d_f / np.float32(max_exact)) / np.float32(math.log(MAX_DISTANCE / max_exact))
                         * np.float32(N_BUCKETS - max_exact)).astype(np.int32)
    return np.where(dist < max_exact, dist, np.minimum(large, N_BUCKETS - 1)).astype(np.int32)


def _sort_key(s):
    bits = pltpu.bitcast(s + 0.0, I32)
    return jnp.where(bits < 0, bits ^ 0x7FFFFFFF, bits)


def _q_prep(q, qg, gseg8):
    ms = _dot_exactw(q * q, gseg8) * (1.0 / HD)
    return (q * lax.rsqrt(ms + NORM_EPS) * qg * (HD ** -0.5)).astype(BF16)


def _attp_kernel(q_ref, qi_ref, kw_ref, kn_ref, v_ref, kin_ref, qg_ref, bias_ref, gseg_ref, tri_ref, ones_ref,
                 o_ref,
                 kk_s, vv_s, ki_s, qs_s, qi_s, wb_s, key_s, hi_s, lo_s, tau_s, madd_s, s_s, m_s, l_s, acc_s,
                 *, tq, topk, kg, ag):
    j = pl.program_id(1)
    nh = tq // LANES
    hp = m_s.shape[0] // tq
    nb = (j + 1) * nh
    KW = kg * LANES
    AW = ag * LANES
    ngk = (nb + kg - 1) // kg
    nga = (nb + ag - 1) // ag
    R = hp * tq

    @pl.when(j == 0)
    def _():
        knT = jnp.transpose(kn_ref[0])
        v = v_ref[0]
        for n in range(N_KV):
            kk_s[n] = knT[n * HD:(n + 1) * HD, :].astype(BF16)
            vv_s[n] = v[:, n * HD:(n + 1) * HD].astype(BF16)
        kin = kin_ref[0]
        kinT = jnp.transpose(jnp.concatenate([kin, jnp.zeros(kin.shape, F32)], axis=1))
        ki_s[...] = kinT[0:D_IDX, :].astype(BF16)

    qn = _q_prep(q_ref[0], qg_ref[...], gseg_ref[...])
    for n in range(N_KV):
        qs_s[n] = jnp.concatenate([qn[:, (n * G_A + g) * HD:(n * G_A + g + 1) * HD] for g in range(G_A)], axis=0)
    qi = (qi_ref[0] * (D_IDX ** -0.5)).astype(BF16)
    kw = kw_ref[0]
    for h in range(H_I):
        qi_s[h] = qi[:, h * D_IDX:(h + 1) * D_IDX]
        wb_s[h] = jnp.broadcast_to(kw[:, D_IDX + h:D_IDX + h + 1] * (H_I ** -0.5), (tq, KW))

    row = lax.broadcasted_iota(I32, (tq, KW), 0) + j * tq
    col0 = lax.broadcasted_iota(I32, (tq, KW), 1)

    def idx_body(gi, carry):
        off = pl.multiple_of(gi * KW, KW)
        kc = ki_s[:, pl.ds(off, KW)]
        s = jnp.zeros((tq, KW), F32)
        for h in range(H_I):
            s = s + jnp.maximum(_dot(qi_s[h], kc), 0.0) * wb_s[h]
        s = jnp.where(col0 + off <= row, s, -jnp.inf)
        key = _sort_key(s)
        key_s[:, pl.ds(off, KW)] = key
        hi_s[:, pl.ds(off, KW)] = lax.shift_right_arithmetic(key, 16).astype(I16)
        lo_s[:, pl.ds(off, KW)] = ((key & 0xFFFF) + I16_MIN).astype(I16)
        return carry

    lax.fori_loop(0, ngk, idx_body, 0)

    def key_tiles(gi, ref=key_s):
        off = pl.multiple_of(gi * KW, KW)
        key = ref[:, pl.ds(off, KW)]
        return [key[:, u * LANES:(u + 1) * LANES] for u in range(kg)]

    def lane_total(acc):
        return jnp.broadcast_to(jnp.sum(acc.astype(F32), axis=1, keepdims=True), (tq, LANES))

    one16 = jnp.ones((tq, LANES), I16)
    zero16 = jnp.zeros((tq, LANES), I16)

    def search16(ref, kneed):
        for ng in range(1, key_s.shape[1] // KW + 1):
            @pl.when(ngk == ng)
            def _(ng=ng):
                def bit_body(it, cur):
                    cand = cur + lax.shift_left(jnp.int32(1), jnp.asarray(15 - it, dtype=I32))
                    c16 = cand.astype(I16)
                    accs = [zero16] * kg
                    for gi in range(ng):
                        for u in range(kg):
                            kt = ref[:, gi * KW + u * LANES:gi * KW + (u + 1) * LANES]
                            accs[u] = accs[u] + jnp.where(kt >= c16, one16, zero16)
                    acc = accs[0]
                    for a in accs[1:]:
                        acc = acc + a
                    return jnp.where(lane_total(acc) >= kneed, cand, cur)

                tau_s[...] = lax.fori_loop(0, 16, bit_body, jnp.full((tq, LANES), I16_MIN, I32))

        return tau_s[...]

    tau_hi = search16(hi_s, float(topk))
    th16 = tau_hi.astype(I16)
    lo_pad = jnp.full((tq, LANES), I16_MIN, I16)

    def mid_body(gi, acc):
        off = pl.multiple_of(gi * KW, KW)
        los = key_tiles(gi, lo_s)
        for u, ht in enumerate(key_tiles(gi, hi_s)):
            acc = acc + jnp.where(ht > th16, one16, zero16)
            lo_s[:, pl.ds(off + u * LANES, LANES)] = jnp.where(ht == th16, los[u], lo_pad)
        return acc

    need_lo = float(topk) - lane_total(lax.fori_loop(0, ngk, mid_body, zero16))
    tau_lo = search16(lo_s, need_lo)
    tau = lax.shift_left(tau_hi, 16) + (tau_lo - I16_MIN)

    def cnt2_body(gi, carry):
        cge, cgt = carry
        for kt in key_tiles(gi):
            cge = cge + jnp.where(kt >= tau, 1.0, 0.0)
            cgt = cgt + jnp.where(kt > tau, 1.0, 0.0)
        return cge, cgt

    zero = jnp.zeros((tq, LANES), F32)
    cge, cgt = lax.fori_loop(0, ngk, cnt2_body, (zero, zero))
    need = float(topk) - lane_total(cgt)
    ties = jnp.max(lane_total(cge)) > float(topk)

    @pl.when(jnp.logical_not(ties))
    def _():
        def body(gi, carry):
            off = pl.multiple_of(gi * KW, KW)
            for u, kt in enumerate(key_tiles(gi)):
                take = jnp.where(kt >= tau, jnp.where(kt > KEY_NEG_INF, 0.0, NEG), NEG)
                madd_s[:, pl.ds(off + u * LANES, LANES)] = take
            return carry

        lax.fori_loop(0, ngk, body, 0)

    @pl.when(ties)
    def _():
        tri = tri_ref[...]
        ones = ones_ref[...]

        def body(c, run):
            off = pl.multiple_of(c * LANES, LANES)
            key = key_s[:, pl.ds(off, LANES)]
            eq = jnp.where(key == tau, 1.0, 0.0)
            eqb = eq.astype(BF16)
            pre = _dot(eqb, tri) + run
            take = jnp.where(key > tau, 1.0, jnp.where(pre <= need, eq, 0.0))
            take = jnp.where(key > KEY_NEG_INF, take, 0.0)
            madd_s[:, pl.ds(off, LANES)] = jnp.where(take > 0.5, 0.0, NEG)
            return run + _dot(eqb, ones)

        lax.fori_loop(0, ngk * kg, body, jnp.zeros((tq, LANES), F32))

    slab_list = [(gl, hf) for gl in range(hp) for hf in range(nh)]
    outs = []
    for n in range(N_KV):
      for gp in range(G_A // hp):
        m_s[...] = jnp.full((R, LANES), NEG, F32)
        l_s[...] = jnp.zeros((R, LANES), F32)
        acc_s[...] = jnp.zeros((R, HD), F32)

        def pass_a(gi, carry, n=n, gp=gp):
            off = pl.multiple_of(gi * AW, AW)
            kc = kk_s[n, :, pl.ds(off, AW)]
            s_all = _dot(qs_s[n, gp * R:(gp + 1) * R, :], kc)
            for gl, hf in slab_list:
                r0 = gl * tq + hf * LANES
                rows = pl.ds(r0, LANES)
                head = n * G_A + gp * hp + gl
                mx = m_s[rows, :]
                for u in range(ag):
                    cols = pl.ds(off + u * LANES, LANES)
                    dd = jnp.clip(j * nh + hf - (gi * ag + u), 0, 2)
                    su = (s_all[r0:r0 + LANES, u * LANES:(u + 1) * LANES] + bias_ref[dd, head]
                          + madd_s[hf * LANES:(hf + 1) * LANES, cols])
                    s_s[rows, cols] = su
                    mx = jnp.maximum(mx, su)
                m_s[rows, :] = mx
            return carry

        lax.fori_loop(0, nga, pass_a, 0)
        mb = jnp.broadcast_to(jnp.max(m_s[...], axis=1, keepdims=True), (R, LANES))
        m_s[...] = mb

        def pass_b(gi, carry, n=n):
            off = pl.multiple_of(gi * AW, AW)
            slabs = []
            for gl, hf in slab_list:
                rows = pl.ds(gl * tq + hf * LANES, LANES)
                mrow = m_s[rows, :]
                lacc = l_s[rows, :]
                ps = []
                for u in range(ag):
                    p = jnp.exp(s_s[rows, pl.ds(off + u * LANES, LANES)] - mrow)
                    lacc = lacc + p
                    ps.append(p.astype(BF16))
                l_s[rows, :] = lacc
                slabs.append(ps[0] if ag == 1 else jnp.concatenate(ps, axis=1))
            acc_s[...] += _dot(jnp.concatenate(slabs, axis=0), vv_s[n, pl.ds(off, AW), :])
            return carry

        lax.fori_loop(0, nga, pass_b, 0)
        o = acc_s[...] / jnp.sum(l_s[...], axis=1, keepdims=True)
        for gl in range(hp):
            outs.append(o[gl * tq:(gl + 1) * tq, :])
    o_ref[0] = jnp.concatenate(outs, axis=-1)


def _att_prompt(z3, kn3, kin3, P, bias_p, topk, tq):
    B, T, _ = z3.shape
    nblk = T // LANES
    kg = 4 if nblk % 4 == 0 else (2 if nblk % 2 == 0 else 1)
    ag = kg
    rp = min(G_A * tq, ATT_ROWS)
    cst = lambda shp: pl.BlockSpec(shp, lambda b, j: tuple(0 for _ in shp))
    return pl.pallas_call(
        functools.partial(_attp_kernel, tq=tq, topk=topk, kg=kg, ag=ag),
        grid=(B, T // tq),
        in_specs=[pl.BlockSpec((1, tq, C_A), lambda b, j: (b, j, ZC_Q)),
                  pl.BlockSpec((1, tq, H_I * D_IDX), lambda b, j: (b, j, ZC_QI)),
                  pl.BlockSpec((1, tq, LANES), lambda b, j: (b, j, ZC_KW)),
                  pl.BlockSpec((1, T, N_KV * HD), lambda b, j: (b, 0, 0)),
                  pl.BlockSpec((1, T, N_KV * HD), lambda b, j: (b, 0, ZC_V)),
                  pl.BlockSpec((1, T, D_IDX), lambda b, j: (b, 0, 0)),
                  cst((1, C_A)), cst((3, H_A, LANES, LANES)), cst((C_A, C_A)), cst((LANES, LANES)),
                  cst((LANES, LANES))],
        out_specs=pl.BlockSpec((1, tq, C_A), lambda b, j: (b, j, 0)),
        out_shape=jax.ShapeDtypeStruct((B, T, C_A), F32),
        scratch_shapes=[pltpu.VMEM((N_KV, HD, T), BF16),
                        pltpu.VMEM((N_KV, T, HD), BF16),
                        pltpu.VMEM((D_IDX, T), BF16),
                        pltpu.VMEM((N_KV, G_A * tq, HD), BF16),
                        pltpu.VMEM((H_I, tq, D_IDX), BF16),
                        pltpu.VMEM((H_I, tq, kg * LANES), F32),
                        pltpu.VMEM((tq, T), I32),
                        pltpu.VMEM((tq, T), I16),
                        pltpu.VMEM((tq, T), I16),
                        pltpu.VMEM((tq, LANES), I32),
                        pltpu.VMEM((tq, T), F32),
                        pltpu.VMEM((rp, T), F32),
                        pltpu.VMEM((rp, LANES), F32),
                        pltpu.VMEM((rp, LANES), F32),
                        pltpu.VMEM((rp, HD), F32)],
        compiler_params=_cparams(("parallel", "arbitrary")),
        name="dsa_prompt",
    )(z3, z3, z3, kn3, z3, kin3, P["q_norm8"], bias_p, P["gseg8"], P["tri"], P["ones"])


def _pad_transpose(x):
    r, c = x.shape
    if c < LANES:
        x = jnp.concatenate([x, jnp.zeros((r, LANES - c), F32)], axis=1)
    x = jnp.concatenate([x, jnp.zeros((LANES - r, LANES), F32)], axis=0)
    return jnp.transpose(x)


def _atts_kernel(pt_ref, ckT_ref, cvT_ref, ckiT_ref, q_ref, qi_ref, kw_ref, kn_ref, vn_ref, kin_ref, qg_ref,
                 bias_ref, gseg_ref, tri_ref, ones_ref,
                 o_ref,
                 kT_s, vT_s, kiT_s, madd_s, sem, *, layer, npages, nbatch, Ts, topk):
    b = pl.program_id(0)
    slot = b % 2
    P = LANES
    L = (npages + 1) * P

    def page_copies(seq, sl, p):
        pg = pt_ref[seq, p]
        col = pl.multiple_of(p * P, P)
        return (pltpu.make_async_copy(ckT_ref.at[layer, pg], kT_s.at[sl, :, :, pl.ds(col, P)], sem.at[sl, 0]),
                pltpu.make_async_copy(cvT_ref.at[layer, pg], vT_s.at[sl, :, :, pl.ds(col, P)], sem.at[sl, 1]),
                pltpu.make_async_copy(ckiT_ref.at[layer, pg], kiT_s.at[sl, :, pl.ds(col, P)], sem.at[sl, 2]))

    def start_pages(seq, sl):
        def body(p, carry):
            for cp in page_copies(seq, sl, p):
                cp.start()
            return carry
        lax.fori_loop(0, npages, body, 0)

    def wait_pages(seq, sl):
        def body(p, carry):
            for cp in page_copies(seq, sl, p):
                cp.wait()
            return carry
        lax.fori_loop(0, npages, body, 0)

    @pl.when(b == 0)
    def _():
        start_pages(0, 0)

    @pl.when(b + 1 < nbatch)
    def _():
        start_pages(b + 1, 1 - slot)

    knT = _pad_transpose(kn_ref[0])
    vnT = _pad_transpose(vn_ref[0])
    for n in range(N_KV):
        kT_s[slot, n, :, npages * P:L] = knT[n * HD:(n + 1) * HD, :]
        vT_s[slot, n, :, npages * P:L] = vnT[n * HD:(n + 1) * HD, :]
    kiT_s[slot, :, npages * P:L] = _pad_transpose(kin_ref[0])[0:D_IDX, :]

    qn = _q_prep(q_ref[0], qg_ref[...], gseg_ref[...])
    qi = (qi_ref[0] * (D_IDX ** -0.5)).astype(BF16)
    qis = jnp.concatenate([qi[:, h * D_IDX:(h + 1) * D_IDX] for h in range(H_I)], axis=0)
    kw = kw_ref[0]

    wait_pages(b, slot)

    d = _dot(qis, kiT_s[slot].astype(BF16))
    s = jnp.zeros((Ts, L), F32)
    for h in range(H_I):
        s = s + jnp.maximum(d[h * Ts:(h + 1) * Ts, :], 0.0) * (kw[:, D_IDX + h:D_IDX + h + 1] * (H_I ** -0.5))
    col = lax.broadcasted_iota(I32, (Ts, L), 1)
    row = lax.broadcasted_iota(I32, (Ts, L), 0) + npages * P
    key = _sort_key(jnp.where(col <= row, s, -jnp.inf))

    def bit_body(it, cur):
        step = lax.shift_left(jnp.int32(1), jnp.asarray(30 - 2 * it, dtype=I32))
        for mult in (1, 2, 3):
            cand = cur + mult * step
            cnt = jnp.sum(jnp.where(key >= cand, 1.0, 0.0), axis=1, keepdims=True)
            new = jnp.where(cnt >= float(topk), cand, cur) if mult == 1 else jnp.where(cnt >= float(topk), cand, new)
        return new

    tau = lax.fori_loop(0, 16, bit_body, jnp.full((Ts, 1), INT_MIN, I32))
    need = float(topk) - jnp.sum(jnp.where(key > tau, 1.0, 0.0), axis=1, keepdims=True)
    cge = jnp.sum(jnp.where(key >= tau, 1.0, 0.0), axis=1, keepdims=True)
    ties = jnp.max(cge) > float(topk)

    @pl.when(jnp.logical_not(ties))
    def _():
        madd_s[...] = jnp.where(key >= tau, jnp.where(key > KEY_NEG_INF, 0.0, NEG), NEG)

    @pl.when(ties)
    def _():
        tri = tri_ref[...]
        ones = ones_ref[...]
        eq = jnp.where(key == tau, 1.0, 0.0)
        run = jnp.zeros((Ts, LANES), F32)
        for c in range(L // LANES):
            sl = slice(c * LANES, (c + 1) * LANES)
            eqc = eq[:, sl]
            eqb = eqc.astype(BF16)
            pre = _dot(eqb, tri) + run
            take = jnp.where(key[:, sl] > tau, 1.0, jnp.where(pre <= need, eqc, 0.0))
            take = jnp.where(key[:, sl] > KEY_NEG_INF, take, 0.0)
            madd_s[:, sl] = jnp.where(take > 0.5, 0.0, NEG)
            run = run + _dot(eqb, ones)

    madd = madd_s[...]
    madd4 = jnp.concatenate([madd] * G_A, axis=0)
    outs = []
    for n in range(N_KV):
        qs = jnp.concatenate([qn[:, (n * G_A + g) * HD:(n * G_A + g + 1) * HD] for g in range(G_A)], axis=0)
        sc = _dot(qs, kT_s[slot, n].astype(BF16)) + bias_ref[n] + madd4
        m = jnp.max(sc, axis=1, keepdims=True)
        pe = jnp.exp(sc - m)
        l = jnp.sum(pe, axis=1, keepdims=True)
        o = _dot_nt(pe.astype(BF16), vT_s[slot, n].astype(BF16)) / l
        for g in range(G_A):
            outs.append(o[g * Ts:(g + 1) * Ts, :])
    o_ref[0] = jnp.concatenate(outs, axis=-1)


def _att_sample(layer, page_table, ckT, cvT, ckiT, z3, kn3, kin3, P, bias_s, topk):
    B, Ts, _ = z3.shape
    npages = page_table.shape[1]
    L = (npages + 1) * LANES
    hbm = pl.BlockSpec(memory_space=pl.ANY)
    cst = lambda shp: pl.BlockSpec(shp, lambda b, pt: tuple(0 for _ in shp))
    grid_spec = pltpu.PrefetchScalarGridSpec(
        num_scalar_prefetch=1,
        grid=(B,),
        in_specs=[hbm, hbm, hbm,
                  pl.BlockSpec((1, Ts, C_A), lambda b, pt: (b, 0, ZC_Q)),
                  pl.BlockSpec((1, Ts, H_I * D_IDX), lambda b, pt: (b, 0, ZC_QI)),
                  pl.BlockSpec((1, Ts, LANES), lambda b, pt: (b, 0, ZC_KW)),
                  pl.BlockSpec((1, Ts, N_KV * HD), lambda b, pt: (b, 0, 0)),
                  pl.BlockSpec((1, Ts, N_KV * HD), lambda b, pt: (b, 0, ZC_V)),
                  pl.BlockSpec((1, Ts, D_IDX), lambda b, pt: (b, 0, 0)),
                  cst((1, C_A)), cst((N_KV, G_A * Ts, L)), cst((C_A, C_A)), cst((LANES, LANES)),
                  cst((LANES, LANES))],
        out_specs=pl.BlockSpec((1, Ts, C_A), lambda b, pt: (b, 0, 0)),
        scratch_shapes=[pltpu.VMEM((2, N_KV, HD, L), F32),
                        pltpu.VMEM((2, N_KV, HD, L), F32),
                        pltpu.VMEM((2, D_IDX, L), F32),
                        pltpu.VMEM((Ts, L), F32),
                        pltpu.SemaphoreType.DMA((2, 3))])
    return pl.pallas_call(
        functools.partial(_atts_kernel, layer=layer, npages=npages, nbatch=B, Ts=Ts, topk=topk),
        grid_spec=grid_spec,
        out_shape=jax.ShapeDtypeStruct((B, Ts, C_A), F32),
        compiler_params=_cparams(("arbitrary",)),
        name="dsa_sample",
    )(page_table, ckT, cvT, ckiT, z3, z3, z3, kn3, z3, kin3, P["q_norm8"], bias_s, P["gseg8"], P["tri"], P["ones"])


def _out_proj_kernel(x_ref, orw_ref, ocv_ref, oat_ref, w_ref, y_ref):
    acc = _dot(orw_ref[...].astype(BF16), w_ref[0:C_R, :])
    acc = acc + _dot(ocv_ref[...].astype(BF16), w_ref[C_R:C_R + C_CONV, :])
    acc = acc + _dot(oat_ref[...].astype(BF16), w_ref[C_R + C_CONV:, :])
    y_ref[...] = x_ref[...] + acc


def _out_proj(x2, orw, ocv, oat, w_bf, tm):
    M, D = x2.shape
    row = lambda n: pl.BlockSpec((tm, n), lambda i: (i, 0))
    return pl.pallas_call(
        _out_proj_kernel,
        grid=(M // tm,),
        in_specs=[row(D), row(C_R), row(C_CONV), row(C_A), pl.BlockSpec((D, D), lambda i: (0, 0))],
        out_specs=row(D),
        out_shape=jax.ShapeDtypeStruct((M, D), F32),
        compiler_params=_cparams(("parallel",)),
        name="out_proj",
    )(x2, orw, ocv, oat, w_bf)


def _gelu(x):
    return 0.5 * x * (1.0 + lax.erf(x * (2.0 ** -0.5)))


def _ffn_kernel(x_ref, p_ref, ng_ref, upg_ref, upv_ref, cw_ref, cb_ref, dn_ref, h_ref, h2_ref, pg_ref, pgate_ref,
                pproj_ref,
                xo_ref, tail_ref,
                xn_s, acc_s, hist_s, *, tm, nF, seq_tiles, seq_len):
    i = pl.program_id(0)
    f = pl.program_id(1)

    @pl.when(f == 0)
    def _():
        xn_s[...] = _rms(x_ref[...], ng_ref[...]).astype(BF16)
        acc_s[...] = jnp.zeros(acc_s.shape, F32)

    xn = xn_s[...]
    g = _dot(xn, upg_ref[...])
    val = _dot(xn, upv_ref[...])
    row = lax.broadcasted_iota(I32, g.shape, 0)
    if seq_tiles is not None:
        first = (i % seq_tiles) == 0
        hs = hist_s[f]
        h0 = h_ref[0]
        hm2 = jnp.where(first, h0[0:1, :], hs[SUBLANES - 2:SUBLANES - 1, :])
        hm1 = jnp.where(first, h0[1:2, :], hs[SUBLANES - 1:SUBLANES, :])
        g1 = jnp.where(row == 0, hm1, pltpu.roll(g, 1, 0))
        g2 = jnp.where(row == 0, hm2, jnp.where(row == 1, hm1, pltpu.roll(g, 2, 0)))
        slab = g[tm - SUBLANES:tm, :]
        hist_s[f] = slab
        tail_ref[0] = slab
    else:
        t = row % seq_len
        g1 = jnp.where(t == 0, 0.0, pltpu.roll(g, 1, 0)) + h_ref[...]
        g2 = jnp.where(t < 2, 0.0, pltpu.roll(g, 2, 0)) + h2_ref[...]
        tail_ref[...] = g
    cw = cw_ref[...]
    gate = cw[0:1, :] * g2 + cw[1:2, :] * g1 + cw[2:3, :] * g + cb_ref[...]
    act = (_gelu(gate) * val).astype(BF16)
    acc_s[...] += _dot(act, dn_ref[...])

    @pl.when(f == nF - 1)
    def _():
        x2 = x_ref[...] + acc_s[...]
        xn2 = _rms(x2, pg_ref[...]).astype(BF16)
        gate2 = _sigmoid(_dot(xn2, pgate_ref[...]))
        xo_ref[...] = x2 + _dot(p_ref[...].astype(BF16), pproj_ref[...]) * gate2


def _ffn(x2, p2, ffn0, P, tm, B, T):
    M, D = x2.shape
    tf = 256
    nF = D_FF // tf
    DP = p2.shape[1]
    whole = tm % T == 0
    if whole:
        seq_tiles, seq_len = None, T
        h1 = jnp.pad(ffn0[:, 1:2, :], ((0, 0), (0, T - 1), (0, 0))).reshape(M, D_FF)
        h2 = jnp.pad(ffn0, ((0, 0), (0, T - 2), (0, 0))).reshape(M, D_FF)
        h_specs = [pl.BlockSpec((tm, tf), lambda i, f: (i, f)), pl.BlockSpec((tm, tf), lambda i, f: (i, f))]
        tail_spec = pl.BlockSpec((tm, tf), lambda i, f: (i, f))
        tail_shape = jax.ShapeDtypeStruct((M, D_FF), F32)
    else:
        seq_tiles, seq_len = T // tm, T
        h1, h2 = ffn0, ffn0
        h_specs = [pl.BlockSpec((1, 2, tf), lambda i, f: (i // seq_tiles, 0, f)),
                   pl.BlockSpec((1, 2, tf), lambda i, f: (i // seq_tiles, 0, f))]
        tail_spec = pl.BlockSpec((1, SUBLANES, tf), lambda i, f: (i, 0, f))
        tail_shape = jax.ShapeDtypeStruct((M // tm, SUBLANES, D_FF), F32)
    cst = lambda shp: pl.BlockSpec(shp, lambda i, f: tuple(0 for _ in shp))
    xo, tail = pl.pallas_call(
        functools.partial(_ffn_kernel, tm=tm, nF=nF, seq_tiles=seq_tiles, seq_len=seq_len),
        grid=(M // tm, nF),
        in_specs=[pl.BlockSpec((tm, D), lambda i, f: (i, 0)),
                  pl.BlockSpec((tm, DP), lambda i, f: (i, 0)),
                  cst((1, D)),
                  pl.BlockSpec((D, tf), lambda i, f: (0, f)),
                  pl.BlockSpec((D, tf), lambda i, f: (0, nF + f)),
                  pl.BlockSpec((3, tf), lambda i, f: (0, f)),
                  pl.BlockSpec((1, tf), lambda i, f: (0, f)),
                  pl.BlockSpec((tf, D), lambda i, f: (f, 0))] + h_specs +
                 [cst((1, D)), cst((D, D)), cst((DP, D))],
        out_specs=[pl.BlockSpec((tm, D), lambda i, f: (i, 0)), tail_spec],
        out_shape=[jax.ShapeDtypeStruct((M, D), F32), tail_shape],
        scratch_shapes=[pltpu.VMEM((tm, D), BF16), pltpu.VMEM((tm, D), F32), pltpu.VMEM((nF, SUBLANES, tf), F32)],
        compiler_params=_cparams(("parallel", "arbitrary")),
        name="conv_ffn_ple",
    )(x2, p2, P["norm_ffn"], P["ffn_up"], P["ffn_up"], P["ffn_conv_w"], P["ffn_conv_b"], P["ffn_down"], h1, h2,
      P["ple_norm"], P["ple_gate"], P["ple_proj"])
    if whole:
        ffn_T = tail.reshape(B, T, D_FF)[:, T - 2:, :]
    else:
        ffn_T = tail.reshape(B, seq_tiles, SUBLANES, D_FF)[:, seq_tiles - 1, SUBLANES - 2:, :]
    return xo, ffn_T


def _seg_ones(n, seg):
    i = np.arange(n)
    return jnp.asarray((i[:, None] // seg) == (i[None, :] // seg), dtype=BF16)


def _hl(w):
    hi = w.astype(BF16)
    return hi, (w - hi.astype(F32)).astype(BF16)


def _layer_params(i, W):
    P = {}
    w_in = W["w_in"][i]
    D = w_in.shape[0]
    P["w_in"] = jnp.concatenate([w_in[:, :N_RWKV_COLS], jnp.zeros((D, 1024 - N_RWKV_COLS), F32),
                                 w_in[:, N_RWKV_COLS:], jnp.zeros((D, NZ - 128 - N_IN), F32)], axis=1).astype(BF16)
    row = lambda v: v.reshape(1, -1)
    P["norm_mix"] = row(W["norm_mix"][i])
    P["mu"] = row(W["mu_shift"][i])
    P["w0"] = row(W["w0"][i])
    P["a0"] = row(W["a0"][i])
    z = lambda r: jnp.zeros((r, C_R), F32)
    P["wl_hi"], P["wl_lo"] = _hl(jnp.concatenate([W["w_lora"][i], z(96)], axis=0))
    P["al_hi"], P["al_lo"] = _hl(jnp.concatenate([z(32), W["a_lora"][i], z(64)], axis=0))
    P["gl_hi"], P["gl_lo"] = _hl(jnp.concatenate([z(64), W["g_lora"][i]], axis=0))
    P["k_k"] = row(W["k_k"][i])
    P["k_a"] = row(W["k_a"][i])
    P["r_k"] = row(W["r_k"][i])
    P["lnx_g"] = row(W["lnx_g"][i])
    P["lnx_b"] = row(W["lnx_b"][i])
    P["conv_w"] = W["conv_w"][i]
    P["conv_b"] = row(W["conv_b"][i])
    P["conv_ln_g"] = row(W["conv_ln_g"][i])
    P["conv_ln_b"] = row(W["conv_ln_b"][i])
    P["q_norm8"] = row(jnp.tile(W["q_norm"][i], H_A))
    P["k_norm2"] = row(jnp.tile(W["k_norm"][i], N_KV))
    P["kidx_norm"] = row(W["kidx_norm"][i])
    P["w_out"] = W["w_out"][i].astype(BF16)
    P["norm_ffn"] = row(W["norm_ffn"][i])
    P["ffn_up"] = W["ffn_up"][i].astype(BF16)
    P["ffn_conv_w"] = W["ffn_conv_w"][i]
    P["ffn_conv_b"] = row(W["ffn_conv_b"][i])
    P["ffn_down"] = W["ffn_down"][i].astype(BF16)
    P["ple_norm"] = row(W["ple_norm"][i])
    P["ple_gate"] = W["ple_gate"][i].astype(BF16)
    P["ple_proj"] = W["ple_proj"][i].astype(BF16)
    P["gseg4"] = _seg_ones(C_R, HD)
    P["gseg2"] = _seg_ones(N_KV * HD, HD)
    P["gseg8"] = _seg_ones(C_A, HD)
    P["tri"] = jnp.asarray(np.arange(LANES)[:, None] <= np.arange(LANES)[None, :], dtype=BF16)
    P["ones"] = jnp.ones((LANES, LANES), BF16)
    return P


def _pick_tile(n, pref):
    t = min(n, pref)
    while n % t:
        t //= 2
    return t


def _layer(layer, x, p_i, st, P, att_fn, bb):
    B, T, D = x.shape
    M = B * T
    shift0, wkv0, conv0, ffn0 = st
    x2 = x.reshape(M, D)
    tm = _pick_tile(M, 512)
    z2 = _in_proj(x2, P["norm_mix"], P["w_in"], tm)
    z3 = z2.reshape(B, T, NZ)
    Tc = _pick_tile(T, 256)
    o_rw, wkv_T = _rwkv(z3, shift0.reshape(B, 1, N_RWKV_COLS), wkv0, P, bb, Tc)
    conv0p = jnp.pad(conv0, ((0, 0), (HIST - (CONV_W - 1), 0), (0, 0)))
    o_cv, conv_Tp = _conv(z3, conv0p, P, _pick_tile(T, 512))
    kn2, kin2 = _att_prep(z2, P, tm)
    kn3 = kn2.reshape(B, T, N_KV * HD)
    kin3 = kin2.reshape(B, T, D_IDX)
    o_at = att_fn(layer, z3, kn3, kin3, P)
    xa = _out_proj(x2, o_rw.reshape(M, C_R), o_cv.reshape(M, C_CONV), o_at.reshape(M, C_A), P["w_out"], tm)
    tmf = M if M <= 1024 else _pick_tile(T, 1024)
    xo, ffn_T = _ffn(xa, p_i.reshape(M, -1), ffn0, P, tmf, B, T)
    k_new = kn3.reshape(B, T, N_KV, HD)
    v_new = z3[:, :, ZC_V * LANES:(ZC_V + 1) * LANES].reshape(B, T, N_KV, HD)
    shift_T = z3[:, T - 1, :N_RWKV_COLS]
    conv_T = conv_Tp[:, HIST - (CONV_W - 1):, :]
    return xo.reshape(B, T, D), (k_new, v_new, kin3, wkv_T, shift_T, conv_T, ffn_T)


def kernel(x_prompt, x_sample, cache_k, cache_v, cache_kidx, state_wkv, state_shift, state_conv, state_ffn, page_table, p_prompt, p_sample, norm_mix, w_in, w_out, mu_shift, w0, w_lora, a0, a_lora, g_lora, k_k, k_a, r_k, lnx_g, lnx_b, conv_w, conv_b, conv_ln_g, conv_ln_b, q_norm, k_norm, kidx_norm, rel_bias, norm_ffn, ffn_up, ffn_conv_w, ffn_conv_b, ffn_down, ple_norm, ple_proj, ple_gate):
    W = dict(norm_mix=norm_mix, w_in=w_in, w_out=w_out, mu_shift=mu_shift, w0=w0, w_lora=w_lora, a0=a0,
             a_lora=a_lora, g_lora=g_lora, k_k=k_k, k_a=k_a, r_k=r_k, lnx_g=lnx_g, lnx_b=lnx_b,
             conv_w=conv_w, conv_b=conv_b, conv_ln_g=conv_ln_g, conv_ln_b=conv_ln_b, q_norm=q_norm,
             k_norm=k_norm, kidx_norm=kidx_norm, norm_ffn=norm_ffn, ffn_up=ffn_up,
             ffn_conv_w=ffn_conv_w, ffn_conv_b=ffn_conv_b, ffn_down=ffn_down, ple_norm=ple_norm,
             ple_proj=ple_proj, ple_gate=ple_gate)
    depth = w_in.shape[0]
    params = [_layer_params(i, W) for i in range(depth)]

    B, T, D = x_prompt.shape
    tq = ATT_TQ if T % ATT_TQ == 0 else LANES
    topk_p = min(TOPK_MAX, T // 4)
    t_ = np.arange(LANES)[:, None]
    s_ = np.arange(LANES)[None, :]
    tiles = np.stack([_rel_bucket_np(np.maximum(t_ - s_, 0)), _rel_bucket_np(LANES + t_ - s_),
                      np.full((LANES, LANES), N_BUCKETS - 1, np.int32)])
    idx_p = np.broadcast_to(tiles[:, None], (3, H_A, LANES, LANES)).reshape(3 * H_A, LANES, LANES)
    bias_p = _bias_expand(rel_bias, jnp.asarray(idx_p)).reshape(3, H_A, LANES, LANES)

    def att_p(layer, z3, kn3, kin3, P):
        return _att_prompt(z3, kn3, kin3, P, bias_p, topk_p, tq)

    st0 = (jnp.zeros((B, N_RWKV_COLS), F32), jnp.zeros((B, H_R, HD, HD), F32),
           jnp.zeros((B, CONV_W - 1, C_CONV), F32), jnp.zeros((B, 2, D_FF), F32))
    x = x_prompt
    new_p = []
    for i in range(depth):
        x, new = _layer(i, x, p_prompt[i], st0, params[i], att_p, B)
        new_p.append(new)
    y_prompt = x

    Bs, Ts, _ = x_sample.shape
    npages = page_table.shape[1]
    psz = cache_k.shape[2]
    past_len = npages * psz
    topk_s = min(TOPK_MAX, (past_len + Ts) // 4)
    Ls = past_len + psz
    qpos = past_len + np.arange(Ts)[:, None]
    bk = _rel_bucket_np(np.maximum(qpos - np.arange(Ls)[None, :], 0))
    idx_s = np.broadcast_to(bk[None], (H_A, Ts, Ls))
    bias_s = _bias_expand(rel_bias, jnp.asarray(idx_s)).reshape(N_KV, G_A * Ts, Ls)
    ckT = jnp.transpose(cache_k, (0, 1, 3, 4, 2))
    cvT = jnp.transpose(cache_v, (0, 1, 3, 4, 2))
    ckiT = jnp.transpose(cache_kidx, (0, 1, 3, 2))

    def att_s(layer, z3, kn3, kin3, P):
        return _att_sample(layer, page_table, ckT, cvT, ckiT, z3, kn3, kin3, P, bias_s, topk_s)

    x = x_sample
    new_s = []
    for i in range(depth):
        st = (state_shift[i], state_wkv[i], state_conv[i], state_ffn[i])
        x, new = _layer(i, x, p_sample[i], st, params[i], att_s, 4)
        new_s.append(new)
    y_sample = x

    stack = lambda lst: tuple(jnp.stack(t) for t in zip(*lst))
    return (y_prompt, y_sample) + stack(new_p) + stack(new_s)
```

```python
import functools
import math

import numpy as np
import jax
import jax.numpy as jnp
from jax import lax
from jax.experimental import pallas as pl
from jax.experimental.pallas import tpu as pltpu

F32 = jnp.float32
BF16 = jnp.bfloat16
I32 = jnp.int32
I16 = jnp.int16

HD = 64
C_R = 256
H_R = C_R // HD
C_CONV = 256
CONV_W = 31
C_A = 512
H_A = C_A // HD
N_KV = 2
G_A = H_A // N_KV
H_I = 4
D_IDX = 64
TOPK_MAX = 256
N_BUCKETS = 32
MAX_DISTANCE = 128
D_FF = 2816
NORM_EPS = 1e-6
RWKV_LN_EPS = 64e-5
CONV_LN_EPS = 1e-5
N_RWKV_COLS = 896
N_IN = 2500

LANES = 128
SUBLANES = 8
VMEM_LIMIT = 56 * 1024 * 1024

NZ = 2688
ZC_CONV = 1024 // 512
ZC_Q = 1536 // 512
ZC_K = 2048 // 128
ZC_V = 2176 // 128
ZC_QI = 2304 // 256
ZC_KW = 2560 // 128

HIST = 32
ATT_ROWS = 1024
ATT_TQ = 256
NEG = -1e30
INT_MIN = -2 ** 31
I16_MIN = -2 ** 15
KEY_NEG_INF = int(np.array(0xFF800000 ^ 0x7FFFFFFF, dtype=np.uint32).view(np.int32))


def _cparams(sem):
    return pltpu.CompilerParams(dimension_semantics=sem, vmem_limit_bytes=VMEM_LIMIT)


def _split2(x):
    hi = x.astype(BF16)
    lo = (x - hi.astype(F32)).astype(BF16)
    return hi, lo


def _dot(a, b):
    return jnp.dot(a, b, preferred_element_type=F32)


def _dot_nt(a, b):
    return lax.dot_general(a, b, (((1,), (1,)), ((), ())), preferred_element_type=F32)


def _dot_exactw(x, w_bf):
    hi, lo = _split2(x)
    return _dot(hi, w_bf) + _dot(lo, w_bf)


def _dot_hl(x, w_hi, w_lo):
    hi, lo = _split2(x)
    return _dot(hi, w_hi) + (_dot(lo, w_hi) + _dot(hi, w_lo))


def _sigmoid(x):
    return 1.0 / (1.0 + jnp.exp(-x))


def _rms(x, g):
    ms = jnp.mean(x * x, axis=-1, keepdims=True)
    return x * lax.rsqrt(ms + NORM_EPS) * g


def _in_proj_kernel(x_ref, g_ref, w_ref, z_ref, xn_s):
    @pl.when(pl.program_id(1) == 0)
    def _():
        xn_s[...] = _rms(x_ref[...], g_ref[...]).astype(BF16)

    z_ref[...] = _dot(xn_s[...], w_ref[...])


def _in_proj(x2, g, w_bf, tm):
    M, D = x2.shape
    tn = NZ
    return pl.pallas_call(
        _in_proj_kernel,
        grid=(M // tm, NZ // tn),
        in_specs=[pl.BlockSpec((tm, D), lambda i, j: (i, 0)),
                  pl.BlockSpec((1, D), lambda i, j: (0, 0)),
                  pl.BlockSpec((D, tn), lambda i, j: (0, j))],
        out_specs=pl.BlockSpec((tm, tn), lambda i, j: (i, j)),
        out_shape=jax.ShapeDtypeStruct((M, NZ), F32),
        scratch_shapes=[pltpu.VMEM((tm, D), BF16)],
        compiler_params=_cparams(("parallel", "arbitrary")),
        name="in_proj",
    )(x2, g, w_bf)


def _rwkv_kernel(z_ref, sh0_ref, wkv0_ref, mu_ref, w0_ref, wl_hi, wl_lo, a0_ref, al_hi, al_lo, gl_hi, gl_lo,
                 kk_ref, ka_ref, rk_ref, lg_ref, lb_ref, gseg_ref,
                 o_ref, st_ref,
                 S_s, carry_s, w_s, k_s, v_s, a_s, b_s, r_s, y_s, bon_s, g_s, *, bb, Tc):
    c = pl.program_id(1)
    NP = H_R // 2

    @pl.when(c == 0)
    def _():
        for b in range(bb):
            for p in range(NP):
                S_s[b, p] = jnp.concatenate([wkv0_ref[b, 2 * p], wkv0_ref[b, 2 * p + 1]], axis=1)
        carry_s[...] = sh0_ref[...]

    gseg = gseg_ref[...]
    mu = mu_ref[...]
    rid = lax.broadcasted_iota(I32, (Tc, N_RWKV_COLS), 0)
    for b in range(bb):
        z = z_ref[b]
        prev = jnp.where(rid == 0, carry_s[b], pltpu.roll(z, 1, 0))
        carry_s[b] = z[Tc - 1:Tc, :]
        zs = z + (prev - z) * mu
        r = zs[:, 0:C_R]
        k = zs[:, C_R:2 * C_R]
        v = zs[:, 2 * C_R:3 * C_R]
        t6 = zs[:, 3 * C_R:]
        lw = w0_ref[...] + _dot_hl(jnp.tanh(t6), wl_hi[...], wl_lo[...])
        nl = -lw
        softplus = jnp.maximum(nl, 0.0) + jnp.log(1.0 + jnp.exp(-jnp.abs(nl)))
        decay = jnp.exp(-jnp.exp(-softplus - 0.5))
        a = _sigmoid(a0_ref[...] + _dot_hl(t6, al_hi[...], al_lo[...]))
        g_s[b] = _dot_hl(_sigmoid(t6), gl_hi[...], gl_lo[...])
        kk = k * kk_ref[...]
        nrm = jnp.sqrt(_dot_exactw(kk * kk, gseg))
        kk = kk / jnp.maximum(nrm, 1e-12)
        k2 = k * (1.0 + (a - 1.0) * ka_ref[...])
        bon_s[b] = _dot_exactw(r * k2 * rk_ref[...], gseg) * v
        na = -kk
        kb = kk * a
        for p in range(NP):
            sl = slice(p * 2 * HD, (p + 1) * 2 * HD)
            w_s[b, p] = decay[:, sl]
            k_s[b, p] = k2[:, sl]
            v_s[b, p] = v[:, sl]
            a_s[b, p] = na[:, sl]
            b_s[b, p] = kb[:, sl]
            r_s[b, p] = r[:, sl]

    W2 = 2 * HD
    i0 = lax.broadcasted_iota(I32, (HD, W2), 0)
    i1 = lax.broadcasted_iota(I32, (HD, W2), 1)
    eye_a = (i1 == i0).astype(F32)
    eye_b = (i1 == i0 + HD).astype(F32)
    lane_lo = i1 < HD
    j0 = lax.broadcasted_iota(I32, (W2, W2), 0)
    j1 = lax.broadcasted_iota(I32, (W2, W2), 1)
    ones_bd = ((j0 < HD) == (j1 < HD)).astype(F32).astype(BF16)
    s0 = lax.broadcasted_iota(I32, (SUBLANES, W2), 0)
    s1 = lax.broadcasted_iota(I32, (SUBLANES, W2), 1)
    rsel = (((s0 == 0) & (s1 < HD)) | ((s0 == 1) & (s1 >= HD))).astype(F32)
    chains = [(b, p) for b in range(bb) for p in range(NP)]

    def sub(i, carry):
        t0 = pl.multiple_of(i * SUBLANES, SUBLANES)
        S = [S_s[b, p] for b, p in chains]

        def emit_y(t, Sb):
            for ci, (b, p) in enumerate(chains):
                rm = (r_s[b, p, pl.ds(t, 1), :] * rsel).astype(BF16)
                ym = _dot_nt(rm, Sb[ci])
                y_s[b, 2 * p, pl.ds(t, 1), :] = ym[0:1, :]
                y_s[b, 2 * p + 1, pl.ds(t, 1), :] = ym[1:2, :]

        pending = None
        for j in range(SUBLANES):
            t = t0 + j
            row = lambda ref, ci: ref[chains[ci][0], chains[ci][1], pl.ds(t, 1), :]
            n = len(chains)
            sa = [_dot((S[ci] * row(a_s, ci)).astype(BF16), ones_bd) for ci in range(n)]
            if pending is not None:
                emit_y(*pending)
            for ci in range(n):
                v = row(v_s, ci)
                vc = jnp.where(lane_lo, jnp.sum(eye_a * v, axis=1, keepdims=True),
                               jnp.sum(eye_b * v, axis=1, keepdims=True))
                S[ci] = S[ci] * row(w_s, ci) + sa[ci] * row(b_s, ci) + vc * row(k_s, ci)
            pending = (t, [s.astype(BF16) for s in S])
        emit_y(*pending)
        for ci, (b, p) in enumerate(chains):
            S_s[b, p] = S[ci]
        return carry

    lax.fori_loop(0, Tc // SUBLANES, sub, 0)

    for b in range(bb):
        y = jnp.concatenate([y_s[b, h] for h in range(H_R)], axis=-1)
        mean = _dot_exactw(y, gseg) * (1.0 / HD)
        d = y - mean
        var = _dot_exactw(d * d, gseg) * (1.0 / HD)
        yn = d * lax.rsqrt(var + RWKV_LN_EPS) * lg_ref[...] + lb_ref[...]
        o_ref[b] = (yn + bon_s[b]) * g_s[b]

    @pl.when(c == pl.num_programs(1) - 1)
    def _():
        for b in range(bb):
            for p in range(NP):
                s2 = S_s[b, p]
                st_ref[b, 2 * p] = s2[:, 0:HD]
                st_ref[b, 2 * p + 1] = s2[:, HD:]


def _rwkv(z3, shift0, wkv0, P, bb, Tc):
    B, T, _ = z3.shape
    vec = lambda n: pl.BlockSpec((1, n), lambda bi, c: (0, 0))
    mat = lambda r, n: pl.BlockSpec((r, n), lambda bi, c: (0, 0))
    kern = functools.partial(_rwkv_kernel, bb=bb, Tc=Tc)
    hv = lambda: pltpu.VMEM((bb, H_R // 2, Tc, 2 * HD), F32)
    return pl.pallas_call(
        kern,
        grid=(B // bb, T // Tc),
        in_specs=[pl.BlockSpec((bb, Tc, N_RWKV_COLS), lambda bi, c: (bi, c, 0)),
                  pl.BlockSpec((bb, 1, N_RWKV_COLS), lambda bi, c: (bi, 0, 0)),
                  pl.BlockSpec((bb, H_R, HD, HD), lambda bi, c: (bi, 0, 0, 0)),
                  vec(N_RWKV_COLS), vec(C_R), mat(LANES, C_R), mat(LANES, C_R), vec(C_R), mat(LANES, C_R),
                  mat(LANES, C_R), mat(LANES, C_R), mat(LANES, C_R),
                  vec(C_R), vec(C_R), vec(C_R), vec(C_R), vec(C_R), mat(C_R, C_R)],
        out_specs=[pl.BlockSpec((bb, Tc, C_R), lambda bi, c: (bi, c, 0)),
                   pl.BlockSpec((bb, H_R, HD, HD), lambda bi, c: (bi, 0, 0, 0))],
        out_shape=[jax.ShapeDtypeStruct((B, T, C_R), F32),
                   jax.ShapeDtypeStruct((B, H_R, HD, HD), F32)],
        scratch_shapes=[pltpu.VMEM((bb, H_R // 2, HD, 2 * HD), F32),
                        pltpu.VMEM((bb, 1, N_RWKV_COLS), F32),
                        hv(), hv(), hv(), hv(), hv(), hv(),
                        pltpu.VMEM((bb, H_R, Tc, HD), F32),
                        pltpu.VMEM((bb, Tc, C_R), F32),
                        pltpu.VMEM((bb, Tc, C_R), F32)],
        compiler_params=_cparams(("parallel", "arbitrary")),
        name="rwkv7",
    )(z3, shift0, wkv0, P["mu"], P["w0"], P["wl_hi"], P["wl_lo"], P["a0"], P["al_hi"], P["al_lo"],
      P["gl_hi"], P["gl_lo"], P["k_k"], P["k_a"], P["r_k"], P["lnx_g"], P["lnx_b"], P["gseg4"])


def _conv_kernel(z_ref, c0_ref, w_ref, b_ref, lg_ref, lb_ref, o_ref, ct_ref, buf, *, Tc):
    @pl.when(pl.program_id(1) == 0)
    def _():
        buf[0:HIST, :] = c0_ref[0]

    z = z_ref[0]
    u = z[:, 0:C_CONV] * _sigmoid(z[:, C_CONV:])
    buf[HIST:HIST + Tc, :] = u
    w = w_ref[...]
    acc = jnp.zeros((Tc, C_CONV), F32) + b_ref[...]
    off = HIST - (CONV_W - 1)
    for j in range(CONV_W):
        acc = acc + w[j:j + 1, :] * buf[off + j:off + j + Tc, :]
    mu = jnp.mean(acc, axis=-1, keepdims=True)
    d = acc - mu
    var = jnp.mean(d * d, axis=-1, keepdims=True)
    cn = d * lax.rsqrt(var + CONV_LN_EPS) * lg_ref[...] + lb_ref[...]
    o_ref[0] = cn * _sigmoid(cn)
    hist = buf[Tc:Tc + HIST, :]
    buf[0:HIST, :] = hist
    ct_ref[0] = hist


def _conv(z3, conv0p, P, Tc):
    B, T, _ = z3.shape
    vec = pl.BlockSpec((1, C_CONV), lambda b, c: (0, 0))
    return pl.pallas_call(
        functools.partial(_conv_kernel, Tc=Tc),
        grid=(B, T // Tc),
        in_specs=[pl.BlockSpec((1, Tc, 2 * C_CONV), lambda b, c: (b, c, ZC_CONV)),
                  pl.BlockSpec((1, HIST, C_CONV), lambda b, c: (b, 0, 0)),
                  pl.BlockSpec((CONV_W, C_CONV), lambda b, c: (0, 0)),
                  vec, vec, vec],
        out_specs=[pl.BlockSpec((1, Tc, C_CONV), lambda b, c: (b, c, 0)),
                   pl.BlockSpec((1, HIST, C_CONV), lambda b, c: (b, 0, 0))],
        out_shape=[jax.ShapeDtypeStruct((B, T, C_CONV), F32),
                   jax.ShapeDtypeStruct((B, HIST, C_CONV), F32)],
        scratch_shapes=[pltpu.VMEM((HIST + Tc, C_CONV), F32)],
        compiler_params=_cparams(("parallel", "arbitrary")),
        name="conformer_conv",
    )(z3, conv0p, P["conv_w"], P["conv_b"], P["conv_ln_g"], P["conv_ln_b"])


def _att_prep_kernel(k_ref, kw_ref, kg_ref, kig_ref, gseg_ref, kn_ref, kin_ref):
    k = k_ref[...]
    ms = _dot_exactw(k * k, gseg_ref[...]) * (1.0 / HD)
    kn_ref[...] = k * lax.rsqrt(ms + NORM_EPS) * kg_ref[...]
    kw = kw_ref[...]
    ki = kw[:, 0:D_IDX]
    msi = jnp.mean(ki * ki, axis=-1, keepdims=True)
    kin_ref[...] = ki * lax.rsqrt(msi + NORM_EPS) * kig_ref[...]


def _att_prep(z2, P, tm):
    M = z2.shape[0]
    return pl.pallas_call(
        _att_prep_kernel,
        grid=(M // tm,),
        in_specs=[pl.BlockSpec((tm, LANES), lambda i: (i, ZC_K)),
                  pl.BlockSpec((tm, LANES), lambda i: (i, ZC_KW)),
                  pl.BlockSpec((1, LANES), lambda i: (0, 0)),
                  pl.BlockSpec((1, D_IDX), lambda i: (0, 0)),
                  pl.BlockSpec((LANES, LANES), lambda i: (0, 0))],
        out_specs=[pl.BlockSpec((tm, LANES), lambda i: (i, 0)),
                   pl.BlockSpec((tm, D_IDX), lambda i: (i, 0))],
        out_shape=[jax.ShapeDtypeStruct((M, N_KV * HD), F32),
                   jax.ShapeDtypeStruct((M, D_IDX), F32)],
        compiler_params=_cparams(("parallel",)),
        name="att_prep",
    )(z2, z2, P["k_norm2"], P["kidx_norm"], P["gseg2"])


def _bias_kernel(rb_ref, idx_ref, o_ref):
    h = pl.program_id(0) % H_A
    idx = idx_ref[0]
    out = jnp.zeros(idx.shape, F32)
    for bk in range(N_BUCKETS):
        out = jnp.where(idx == bk, rb_ref[bk, h], out)
    o_ref[0] = out


def _bias_expand(rel_bias, idx):
    S, R, L = idx.shape
    return pl.pallas_call(
        _bias_kernel,
        grid=(S,),
        in_specs=[pl.BlockSpec(memory_space=pltpu.SMEM),
                  pl.BlockSpec((1, R, L), lambda s: (s, 0, 0))],
        out_specs=pl.BlockSpec((1, R, L), lambda s: (s, 0, 0)),
        out_shape=jax.ShapeDtypeStruct((S, R, L), F32),
        compiler_params=_cparams(("parallel",)),
        name="rel_bias_tiles",
    )(rel_bias, idx)


def _rel_bucket_np(dist):
    max_exact = N_BUCKETS // 2
    d_f = np.maximum(dist, max_exact).astype(np.float32)
    large = max_exact + (np.log(d_f / np.float32(max_exact)) / np.float32(math.log(MAX_DISTANCE / max_exact))
                         * np.float32(N_BUCKETS - max_exact)).astype(np.int32)
    return np.where(dist < max_exact, dist, np.minimum(large, N_BUCKETS - 1)).astype(np.int32)


def _sort_key(s):
    bits = pltpu.bitcast(s + 0.0, I32)
    return jnp.where(bits < 0, bits ^ 0x7FFFFFFF, bits)


def _q_prep(q, qg, gseg8):
    ms = _dot_exactw(q * q, gseg8) * (1.0 / HD)
    return (q * lax.rsqrt(ms + NORM_EPS) * qg * (HD ** -0.5)).astype(BF16)


def _attp_kernel(q_ref, qi_ref, kw_ref, kn_ref, v_ref, kin_ref, qg_ref, bias_ref, gseg_ref, tri_ref, ones_ref,
                 o_ref,
                 kk_s, vv_s, ki_s, qs_s, qi_s, wb_s, key_s, hi_s, lo_s, tau_s, madd_s, s_s, m_s, l_s, acc_s,
                 *, tq, topk, kg, ag):
    j = pl.program_id(1)
    nh = tq // LANES
    hp = m_s.shape[0] // tq
    nb = (j + 1) * nh
    KW = kg * LANES
    AW = ag * LANES
    ngk = (nb + kg - 1) // kg
    nga = (nb + ag - 1) // ag
    R = hp * tq

    @pl.when(j == 0)
    def _():
        knT = jnp.transpose(kn_ref[0])
        v = v_ref[0]
        for n in range(N_KV):
            kk_s[n] = knT[n * HD:(n + 1) * HD, :].astype(BF16)
            vv_s[n] = v[:, n * HD:(n + 1) * HD].astype(BF16)
        kin = kin_ref[0]
        kinT = jnp.transpose(jnp.concatenate([kin, jnp.zeros(kin.shape, F32)], axis=1))
        ki_s[...] = kinT[0:D_IDX, :].astype(BF16)

    qn = _q_prep(q_ref[0], qg_ref[...], gseg_ref[...])
    for n in range(N_KV):
        qs_s[n] = jnp.concatenate([qn[:, (n * G_A + g) * HD:(n * G_A + g + 1) * HD] for g in range(G_A)], axis=0)
    qi = (qi_ref[0] * (D_IDX ** -0.5)).astype(BF16)
    kw = kw_ref[0]
    for h in range(H_I):
        qi_s[h] = qi[:, h * D_IDX:(h + 1) * D_IDX]
        wb_s[h] = jnp.broadcast_to(kw[:, D_IDX + h:D_IDX + h + 1] * (H_I ** -0.5), (tq, KW))

    row = lax.broadcasted_iota(I32, (tq, KW), 0) + j * tq
    col0 = lax.broadcasted_iota(I32, (tq, KW), 1)

    def idx_body(gi, carry):
        off = pl.multiple_of(gi * KW, KW)
        kc = ki_s[:, pl.ds(off, KW)]
        s = jnp.zeros((tq, KW), F32)
        for h in range(H_I):
            s = s + jnp.maximum(_dot(qi_s[h], kc), 0.0) * wb_s[h]
        s = jnp.where(col0 + off <= row, s, -jnp.inf)
        key = _sort_key(s)
        key_s[:, pl.ds(off, KW)] = key
        hi_s[:, pl.ds(off, KW)] = lax.shift_right_arithmetic(key, 16).astype(I16)
        lo_s[:, pl.ds(off, KW)] = ((key & 0xFFFF) + I16_MIN).astype(I16)
        return carry

    lax.fori_loop(0, ngk, idx_body, 0)

    def key_tiles(gi, ref=key_s):
        off = pl.multiple_of(gi * KW, KW)
        key = ref[:, pl.ds(off, KW)]
        return [key[:, u * LANES:(u + 1) * LANES] for u in range(kg)]

    def lane_total(acc):
        return jnp.broadcast_to(jnp.sum(acc.astype(F32), axis=1, keepdims=True), (tq, LANES))

    one16 = jnp.ones((tq, LANES), I16)
    zero16 = jnp.zeros((tq, LANES), I16)

    def search16(ref, kneed):
        for ng in range(1, key_s.shape[1] // KW + 1):
            @pl.when(ngk == ng)
            def _(ng=ng):
                def bit_body(it, cur):
                    cand = cur + lax.shift_left(jnp.int32(1), jnp.asarray(15 - it, dtype=I32))
                    c16 = cand.astype(I16)
                    accs = [zero16] * kg
                    for gi in range(ng):
                        for u in range(kg):
                            kt = ref[:, gi * KW + u * LANES:gi * KW + (u + 1) * LANES]
                            accs[u] = accs[u] + jnp.where(kt >= c16, one16, zero16)
                    acc = accs[0]
                    for a in accs[1:]:
                        acc = acc + a
                    return jnp.where(lane_total(acc) >= kneed, cand, cur)

                tau_s[...] = lax.fori_loop(0, 16, bit_body, jnp.full((tq, LANES), I16_MIN, I32))

        return tau_s[...]

    tau_hi = search16(hi_s, float(topk))
    th16 = tau_hi.astype(I16)
    lo_pad = jnp.full((tq, LANES), I16_MIN, I16)

    def mid_body(gi, acc):
        off = pl.multiple_of(gi * KW, KW)
        los = key_tiles(gi, lo_s)
        for u, ht in enumerate(key_tiles(gi, hi_s)):
            acc = acc + jnp.where(ht > th16, one16, zero16)
            lo_s[:, pl.ds(off + u * LANES, LANES)] = jnp.where(ht == th16, los[u], lo_pad)
        return acc

    need_lo = float(topk) - lane_total(lax.fori_loop(0, ngk, mid_body, zero16))
    tau_lo = search16(lo_s, need_lo)
    tau = lax.shift_left(tau_hi, 16) + (tau_lo - I16_MIN)

    def cnt2_body(gi, carry):
        cge, cgt = carry
        for kt in key_tiles(gi):
            cge = cge + jnp.where(kt >= tau, 1.0, 0.0)
            cgt = cgt + jnp.where(kt > tau, 1.0, 0.0)
        return cge, cgt

    zero = jnp.zeros((tq, LANES), F32)
    cge, cgt = lax.fori_loop(0, ngk, cnt2_body, (zero, zero))
    need = float(topk) - lane_total(cgt)
    ties = jnp.max(lane_total(cge)) > float(topk)

    @pl.when(jnp.logical_not(ties))
    def _():
        def body(gi, carry):
            off = pl.multiple_of(gi * KW, KW)
            for u, kt in enumerate(key_tiles(gi)):
                take = jnp.where(kt >= tau, jnp.where(kt > KEY_NEG_INF, 0.0, NEG), NEG)
                madd_s[:, pl.ds(off + u * LANES, LANES)] = take
            return carry

        lax.fori_loop(0, ngk, body, 0)

    @pl.when(ties)
    def _():
        tri = tri_ref[...]
        ones = ones_ref[...]

        def body(c, run):
            off = pl.multiple_of(c * LANES, LANES)
            key = key_s[:, pl.ds(off, LANES)]
            eq = jnp.where(key == tau, 1.0, 0.0)
            eqb = eq.astype(BF16)
            pre = _dot(eqb, tri) + run
            take = jnp.where(key > tau, 1.0, jnp.where(pre <= need, eq, 0.0))
            take = jnp.where(key > KEY_NEG_INF, take, 0.0)
            madd_s[:, pl.ds(off, LANES)] = jnp.where(take > 0.5, 0.0, NEG)
            return run + _dot(eqb, ones)

        lax.fori_loop(0, ngk * kg, body, jnp.zeros((tq, LANES), F32))

    slab_list = [(gl, hf) for gl in range(hp) for hf in range(nh)]
    outs = []
    for n in range(N_KV):
      for gp in range(G_A // hp):
        m_s[...] = jnp.full((R, LANES), NEG, F32)
        l_s[...] = jnp.zeros((R, LANES), F32)
        acc_s[...] = jnp.zeros((R, HD), F32)

        def pass_a(gi, carry, n=n, gp=gp, near=True):
            off = pl.multiple_of(gi * AW, AW)
            kc = kk_s[n, :, pl.ds(off, AW)]
            s_all = _dot(qs_s[n, gp * R:(gp + 1) * R, :], kc)
            for gl, hf in slab_list:
                r0 = gl * tq + hf * LANES
                rows = pl.ds(r0, LANES)
                head = n * G_A + gp * hp + gl
                mx = m_s[rows, :]
                for u in range(ag):
                    cols = pl.ds(off + u * LANES, LANES)
                    su = s_all[r0:r0 + LANES, u * LANES:(u + 1) * LANES] + madd_s[hf * LANES:(hf + 1) * LANES, cols]
                    if near:
                        dd = jnp.clip(j * nh + hf - (gi * ag + u), 0, 2)
                        su = su + bias_ref[dd, head]
                    s_s[rows, cols] = su
                    mx = jnp.maximum(mx, su)
                m_s[rows, :] = mx
            return carry

        n_far = jnp.maximum(nb - nh - 1, 0) // ag
        lax.fori_loop(0, n_far, functools.partial(pass_a, near=False), 0)
        lax.fori_loop(n_far, nga, pass_a, 0)
        mb = jnp.broadcast_to(jnp.max(m_s[...], axis=1, keepdims=True), (R, LANES))
        m_s[...] = mb

        def pass_b(gi, carry, n=n):
            off = pl.multiple_of(gi * AW, AW)
            slabs = []
            for gl, hf in slab_list:
                rows = pl.ds(gl * tq + hf * LANES, LANES)
                mrow = m_s[rows, :]
                lacc = l_s[rows, :]
                ps = []
                for u in range(ag):
                    p = jnp.exp(s_s[rows, pl.ds(off + u * LANES, LANES)] - mrow)
                    lacc = lacc + p
                    ps.append(p.astype(BF16))
                l_s[rows, :] = lacc
                slabs.append(ps[0] if ag == 1 else jnp.concatenate(ps, axis=1))
            acc_s[...] += _dot(jnp.concatenate(slabs, axis=0), vv_s[n, pl.ds(off, AW), :])
            return carry

        lax.fori_loop(0, nga, pass_b, 0)
        o = acc_s[...] / jnp.sum(l_s[...], axis=1, keepdims=True)
        for gl in range(hp):
            outs.append(o[gl * tq:(gl + 1) * tq, :])
    o_ref[0] = jnp.concatenate(outs, axis=-1)


def _att_prompt(z3, kn3, kin3, P, bias_p, topk, tq):
    B, T, _ = z3.shape
    nblk = T // LANES
    kg = 4 if nblk % 4 == 0 else (2 if nblk % 2 == 0 else 1)
    ag = kg
    rp = min(G_A * tq, ATT_ROWS)
    cst = lambda shp: pl.BlockSpec(shp, lambda b, j: tuple(0 for _ in shp))
    return pl.pallas_call(
        functools.partial(_attp_kernel, tq=tq, topk=topk, kg=kg, ag=ag),
        grid=(B, T // tq),
        in_specs=[pl.BlockSpec((1, tq, C_A), lambda b, j: (b, j, ZC_Q)),
                  pl.BlockSpec((1, tq, H_I * D_IDX), lambda b, j: (b, j, ZC_QI)),
                  pl.BlockSpec((1, tq, LANES), lambda b, j: (b, j, ZC_KW)),
                  pl.BlockSpec((1, T, N_KV * HD), lambda b, j: (b, 0, 0)),
                  pl.BlockSpec((1, T, N_KV * HD), lambda b, j: (b, 0, ZC_V)),
                  pl.BlockSpec((1, T, D_IDX), lambda b, j: (b, 0, 0)),
                  cst((1, C_A)), cst((3, H_A, LANES, LANES)), cst((C_A, C_A)), cst((LANES, LANES)),
                  cst((LANES, LANES))],
        out_specs=pl.BlockSpec((1, tq, C_A), lambda b, j: (b, j, 0)),
        out_shape=jax.ShapeDtypeStruct((B, T, C_A), F32),
        scratch_shapes=[pltpu.VMEM((N_KV, HD, T), BF16),
                        pltpu.VMEM((N_KV, T, HD), BF16),
                        pltpu.VMEM((D_IDX, T), BF16),
                        pltpu.VMEM((N_KV, G_A * tq, HD), BF16),
                        pltpu.VMEM((H_I, tq, D_IDX), BF16),
                        pltpu.VMEM((H_I, tq, kg * LANES), F32),
                        pltpu.VMEM((tq, T), I32),
                        pltpu.VMEM((tq, T), I16),
                        pltpu.VMEM((tq, T), I16),
                        pltpu.VMEM((tq, LANES), I32),
                        pltpu.VMEM((tq, T), F32),
                        pltpu.VMEM((rp, T), F32),
                        pltpu.VMEM((rp, LANES), F32),
                        pltpu.VMEM((rp, LANES), F32),
                        pltpu.VMEM((rp, HD), F32)],
        compiler_params=_cparams(("parallel", "arbitrary")),
        name="dsa_prompt",
    )(z3, z3, z3, kn3, z3, kin3, P["q_norm8"], bias_p, P["gseg8"], P["tri"], P["ones"])


def _pad_transpose(x):
    r, c = x.shape
    if c < LANES:
        x = jnp.concatenate([x, jnp.zeros((r, LANES - c), F32)], axis=1)
    x = jnp.concatenate([x, jnp.zeros((LANES - r, LANES), F32)], axis=0)
    return jnp.transpose(x)


def _atts_kernel(pt_ref, ckT_ref, cvT_ref, ckiT_ref, q_ref, qi_ref, kw_ref, kn_ref, vn_ref, kin_ref, qg_ref,
                 bias_ref, gseg_ref, tri_ref, ones_ref,
                 o_ref,
                 kT_s, vT_s, kiT_s, madd_s, sem, *, layer, npages, nbatch, Ts, topk):
    b = pl.program_id(0)
    slot = b % 2
    P = LANES
    L = (npages + 1) * P

    def page_copies(seq, sl, p):
        pg = pt_ref[seq, p]
        col = pl.multiple_of(p * P, P)
        return (pltpu.make_async_copy(ckT_ref.at[layer, pg], kT_s.at[sl, :, :, pl.ds(col, P)], sem.at[sl, 0]),
                pltpu.make_async_copy(cvT_ref.at[layer, pg], vT_s.at[sl, :, :, pl.ds(col, P)], sem.at[sl, 1]),
                pltpu.make_async_copy(ckiT_ref.at[layer, pg], kiT_s.at[sl, :, pl.ds(col, P)], sem.at[sl, 2]))

    def start_pages(seq, sl):
        def body(p, carry):
            for cp in page_copies(seq, sl, p):
                cp.start()
            return carry
        lax.fori_loop(0, npages, body, 0)

    def wait_pages(seq, sl):
        def body(p, carry):
            for cp in page_copies(seq, sl, p):
                cp.wait()
            return carry
        lax.fori_loop(0, npages, body, 0)

    @pl.when(b == 0)
    def _():
        start_pages(0, 0)

    @pl.when(b + 1 < nbatch)
    def _():
        start_pages(b + 1, 1 - slot)

    knT = _pad_transpose(kn_ref[0])
    vnT = _pad_transpose(vn_ref[0])
    for n in range(N_KV):
        kT_s[slot, n, :, npages * P:L] = knT[n * HD:(n + 1) * HD, :]
        vT_s[slot, n, :, npages * P:L] = vnT[n * HD:(n + 1) * HD, :]
    kiT_s[slot, :, npages * P:L] = _pad_transpose(kin_ref[0])[0:D_IDX, :]

    qn = _q_prep(q_ref[0], qg_ref[...], gseg_ref[...])
    qi = (qi_ref[0] * (D_IDX ** -0.5)).astype(BF16)
    qis = jnp.concatenate([qi[:, h * D_IDX:(h + 1) * D_IDX] for h in range(H_I)], axis=0)
    kw = kw_ref[0]

    wait_pages(b, slot)

    d = _dot(qis, kiT_s[slot].astype(BF16))
    s = jnp.zeros((Ts, L), F32)
    for h in range(H_I):
        s = s + jnp.maximum(d[h * Ts:(h + 1) * Ts, :], 0.0) * (kw[:, D_IDX + h:D_IDX + h + 1] * (H_I ** -0.5))
    col = lax.broadcasted_iota(I32, (Ts, L), 1)
    row = lax.broadcasted_iota(I32, (Ts, L), 0) + npages * P
    key = _sort_key(jnp.where(col <= row, s, -jnp.inf))

    def bit_body(it, cur):
        step = lax.shift_left(jnp.int32(1), jnp.asarray(30 - 2 * it, dtype=I32))
        for mult in (1, 2, 3):
            cand = cur + mult * step
            cnt = jnp.sum(jnp.where(key >= cand, 1.0, 0.0), axis=1, keepdims=True)
            new = jnp.where(cnt >= float(topk), cand, cur) if mult == 1 else jnp.where(cnt >= float(topk), cand, new)
        return new

    tau = lax.fori_loop(0, 16, bit_body, jnp.full((Ts, 1), INT_MIN, I32))
    need = float(topk) - jnp.sum(jnp.where(key > tau, 1.0, 0.0), axis=1, keepdims=True)
    cge = jnp.sum(jnp.where(key >= tau, 1.0, 0.0), axis=1, keepdims=True)
    ties = jnp.max(cge) > float(topk)

    @pl.when(jnp.logical_not(ties))
    def _():
        madd_s[...] = jnp.where(key >= tau, jnp.where(key > KEY_NEG_INF, 0.0, NEG), NEG)

    @pl.when(ties)
    def _():
        tri = tri_ref[...]
        ones = ones_ref[...]
        eq = jnp.where(key == tau, 1.0, 0.0)
        run = jnp.zeros((Ts, LANES), F32)
        for c in range(L // LANES):
            sl = slice(c * LANES, (c + 1) * LANES)
            eqc = eq[:, sl]
            eqb = eqc.astype(BF16)
            pre = _dot(eqb, tri) + run
            take = jnp.where(key[:, sl] > tau, 1.0, jnp.where(pre <= need, eqc, 0.0))
            take = jnp.where(key[:, sl] > KEY_NEG_INF, take, 0.0)
            madd_s[:, sl] = jnp.where(take > 0.5, 0.0, NEG)
            run = run + _dot(eqb, ones)

    madd = madd_s[...]
    madd4 = jnp.concatenate([madd] * G_A, axis=0)
    outs = []
    for n in range(N_KV):
        qs = jnp.concatenate([qn[:, (n * G_A + g) * HD:(n * G_A + g + 1) * HD] for g in range(G_A)], axis=0)
        sc = _dot(qs, kT_s[slot, n].astype(BF16)) + bias_ref[n] + madd4
        m = jnp.max(sc, axis=1, keepdims=True)
        pe = jnp.exp(sc - m)
        l = jnp.sum(pe, axis=1, keepdims=True)
        o = _dot_nt(pe.astype(BF16), vT_s[slot, n].astype(BF16)) / l
        for g in range(G_A):
            outs.append(o[g * Ts:(g + 1) * Ts, :])
    o_ref[0] = jnp.concatenate(outs, axis=-1)


def _att_sample(layer, page_table, ckT, cvT, ckiT, z3, kn3, kin3, P, bias_s, topk):
    B, Ts, _ = z3.shape
    npages = page_table.shape[1]
    L = (npages + 1) * LANES
    hbm = pl.BlockSpec(memory_space=pl.ANY)
    cst = lambda shp: pl.BlockSpec(shp, lambda b, pt: tuple(0 for _ in shp))
    grid_spec = pltpu.PrefetchScalarGridSpec(
        num_scalar_prefetch=1,
        grid=(B,),
        in_specs=[hbm, hbm, hbm,
                  pl.BlockSpec((1, Ts, C_A), lambda b, pt: (b, 0, ZC_Q)),
                  pl.BlockSpec((1, Ts, H_I * D_IDX), lambda b, pt: (b, 0, ZC_QI)),
                  pl.BlockSpec((1, Ts, LANES), lambda b, pt: (b, 0, ZC_KW)),
                  pl.BlockSpec((1, Ts, N_KV * HD), lambda b, pt: (b, 0, 0)),
                  pl.BlockSpec((1, Ts, N_KV * HD), lambda b, pt: (b, 0, ZC_V)),
                  pl.BlockSpec((1, Ts, D_IDX), lambda b, pt: (b, 0, 0)),
                  cst((1, C_A)), cst((N_KV, G_A * Ts, L)), cst((C_A, C_A)), cst((LANES, LANES)),
                  cst((LANES, LANES))],
        out_specs=pl.BlockSpec((1, Ts, C_A), lambda b, pt: (b, 0, 0)),
        scratch_shapes=[pltpu.VMEM((2, N_KV, HD, L), F32),
                        pltpu.VMEM((2, N_KV, HD, L), F32),
                        pltpu.VMEM((2, D_IDX, L), F32),
                        pltpu.VMEM((Ts, L), F32),
                        pltpu.SemaphoreType.DMA((2, 3))])
    return pl.pallas_call(
        functools.partial(_atts_kernel, layer=layer, npages=npages, nbatch=B, Ts=Ts, topk=topk),
        grid_spec=grid_spec,
        out_shape=jax.ShapeDtypeStruct((B, Ts, C_A), F32),
        compiler_params=_cparams(("arbitrary",)),
        name="dsa_sample",
    )(page_table, ckT, cvT, ckiT, z3, z3, z3, kn3, z3, kin3, P["q_norm8"], bias_s, P["gseg8"], P["tri"], P["ones"])


def _out_proj_kernel(x_ref, orw_ref, ocv_ref, oat_ref, w_ref, y_ref):
    acc = _dot(orw_ref[...].astype(BF16), w_ref[0:C_R, :])
    acc = acc + _dot(ocv_ref[...].astype(BF16), w_ref[C_R:C_R + C_CONV, :])
    acc = acc + _dot(oat_ref[...].astype(BF16), w_ref[C_R + C_CONV:, :])
    y_ref[...] = x_ref[...] + acc


def _out_proj(x2, orw, ocv, oat, w_bf, tm):
    M, D = x2.shape
    row = lambda n: pl.BlockSpec((tm, n), lambda i: (i, 0))
    return pl.pallas_call(
        _out_proj_kernel,
        grid=(M // tm,),
        in_specs=[row(D), row(C_R), row(C_CONV), row(C_A), pl.BlockSpec((D, D), lambda i: (0, 0))],
        out_specs=row(D),
        out_shape=jax.ShapeDtypeStruct((M, D), F32),
        compiler_params=_cparams(("parallel",)),
        name="out_proj",
    )(x2, orw, ocv, oat, w_bf)


def _gelu(x):
    return 0.5 * x * (1.0 + lax.erf(x * (2.0 ** -0.5)))


def _ffn_kernel(x_ref, p_ref, ng_ref, upg_ref, upv_ref, cw_ref, cb_ref, dn_ref, h_ref, h2_ref, pg_ref, pgate_ref,
                pproj_ref,
                xo_ref, tail_ref,
                xn_s, acc_s, hist_s, *, tm, nF, seq_tiles, seq_len):
    i = pl.program_id(0)
    f = pl.program_id(1)

    @pl.when(f == 0)
    def _():
        xn_s[...] = _rms(x_ref[...], ng_ref[...]).astype(BF16)
        acc_s[...] = jnp.zeros(acc_s.shape, F32)

    xn = xn_s[...]
    g = _dot(xn, upg_ref[...])
    val = _dot(xn, upv_ref[...])
    row = lax.broadcasted_iota(I32, g.shape, 0)
    if seq_tiles is not None:
        first = (i % seq_tiles) == 0
        hs = hist_s[f]
        h0 = h_ref[0]
        hm2 = jnp.where(first, h0[0:1, :], hs[SUBLANES - 2:SUBLANES - 1, :])
        hm1 = jnp.where(first, h0[1:2, :], hs[SUBLANES - 1:SUBLANES, :])
        g1 = jnp.where(row == 0, hm1, pltpu.roll(g, 1, 0))
        g2 = jnp.where(row == 0, hm2, jnp.where(row == 1, hm1, pltpu.roll(g, 2, 0)))
        slab = g[tm - SUBLANES:tm, :]
        hist_s[f] = slab
        tail_ref[0] = slab
    else:
        t = row % seq_len
        g1 = jnp.where(t == 0, 0.0, pltpu.roll(g, 1, 0)) + h_ref[...]
        g2 = jnp.where(t < 2, 0.0, pltpu.roll(g, 2, 0)) + h2_ref[...]
        tail_ref[...] = g
    cw = cw_ref[...]
    gate = cw[0:1, :] * g2 + cw[1:2, :] * g1 + cw[2:3, :] * g + cb_ref[...]
    act = (_gelu(gate) * val).astype(BF16)
    acc_s[...] += _dot(act, dn_ref[...])

    @pl.when(f == nF - 1)
    def _():
        x2 = x_ref[...] + acc_s[...]
        xn2 = _rms(x2, pg_ref[...]).astype(BF16)
        gate2 = _sigmoid(_dot(xn2, pgate_ref[...]))
        xo_ref[...] = x2 + _dot(p_ref[...].astype(BF16), pproj_ref[...]) * gate2


def _ffn(x2, p2, ffn0, P, tm, B, T):
    M, D = x2.shape
    tf = 256
    nF = D_FF // tf
    DP = p2.shape[1]
    whole = tm % T == 0
    if whole:
        seq_tiles, seq_len = None, T
        h1 = jnp.pad(ffn0[:, 1:2, :], ((0, 0), (0, T - 1), (0, 0))).reshape(M, D_FF)
        h2 = jnp.pad(ffn0, ((0, 0), (0, T - 2), (0, 0))).reshape(M, D_FF)
        h_specs = [pl.BlockSpec((tm, tf), lambda i, f: (i, f)), pl.BlockSpec((tm, tf), lambda i, f: (i, f))]
        tail_spec = pl.BlockSpec((tm, tf), lambda i, f: (i, f))
        tail_shape = jax.ShapeDtypeStruct((M, D_FF), F32)
    else:
        seq_tiles, seq_len = T // tm, T
        h1, h2 = ffn0, ffn0
        h_specs = [pl.BlockSpec((1, 2, tf), lambda i, f: (i // seq_tiles, 0, f)),
                   pl.BlockSpec((1, 2, tf), lambda i, f: (i // seq_tiles, 0, f))]
        tail_spec = pl.BlockSpec((1, SUBLANES, tf), lambda i, f: (i, 0, f))
        tail_shape = jax.ShapeDtypeStruct((M // tm, SUBLANES, D_FF), F32)
    cst = lambda shp: pl.BlockSpec(shp, lambda i, f: tuple(0 for _ in shp))
    xo, tail = pl.pallas_call(
        functools.partial(_ffn_kernel, tm=tm, nF=nF, seq_tiles=seq_tiles, seq_len=seq_len),
        grid=(M // tm, nF),
        in_specs=[pl.BlockSpec((tm, D), lambda i, f: (i, 0)),
                  pl.BlockSpec((tm, DP), lambda i, f: (i, 0)),
                  cst((1, D)),
                  pl.BlockSpec((D, tf), lambda i, f: (0, f)),
                  pl.BlockSpec((D, tf), lambda i, f: (0, nF + f)),
                  pl.BlockSpec((3, tf), lambda i, f: (0, f)),
                  pl.BlockSpec((1, tf), lambda i, f: (0, f)),
                  pl.BlockSpec((tf, D), lambda i, f: (f, 0))] + h_specs +
                 [cst((1, D)), cst((D, D)), cst((DP, D))],
        out_specs=[pl.BlockSpec((tm, D), lambda i, f: (i, 0)), tail_spec],
        out_shape=[jax.ShapeDtypeStruct((M, D), F32), tail_shape],
        scratch_shapes=[pltpu.VMEM((tm, D), BF16), pltpu.VMEM((tm, D), F32), pltpu.VMEM((nF, SUBLANES, tf), F32)],
        compiler_params=_cparams(("parallel", "arbitrary")),
        name="conv_ffn_ple",
    )(x2, p2, P["norm_ffn"], P["ffn_up"], P["ffn_up"], P["ffn_conv_w"], P["ffn_conv_b"], P["ffn_down"], h1, h2,
      P["ple_norm"], P["ple_gate"], P["ple_proj"])
    if whole:
        ffn_T = tail.reshape(B, T, D_FF)[:, T - 2:, :]
    else:
        ffn_T = tail.reshape(B, seq_tiles, SUBLANES, D_FF)[:, seq_tiles - 1, SUBLANES - 2:, :]
    return xo, ffn_T


def _seg_ones(n, seg):
    i = np.arange(n)
    return jnp.asarray((i[:, None] // seg) == (i[None, :] // seg), dtype=BF16)


def _hl(w):
    hi = w.astype(BF16)
    return hi, (w - hi.astype(F32)).astype(BF16)


def _layer_params(i, W):
    P = {}
    w_in = W["w_in"][i]
    D = w_in.shape[0]
    P["w_in"] = jnp.concatenate([w_in[:, :N_RWKV_COLS], jnp.zeros((D, 1024 - N_RWKV_COLS), F32),
                                 w_in[:, N_RWKV_COLS:], jnp.zeros((D, NZ - 128 - N_IN), F32)], axis=1).astype(BF16)
    row = lambda v: v.reshape(1, -1)
    P["norm_mix"] = row(W["norm_mix"][i])
    P["mu"] = row(W["mu_shift"][i])
    P["w0"] = row(W["w0"][i])
    P["a0"] = row(W["a0"][i])
    z = lambda r: jnp.zeros((r, C_R), F32)
    P["wl_hi"], P["wl_lo"] = _hl(jnp.concatenate([W["w_lora"][i], z(96)], axis=0))
    P["al_hi"], P["al_lo"] = _hl(jnp.concatenate([z(32), W["a_lora"][i], z(64)], axis=0))
    P["gl_hi"], P["gl_lo"] = _hl(jnp.concatenate([z(64), W["g_lora"][i]], axis=0))
    P["k_k"] = row(W["k_k"][i])
    P["k_a"] = row(W["k_a"][i])
    P["r_k"] = row(W["r_k"][i])
    P["lnx_g"] = row(W["lnx_g"][i])
    P["lnx_b"] = row(W["lnx_b"][i])
    P["conv_w"] = W["conv_w"][i]
    P["conv_b"] = row(W["conv_b"][i])
    P["conv_ln_g"] = row(W["conv_ln_g"][i])
    P["conv_ln_b"] = row(W["conv_ln_b"][i])
    P["q_norm8"] = row(jnp.tile(W["q_norm"][i], H_A))
    P["k_norm2"] = row(jnp.tile(W["k_norm"][i], N_KV))
    P["kidx_norm"] = row(W["kidx_norm"][i])
    P["w_out"] = W["w_out"][i].astype(BF16)
    P["norm_ffn"] = row(W["norm_ffn"][i])
    P["ffn_up"] = W["ffn_up"][i].astype(BF16)
    P["ffn_conv_w"] = W["ffn_conv_w"][i]
    P["ffn_conv_b"] = row(W["ffn_conv_b"][i])
    P["ffn_down"] = W["ffn_down"][i].astype(BF16)
    P["ple_norm"] = row(W["ple_norm"][i])
    P["ple_gate"] = W["ple_gate"][i].astype(BF16)
    P["ple_proj"] = W["ple_proj"][i].astype(BF16)
    P["gseg4"] = _seg_ones(C_R, HD)
    P["gseg2"] = _seg_ones(N_KV * HD, HD)
    P["gseg8"] = _seg_ones(C_A, HD)
    P["tri"] = jnp.asarray(np.arange(LANES)[:, None] <= np.arange(LANES)[None, :], dtype=BF16)
    P["ones"] = jnp.ones((LANES, LANES), BF16)
    return P


def _pick_tile(n, pref):
    t = min(n, pref)
    while n % t:
        t //= 2
    return t


def _layer(layer, x, p_i, st, P, att_fn, bb):
    B, T, D = x.shape
    M = B * T
    shift0, wkv0, conv0, ffn0 = st
    x2 = x.reshape(M, D)
    tm = _pick_tile(M, 512)
    z2 = _in_proj(x2, P["norm_mix"], P["w_in"], tm)
    z3 = z2.reshape(B, T, NZ)
    Tc = _pick_tile(T, 256)
    o_rw, wkv_T = _rwkv(z3, shift0.reshape(B, 1, N_RWKV_COLS), wkv0, P, bb, Tc)
    conv0p = jnp.pad(conv0, ((0, 0), (HIST - (CONV_W - 1), 0), (0, 0)))
    o_cv, conv_Tp = _conv(z3, conv0p, P, _pick_tile(T, 512))
    kn2, kin2 = _att_prep(z2, P, tm)
    kn3 = kn2.reshape(B, T, N_KV * HD)
    kin3 = kin2.reshape(B, T, D_IDX)
    o_at = att_fn(layer, z3, kn3, kin3, P)
    xa = _out_proj(x2, o_rw.reshape(M, C_R), o_cv.reshape(M, C_CONV), o_at.reshape(M, C_A), P["w_out"], tm)
    tmf = M if M <= 1024 else _pick_tile(T, 1024)
    xo, ffn_T = _ffn(xa, p_i.reshape(M, -1), ffn0, P, tmf, B, T)
    k_new = kn3.reshape(B, T, N_KV, HD)
    v_new = z3[:, :, ZC_V * LANES:(ZC_V + 1) * LANES].reshape(B, T, N_KV, HD)
    shift_T = z3[:, T - 1, :N_RWKV_COLS]
    conv_T = conv_Tp[:, HIST - (CONV_W - 1):, :]
    return xo.reshape(B, T, D), (k_new, v_new, kin3, wkv_T, shift_T, conv_T, ffn_T)


def kernel(x_prompt, x_sample, cache_k, cache_v, cache_kidx, state_wkv, state_shift, state_conv, state_ffn, page_table, p_prompt, p_sample, norm_mix, w_in, w_out, mu_shift, w0, w_lora, a0, a_lora, g_lora, k_k, k_a, r_k, lnx_g, lnx_b, conv_w, conv_b, conv_ln_g, conv_ln_b, q_norm, k_norm, kidx_norm, rel_bias, norm_ffn, ffn_up, ffn_conv_w, ffn_conv_b, ffn_down, ple_norm, ple_proj, ple_gate):
    W = dict(norm_mix=norm_mix, w_in=w_in, w_out=w_out, mu_shift=mu_shift, w0=w0, w_lora=w_lora, a0=a0,
             a_lora=a_lora, g_lora=g_lora, k_k=k_k, k_a=k_a, r_k=r_k, lnx_g=lnx_g, lnx_b=lnx_b,
             conv_w=conv_w, conv_b=conv_b, conv_ln_g=conv_ln_g, conv_ln_b=conv_ln_b, q_norm=q_norm,
             k_norm=k_norm, kidx_norm=kidx_norm, norm_ffn=norm_ffn, ffn_up=ffn_up,
             ffn_conv_w=ffn_conv_w, ffn_conv_b=ffn_conv_b, ffn_down=ffn_down, ple_norm=ple_norm,
             ple_proj=ple_proj, ple_gate=ple_gate)
    depth = w_in.shape[0]
    params = [_layer_params(i, W) for i in range(depth)]

    B, T, D = x_prompt.shape
    tq = ATT_TQ if T % ATT_TQ == 0 else LANES
    topk_p = min(TOPK_MAX, T // 4)
    t_ = np.arange(LANES)[:, None]
    s_ = np.arange(LANES)[None, :]
    tiles = np.stack([_rel_bucket_np(np.maximum(t_ - s_, 0)), _rel_bucket_np(LANES + t_ - s_),
                      np.full((LANES, LANES), N_BUCKETS - 1, np.int32)])
    idx_p = np.broadcast_to(tiles[:, None], (3, H_A, LANES, LANES)).reshape(3 * H_A, LANES, LANES)
    bias_p = _bias_expand(rel_bias, jnp.asarray(idx_p)).reshape(3, H_A, LANES, LANES)

    bias_rel = bias_p - bias_p[2:3]

    def att_p(layer, z3, kn3, kin3, P):
        return _att_prompt(z3, kn3, kin3, P, bias_rel, topk_p, tq)

    st0 = (jnp.zeros((B, N_RWKV_COLS), F32), jnp.zeros((B, H_R, HD, HD), F32),
           jnp.zeros((B, CONV_W - 1, C_CONV), F32), jnp.zeros((B, 2, D_FF), F32))
    x = x_prompt
    new_p = []
    for i in range(depth):
        x, new = _layer(i, x, p_prompt[i], st0, params[i], att_p, B)
        new_p.append(new)
    y_prompt = x

    Bs, Ts, _ = x_sample.shape
    npages = page_table.shape[1]
    psz = cache_k.shape[2]
    past_len = npages * psz
    topk_s = min(TOPK_MAX, (past_len + Ts) // 4)
    Ls = past_len + psz
    qpos = past_len + np.arange(Ts)[:, None]
    bk = _rel_bucket_np(np.maximum(qpos - np.arange(Ls)[None, :], 0))
    idx_s = np.broadcast_to(bk[None], (H_A, Ts, Ls))
    bias_s = _bias_expand(rel_bias, jnp.asarray(idx_s)).reshape(N_KV, G_A * Ts, Ls)
    ckT = jnp.transpose(cache_k, (0, 1, 3, 4, 2))
    cvT = jnp.transpose(cache_v, (0, 1, 3, 4, 2))
    ckiT = jnp.transpose(cache_kidx, (0, 1, 3, 2))

    def att_s(layer, z3, kn3, kin3, P):
        return _att_sample(layer, page_table, ckT, cvT, ckiT, z3, kn3, kin3, P, bias_s, topk_s)

    x = x_sample
    new_s = []
    for i in range(depth):
        st = (state_shift[i], state_wkv[i], state_conv[i], state_ffn[i])
        x, new = _layer(i, x, p_sample[i], st, params[i], att_s, 4)
        new_s.append(new)
    y_sample = x

    stack = lambda lst: tuple(jnp.stack(t) for t in zip(*lst))
    return (y_prompt, y_sample) + stack(new_p) + stack(new_s)
```
